```python
import jax, jax.numpy as jnp
from jax import lax
import numpy as np

D_MODEL = 2048
BATCH = 8
SEQ = 8192
DEPTH = 4

HEAD_DIM = 128
SG_WIDTH = D_MODEL // 4
SG_HEADS = SG_WIDTH // HEAD_DIM
SG_CHUNK = 128
POOL_WINDOWS = (2, 4, 8, 16)
POOL_GROUPS = len(POOL_WINDOWS)
POOL_WIDTH = D_MODEL // 4
POOL_CH = POOL_WIDTH // POOL_GROUPS
NA_WIDTH = D_MODEL // 2
NA_HEADS = NA_WIDTH // HEAD_DIM
NA_KH = 8
NA_KW = 16
GRID_W = 64
MIX_WIDTH = SG_WIDTH + POOL_WIDTH + NA_WIDTH
IN_COLS = 2 * SG_WIDTH + POOL_WIDTH + 3 * NA_WIDTH
D_FF = 11 * D_MODEL // 4
EPS = 1e-6
NEG = -1e30

kernel_name = "hybrid_gmlp_pool_natten_macaron_encoder"


def rms_norm(x, g):
    xf = x.astype(jnp.float32)
    y = xf * lax.rsqrt(jnp.mean(xf * xf, axis=-1, keepdims=True) + EPS)
    return (y * g.astype(jnp.float32)).astype(x.dtype)


def swiglu(h, w_gate, w_up, w_down):
    return (jax.nn.silu(h @ w_gate) * (h @ w_up)) @ w_down


def spatial_gating(zu, zv, g, w_s, b_s):
    B, S, _ = zu.shape
    u = jax.nn.gelu(zu, approximate=False)
    v = jax.nn.gelu(zv, approximate=False).reshape(B, S // SG_CHUNK, SG_CHUNK, SG_HEADS, HEAD_DIM)
    v = rms_norm(v, g.reshape(SG_HEADS, HEAD_DIM))
    mixed = jnp.einsum('hpq,bnqhd->bnphd', w_s, v) + b_s.T[None, None, :, :, None]
    return u * mixed.reshape(B, S, SG_WIDTH)


def multiscale_pool(p, w, scale):
    B, S, _ = p.shape
    pf = p.astype(jnp.float32).reshape(B, S, POOL_GROUPS, POOL_CH)
    cs = jnp.concatenate([jnp.zeros((B, 1, POOL_GROUPS, POOL_CH), jnp.float32),
                          jnp.cumsum(pf, axis=1)], axis=1)
    t = jnp.arange(S)
    outs = []
    for g, win in enumerate(POOL_WINDOWS):
        lo = jnp.clip(t - win // 2, 0, S)
        hi = jnp.clip(t + win // 2, 0, S)
        cnt = (hi - lo).astype(jnp.float32)
        mean = (cs[:, hi, g] - cs[:, lo, g]) / cnt[None, :, None]
        outs.append(mean - pf[:, :, g])
    d = jnp.stack(outs, axis=2).astype(p.dtype)
    y = jnp.einsum('bsgc,gcd->bsgd', d, w) * scale.reshape(POOL_GROUPS, POOL_CH)
    return y.reshape(B, S, POOL_WIDTH)


def neighbourhood_attention(q, k, v, rpb):
    B, S, H, Dh = q.shape
    rows = S // GRID_W
    kh = min(NA_KH, rows)
    qg = q.reshape(B, rows, GRID_W, H, Dh)
    kg = k.reshape(B, rows, GRID_W, H, Dh)
    vg = v.reshape(B, rows, GRID_W, H, Dh)
    col = jnp.arange(GRID_W)
    col_start = jnp.clip(col - NA_KW // 2, 0, GRID_W - NA_KW)
    col_in = (col[None, :] >= col_start[:, None]) & (col[None, :] < col_start[:, None] + NA_KW)
    dc_idx = jnp.clip(col[None, :] - col[:, None] + NA_KW - 1, 0, 2 * NA_KW - 2)
    rpb_col = rpb.astype(jnp.float32)[:, :, dc_idx]
    scale = Dh ** -0.5

    def one_row(r):
        sr = jnp.clip(r - kh // 2, 0, rows - kh)
        q_r = lax.dynamic_index_in_dim(qg, r, axis=1, keepdims=False)
        k_r = lax.dynamic_slice_in_dim(kg, sr, kh, axis=1)
        v_r = lax.dynamic_slice_in_dim(vg, sr, kh, axis=1)
        dr = sr + jnp.arange(kh) - r + NA_KH - 1
        bias = jnp.take(rpb_col, dr, axis=1).transpose(0, 2, 1, 3)
        s = jnp.einsum('bqhd,bjkhd->bhqjk', q_r, k_r).astype(jnp.float32) * scale + bias[None]
        s = jnp.where(col_in[:, None, :], s, NEG)
        pr = jax.nn.softmax(s.reshape(B, H, GRID_W, kh * GRID_W), axis=-1)
        pr = pr.reshape(B, H, GRID_W, kh, GRID_W).astype(v.dtype)
        return jnp.einsum('bhqjk,bjkhd->bqhd', pr, v_r)

    out = lax.map(one_row, jnp.arange(rows))
    return out.transpose(1, 0, 2, 3, 4).reshape(B, S, H, Dh)


def _fwd_setup_inputs(seed: int = 0) -> dict:
    key = jax.random.key(seed)
    ks = jax.random.split(key, 20)
    f32 = jnp.float32
    nrm = lambda k, shape, s: jax.random.normal(k, shape, f32) * s
    L = DEPTH
    return {
        "x": jax.random.normal(ks[0], (BATCH, SEQ, D_MODEL), f32),
        "ffn1_norm": 1.0 + nrm(ks[1], (L, D_MODEL), 0.05),
        "ffn1_w_gate": nrm(ks[2], (L, D_MODEL, D_FF), D_MODEL ** -0.5),
        "ffn1_w_up": nrm(ks[3], (L, D_MODEL, D_FF), D_MODEL ** -0.5),
        "ffn1_w_down": nrm(ks[4], (L, D_FF, D_MODEL), D_FF ** -0.5),
        "mix_norm": 1.0 + nrm(ks[5], (L, D_MODEL), 0.05),
        "w_in": nrm(ks[6], (L, D_MODEL, IN_COLS), D_MODEL ** -0.5),
        "sg_norm": 1.0 + nrm(ks[7], (L, SG_WIDTH), 0.05),
        "sg_w": nrm(ks[8], (L, SG_HEADS, SG_CHUNK, SG_CHUNK), SG_CHUNK ** -0.5),
        "sg_b": 1.0 + nrm(ks[9], (L, SG_HEADS, SG_CHUNK), 0.05),
        "pool_w": nrm(ks[10], (L, POOL_GROUPS, POOL_CH, POOL_CH), POOL_CH ** -0.5),
        "pool_scale": 1.0 + nrm(ks[11], (L, POOL_WIDTH), 0.1),
        "na_rpb": nrm(ks[12], (L, NA_HEADS, 2 * NA_KH - 1, 2 * NA_KW - 1), 0.1),
        "w_out": nrm(ks[13], (L, MIX_WIDTH, D_MODEL), MIX_WIDTH ** -0.5),
        "ffn2_norm": 1.0 + nrm(ks[14], (L, D_MODEL), 0.05),
        "ffn2_w_gate": nrm(ks[15], (L, D_MODEL, D_FF), D_MODEL ** -0.5),
        "ffn2_w_up": nrm(ks[16], (L, D_MODEL, D_FF), D_MODEL ** -0.5),
        "ffn2_w_down": nrm(ks[17], (L, D_FF, D_MODEL), D_FF ** -0.5),
        "final_norm": 1.0 + nrm(ks[18], (D_MODEL,), 0.05),
    }


def _fwd_reference(x, ffn1_norm, ffn1_w_gate, ffn1_w_up, ffn1_w_down, mix_norm, w_in,
              sg_norm, sg_w, sg_b, pool_w, pool_scale, na_rpb, w_out,
              ffn2_norm, ffn2_w_gate, ffn2_w_up, ffn2_w_down, final_norm):
    B, S, _ = x.shape
    splits = [SG_WIDTH, 2 * SG_WIDTH, 2 * SG_WIDTH + POOL_WIDTH,
              2 * SG_WIDTH + POOL_WIDTH + NA_WIDTH, 2 * SG_WIDTH + POOL_WIDTH + 2 * NA_WIDTH]
    for l in range(DEPTH):
        x = x + 0.5 * swiglu(rms_norm(x, ffn1_norm[l]), ffn1_w_gate[l], ffn1_w_up[l], ffn1_w_down[l])
        h = rms_norm(x, mix_norm[l])
        z = h @ w_in[l]
        zu, zv, zp, zq, zk, zvv = jnp.split(z, splits, axis=-1)
        a = spatial_gating(zu, zv, sg_norm[l], sg_w[l], sg_b[l])
        bp = multiscale_pool(zp, pool_w[l], pool_scale[l])
        c = neighbourhood_attention(zq.reshape(B, S, NA_HEADS, HEAD_DIM),
                                    zk.reshape(B, S, NA_HEADS, HEAD_DIM),
                                    zvv.reshape(B, S, NA_HEADS, HEAD_DIM),
                                    na_rpb[l]).reshape(B, S, NA_WIDTH)
        x = x + jnp.concatenate([a, bp, c], axis=-1) @ w_out[l]
        x = x + 0.5 * swiglu(rms_norm(x, ffn2_norm[l]), ffn2_w_gate[l], ffn2_w_up[l], ffn2_w_down[l])
    return rms_norm(x, final_norm)


import jax as _jax
import jax.numpy as _jnp

TWIN_FORMAT = 'train_step'
FWD_PARAMS = ['x', 'ffn1_norm', 'ffn1_w_gate', 'ffn1_w_up', 'ffn1_w_down', 'mix_norm', 'w_in', 'sg_norm', 'sg_w', 'sg_b', 'pool_w', 'pool_scale', 'na_rpb', 'w_out', 'ffn2_norm', 'ffn2_w_gate', 'ffn2_w_up', 'ffn2_w_down', 'final_norm']
TWIN_WEIGHTS = ['ffn1_norm', 'ffn1_w_gate', 'ffn1_w_up', 'ffn1_w_down', 'mix_norm', 'w_in', 'sg_norm', 'sg_w', 'sg_b', 'pool_w', 'pool_scale', 'na_rpb', 'w_out', 'ffn2_norm', 'ffn2_w_gate', 'ffn2_w_up', 'ffn2_w_down', 'final_norm']
TWIN_DIFF_INPUT = 'x'
TWIN_INPUTS = ['x', 'ffn1_norm', 'ffn1_w_gate', 'ffn1_w_up', 'ffn1_w_down', 'mix_norm', 'w_in', 'sg_norm', 'sg_w', 'sg_b', 'pool_w', 'pool_scale', 'na_rpb', 'w_out', 'ffn2_norm', 'ffn2_w_gate', 'ffn2_w_up', 'ffn2_w_down', 'final_norm', 'loss_target', 'm_ffn1_norm', 'm_ffn1_w_gate', 'm_ffn1_w_up', 'm_ffn1_w_down', 'm_mix_norm', 'm_w_in', 'm_sg_norm', 'm_sg_w', 'm_sg_b', 'm_pool_w', 'm_pool_scale', 'm_na_rpb', 'm_w_out', 'm_ffn2_norm', 'm_ffn2_w_gate', 'm_ffn2_w_up', 'm_ffn2_w_down', 'm_final_norm', 'v_ffn1_norm', 'v_ffn1_w_gate', 'v_ffn1_w_up', 'v_ffn1_w_down', 'v_mix_norm', 'v_w_in', 'v_sg_norm', 'v_sg_w', 'v_sg_b', 'v_pool_w', 'v_pool_scale', 'v_na_rpb', 'v_w_out', 'v_ffn2_norm', 'v_ffn2_w_gate', 'v_ffn2_w_up', 'v_ffn2_w_down', 'v_final_norm']
TWIN_OUTPUTS = ['loss', 'grad_x', 'grad_ffn1_norm', 'grad_ffn1_w_gate', 'grad_ffn1_w_up', 'grad_ffn1_w_down', 'grad_mix_norm', 'grad_w_in', 'grad_sg_norm', 'grad_sg_w', 'grad_sg_b', 'grad_pool_w', 'grad_pool_scale', 'grad_na_rpb', 'grad_w_out', 'grad_ffn2_norm', 'grad_ffn2_w_gate', 'grad_ffn2_w_up', 'grad_ffn2_w_down', 'grad_final_norm', 'delta_ffn1_norm', 'delta_ffn1_w_gate', 'delta_ffn1_w_up', 'delta_ffn1_w_down', 'delta_mix_norm', 'delta_w_in', 'delta_sg_norm', 'delta_sg_w', 'delta_sg_b', 'delta_pool_w', 'delta_pool_scale', 'delta_na_rpb', 'delta_w_out', 'delta_ffn2_norm', 'delta_ffn2_w_gate', 'delta_ffn2_w_up', 'delta_ffn2_w_down', 'delta_final_norm', 'new_m_ffn1_norm', 'new_m_ffn1_w_gate', 'new_m_ffn1_w_up', 'new_m_ffn1_w_down', 'new_m_mix_norm', 'new_m_w_in', 'new_m_sg_norm', 'new_m_sg_w', 'new_m_sg_b', 'new_m_pool_w', 'new_m_pool_scale', 'new_m_na_rpb', 'new_m_w_out', 'new_m_ffn2_norm', 'new_m_ffn2_w_gate', 'new_m_ffn2_w_up', 'new_m_ffn2_w_down', 'new_m_final_norm', 'new_v_ffn1_norm', 'new_v_ffn1_w_gate', 'new_v_ffn1_w_up', 'new_v_ffn1_w_down', 'new_v_mix_norm', 'new_v_w_in', 'new_v_sg_norm', 'new_v_sg_w', 'new_v_sg_b', 'new_v_pool_w', 'new_v_pool_scale', 'new_v_na_rpb', 'new_v_w_out', 'new_v_ffn2_norm', 'new_v_ffn2_w_gate', 'new_v_ffn2_w_up', 'new_v_ffn2_w_down', 'new_v_final_norm']
TWIN_LEAF_KINDS = {'loss': 'loss', 'grad_x': 'grad_x', 'grad_ffn1_norm': 'grad_w', 'grad_ffn1_w_gate': 'grad_w', 'grad_ffn1_w_up': 'grad_w', 'grad_ffn1_w_down': 'grad_w', 'grad_mix_norm': 'grad_w', 'grad_w_in': 'grad_w', 'grad_sg_norm': 'grad_w', 'grad_sg_w': 'grad_w', 'grad_sg_b': 'grad_w', 'grad_pool_w': 'grad_w', 'grad_pool_scale': 'grad_w', 'grad_na_rpb': 'grad_w', 'grad_w_out': 'grad_w', 'grad_ffn2_norm': 'grad_w', 'grad_ffn2_w_gate': 'grad_w', 'grad_ffn2_w_up': 'grad_w', 'grad_ffn2_w_down': 'grad_w', 'grad_final_norm': 'grad_w', 'delta_ffn1_norm': 'delta_w', 'delta_ffn1_w_gate': 'delta_w', 'delta_ffn1_w_up': 'delta_w', 'delta_ffn1_w_down': 'delta_w', 'delta_mix_norm': 'delta_w', 'delta_w_in': 'delta_w', 'delta_sg_norm': 'delta_w', 'delta_sg_w': 'delta_w', 'delta_sg_b': 'delta_w', 'delta_pool_w': 'delta_w', 'delta_pool_scale': 'delta_w', 'delta_na_rpb': 'delta_w', 'delta_w_out': 'delta_w', 'delta_ffn2_norm': 'delta_w', 'delta_ffn2_w_gate': 'delta_w', 'delta_ffn2_w_up': 'delta_w', 'delta_ffn2_w_down': 'delta_w', 'delta_final_norm': 'delta_w', 'new_m_ffn1_norm': 'new_m', 'new_m_ffn1_w_gate': 'new_m', 'new_m_ffn1_w_up': 'new_m', 'new_m_ffn1_w_down': 'new_m', 'new_m_mix_norm': 'new_m', 'new_m_w_in': 'new_m', 'new_m_sg_norm': 'new_m', 'new_m_sg_w': 'new_m', 'new_m_sg_b': 'new_m', 'new_m_pool_w': 'new_m', 'new_m_pool_scale': 'new_m', 'new_m_na_rpb': 'new_m', 'new_m_w_out': 'new_m', 'new_m_ffn2_norm': 'new_m', 'new_m_ffn2_w_gate': 'new_m', 'new_m_ffn2_w_up': 'new_m', 'new_m_ffn2_w_down': 'new_m', 'new_m_final_norm': 'new_m', 'new_v_ffn1_norm': 'new_v', 'new_v_ffn1_w_gate': 'new_v', 'new_v_ffn1_w_up': 'new_v', 'new_v_ffn1_w_down': 'new_v', 'new_v_mix_norm': 'new_v', 'new_v_w_in': 'new_v', 'new_v_sg_norm': 'new_v', 'new_v_sg_w': 'new_v', 'new_v_sg_b': 'new_v', 'new_v_pool_w': 'new_v', 'new_v_pool_scale': 'new_v', 'new_v_na_rpb': 'new_v', 'new_v_w_out': 'new_v', 'new_v_ffn2_norm': 'new_v', 'new_v_ffn2_w_gate': 'new_v', 'new_v_ffn2_w_up': 'new_v', 'new_v_ffn2_w_down': 'new_v', 'new_v_final_norm': 'new_v'}


def _forward(args):
    return _fwd_reference(*[args[k] for k in FWD_PARAMS])


def _output_shape():
    def fwd():
        inp = _fwd_setup_inputs(0)
        return _fwd_reference(*[inp[k] for k in FWD_PARAMS])
    out = _jax.eval_shape(fwd)
    return out.shape, out.dtype

N_MICROBATCH = 1
ADAM_LR = 0.001
ADAM_B1 = 0.9
ADAM_B2 = 0.999
ADAM_EPS = 1e-08
ADAM_WD = 0.01
ADAM_STEP = 10
PER_EXAMPLE_BATCH_AXIS = {'x': 0, 'loss_target': 0}
SHARED_INPUTS = []
_WEIGHT_DTYPES = {'ffn1_norm': _jnp.float32, 'ffn1_w_gate': _jnp.float32, 'ffn1_w_up': _jnp.float32, 'ffn1_w_down': _jnp.float32, 'mix_norm': _jnp.float32, 'w_in': _jnp.float32, 'sg_norm': _jnp.float32, 'sg_w': _jnp.float32, 'sg_b': _jnp.float32, 'pool_w': _jnp.float32, 'pool_scale': _jnp.float32, 'na_rpb': _jnp.float32, 'w_out': _jnp.float32, 'ffn2_norm': _jnp.float32, 'ffn2_w_gate': _jnp.float32, 'ffn2_w_up': _jnp.float32, 'ffn2_w_down': _jnp.float32, 'final_norm': _jnp.float32}
MOMENT_SCALE = {'ffn1_norm': 5.085693e-02, 'ffn1_w_gate': 2.147804e-02, 'ffn1_w_up': 2.079471e-02, 'ffn1_w_down': 3.454138e-02, 'mix_norm': 7.648326e-02, 'w_in': 5.052774e-02, 'sg_norm': 6.429280e-02, 'sg_w': 6.467729e-02, 'sg_b': 6.662931e-02, 'pool_w': 8.868971e-02, 'pool_scale': 9.141761e-02, 'na_rpb': 9.160758e-03, 'w_out': 7.426439e-02, 'ffn2_norm': 4.024497e-02, 'ffn2_w_gate': 1.736172e-02, 'ffn2_w_up': 1.701058e-02, 'ffn2_w_down': 2.823643e-02, 'final_norm': 3.212971e+01}


def _to_microbatches(a, axis):
    t = _jnp.moveaxis(a, axis, 0)
    t = t.reshape((N_MICROBATCH, t.shape[0] // N_MICROBATCH) + t.shape[1:])
    return _jnp.moveaxis(t, 1, axis + 1)


def setup_inputs(seed: int = 0) -> dict:
    inp = _fwd_setup_inputs(seed)
    key = _jax.random.fold_in(_jax.random.key(seed), 7919)
    shape, _ = _output_shape()
    out = dict(inp)
    out["loss_target"] = _jax.random.normal(_jax.random.fold_in(key, 0), shape, _jnp.float32)
    for i, name in enumerate(TWIN_WEIGHTS):
        w = inp[name].astype(_jnp.float32)
        if MOMENT_SCALE is None:
            s = _jnp.sqrt(_jnp.mean(_jnp.square(w)) + 1e-30)
        else:
            s = MOMENT_SCALE[name]
        km, kv = _jax.random.split(_jax.random.fold_in(key, i + 1))
        out[name] = w
        out["m_" + name] = s * _jax.random.normal(km, w.shape, _jnp.float32)
        out["v_" + name] = (s * s) * _jax.random.uniform(kv, w.shape, _jnp.float32, 0.5, 1.5)
    if N_MICROBATCH > 1:
        for name, axis in PER_EXAMPLE_BATCH_AXIS.items():
            out[name] = _to_microbatches(out[name], axis)
    return {'x': out['x'], 'ffn1_norm': out['ffn1_norm'], 'ffn1_w_gate': out['ffn1_w_gate'], 'ffn1_w_up': out['ffn1_w_up'], 'ffn1_w_down': out['ffn1_w_down'], 'mix_norm': out['mix_norm'], 'w_in': out['w_in'], 'sg_norm': out['sg_norm'], 'sg_w': out['sg_w'], 'sg_b': out['sg_b'], 'pool_w': out['pool_w'], 'pool_scale': out['pool_scale'], 'na_rpb': out['na_rpb'], 'w_out': out['w_out'], 'ffn2_norm': out['ffn2_norm'], 'ffn2_w_gate': out['ffn2_w_gate'], 'ffn2_w_up': out['ffn2_w_up'], 'ffn2_w_down': out['ffn2_w_down'], 'final_norm': out['final_norm'], 'loss_target': out['loss_target'], 'm_ffn1_norm': out['m_ffn1_norm'], 'm_ffn1_w_gate': out['m_ffn1_w_gate'], 'm_ffn1_w_up': out['m_ffn1_w_up'], 'm_ffn1_w_down': out['m_ffn1_w_down'], 'm_mix_norm': out['m_mix_norm'], 'm_w_in': out['m_w_in'], 'm_sg_norm': out['m_sg_norm'], 'm_sg_w': out['m_sg_w'], 'm_sg_b': out['m_sg_b'], 'm_pool_w': out['m_pool_w'], 'm_pool_scale': out['m_pool_scale'], 'm_na_rpb': out['m_na_rpb'], 'm_w_out': out['m_w_out'], 'm_ffn2_norm': out['m_ffn2_norm'], 'm_ffn2_w_gate': out['m_ffn2_w_gate'], 'm_ffn2_w_up': out['m_ffn2_w_up'], 'm_ffn2_w_down': out['m_ffn2_w_down'], 'm_final_norm': out['m_final_norm'], 'v_ffn1_norm': out['v_ffn1_norm'], 'v_ffn1_w_gate': out['v_ffn1_w_gate'], 'v_ffn1_w_up': out['v_ffn1_w_up'], 'v_ffn1_w_down': out['v_ffn1_w_down'], 'v_mix_norm': out['v_mix_norm'], 'v_w_in': out['v_w_in'], 'v_sg_norm': out['v_sg_norm'], 'v_sg_w': out['v_sg_w'], 'v_sg_b': out['v_sg_b'], 'v_pool_w': out['v_pool_w'], 'v_pool_scale': out['v_pool_scale'], 'v_na_rpb': out['v_na_rpb'], 'v_w_out': out['v_w_out'], 'v_ffn2_norm': out['v_ffn2_norm'], 'v_ffn2_w_gate': out['v_ffn2_w_gate'], 'v_ffn2_w_up': out['v_ffn2_w_up'], 'v_ffn2_w_down': out['v_ffn2_w_down'], 'v_final_norm': out['v_final_norm']}


def _loss(weights, diff, rest, loss_target):
    with _jax.named_scope("forward"):
        args = {**rest, TWIN_DIFF_INPUT: diff, **{k: w.astype(_WEIGHT_DTYPES[k]) for k, w in weights.items()}}
        y = _forward(args)
    with _jax.named_scope("loss_head"):
        err = _jnp.square(y.astype(_jnp.float32) - loss_target)
        return 0.5 * _jnp.sum(_jnp.mean(err, axis=-1)) if err.ndim else 0.5 * err


def _adamw(w, g, m, v):
    m = ADAM_B1 * m + (1.0 - ADAM_B1) * g
    v = ADAM_B2 * v + (1.0 - ADAM_B2) * _jnp.square(g)
    m_hat = m / (1.0 - ADAM_B1 ** ADAM_STEP)
    v_hat = v / (1.0 - ADAM_B2 ** ADAM_STEP)
    delta = -ADAM_LR * (m_hat / (_jnp.sqrt(v_hat) + ADAM_EPS) + ADAM_WD * w)
    return delta, m, v


def reference(x, ffn1_norm, ffn1_w_gate, ffn1_w_up, ffn1_w_down, mix_norm, w_in, sg_norm, sg_w, sg_b, pool_w, pool_scale, na_rpb, w_out, ffn2_norm, ffn2_w_gate, ffn2_w_up, ffn2_w_down, final_norm, loss_target, m_ffn1_norm, m_ffn1_w_gate, m_ffn1_w_up, m_ffn1_w_down, m_mix_norm, m_w_in, m_sg_norm, m_sg_w, m_sg_b, m_pool_w, m_pool_scale, m_na_rpb, m_w_out, m_ffn2_norm, m_ffn2_w_gate, m_ffn2_w_up, m_ffn2_w_down, m_final_norm, v_ffn1_norm, v_ffn1_w_gate, v_ffn1_w_up, v_ffn1_w_down, v_mix_norm, v_w_in, v_sg_norm, v_sg_w, v_sg_b, v_pool_w, v_pool_scale, v_na_rpb, v_w_out, v_ffn2_norm, v_ffn2_w_gate, v_ffn2_w_up, v_ffn2_w_down, v_final_norm):
    given = dict(x=x, ffn1_norm=ffn1_norm, ffn1_w_gate=ffn1_w_gate, ffn1_w_up=ffn1_w_up, ffn1_w_down=ffn1_w_down, mix_norm=mix_norm, w_in=w_in, sg_norm=sg_norm, sg_w=sg_w, sg_b=sg_b, pool_w=pool_w, pool_scale=pool_scale, na_rpb=na_rpb, w_out=w_out, ffn2_norm=ffn2_norm, ffn2_w_gate=ffn2_w_gate, ffn2_w_up=ffn2_w_up, ffn2_w_down=ffn2_w_down, final_norm=final_norm, loss_target=loss_target, m_ffn1_norm=m_ffn1_norm, m_ffn1_w_gate=m_ffn1_w_gate, m_ffn1_w_up=m_ffn1_w_up, m_ffn1_w_down=m_ffn1_w_down, m_mix_norm=m_mix_norm, m_w_in=m_w_in, m_sg_norm=m_sg_norm, m_sg_w=m_sg_w, m_sg_b=m_sg_b, m_pool_w=m_pool_w, m_pool_scale=m_pool_scale, m_na_rpb=m_na_rpb, m_w_out=m_w_out, m_ffn2_norm=m_ffn2_norm, m_ffn2_w_gate=m_ffn2_w_gate, m_ffn2_w_up=m_ffn2_w_up, m_ffn2_w_down=m_ffn2_w_down, m_final_norm=m_final_norm, v_ffn1_norm=v_ffn1_norm, v_ffn1_w_gate=v_ffn1_w_gate, v_ffn1_w_up=v_ffn1_w_up, v_ffn1_w_down=v_ffn1_w_down, v_mix_norm=v_mix_norm, v_w_in=v_w_in, v_sg_norm=v_sg_norm, v_sg_w=v_sg_w, v_sg_b=v_sg_b, v_pool_w=v_pool_w, v_pool_scale=v_pool_scale, v_na_rpb=v_na_rpb, v_w_out=v_w_out, v_ffn2_norm=v_ffn2_norm, v_ffn2_w_gate=v_ffn2_w_gate, v_ffn2_w_up=v_ffn2_w_up, v_ffn2_w_down=v_ffn2_w_down, v_final_norm=v_final_norm)
    weights = {n: given[n] for n in TWIN_WEIGHTS}
    shared = {n: given[n] for n in SHARED_INPUTS}
    per_example = {n: given[n] for n in ['x']}
    grad_fn = _jax.value_and_grad(_loss, argnums=(0, 1))

    def one_microbatch(ex, loss_target):
        ex = dict(ex)
        diff = ex.pop(TWIN_DIFF_INPUT)
        return grad_fn(weights, diff, {**shared, **ex}, loss_target)

    if N_MICROBATCH == 1:
        loss, (grad_w, grad_x) = one_microbatch(per_example, given["loss_target"])
    else:
        def body(carry, xs):
            loss_sum, grad_sum = carry
            l_k, (gw_k, gx_k) = one_microbatch(xs[0], xs[1])
            with _jax.named_scope("update"):
                return (loss_sum + l_k, _jax.tree.map(_jnp.add, grad_sum, gw_k)), gx_k

        init = (_jnp.zeros((), _jnp.float32), _jax.tree.map(_jnp.zeros_like, weights))
        (loss, grad_w), grad_x = _jax.lax.scan(body, init, (per_example, given["loss_target"]))
    with _jax.named_scope("update"):
        delta_w, new_m, new_v = {}, {}, {}
        for n in TWIN_WEIGHTS:
            delta_w[n], new_m[n], new_v[n] = _adamw(weights[n], grad_w[n], given["m_" + n], given["v_" + n])
    return (loss, grad_x, *[grad_w[n] for n in TWIN_WEIGHTS], *[delta_w[n] for n in TWIN_WEIGHTS],
            *[new_m[n] for n in TWIN_WEIGHTS], *[new_v[n] for n in TWIN_WEIGHTS])
```

```python
import functools
import math

import jax
import jax.numpy as jnp
from jax import lax
from jax.experimental import pallas as pl
from jax.experimental.pallas import tpu as pltpu

F32 = jnp.float32
BF16 = jnp.bfloat16
EPS = 1e-6
NEG = -1e30

HEAD_DIM = 128
SG_WIDTH = 512
SG_HEADS = 4
SG_CHUNK = 128
POOL_WINDOWS = (2, 4, 8, 16)
POOL_WIDTH = 512
POOL_HALO = 128
NA_WIDTH = 1024
NA_HEADS = 8
NA_KH = 8
NA_KW = 16
GRID_W = 64
Z_COLS = 2 * SG_WIDTH + POOL_WIDTH + 3 * NA_WIDTH
Q_OFF = 2 * SG_WIDTH + POOL_WIDTH
K_OFF = Q_OFF + NA_WIDTH
V_OFF = K_OFF + NA_WIDTH

ADAM_LR = 0.001
ADAM_B1 = 0.9
ADAM_B2 = 0.999
ADAM_EPS = 1e-08
ADAM_WD = 0.01
ADAM_STEP = 10

N_DEV = 8
MESH_AXES = ("x", "y", "c")
MESH = pl.DeviceIdType.MESH

NT_DIMS = (((1,), (1,)), ((), ()))
TN_DIMS = (((0,), (0,)), ((), ()))


def _pick(n, pref, mult):
    best = None
    t = mult
    while t <= min(n, pref):
        if n % t == 0:
            best = t
        t += mult
    return n if best is None else best


def _params(*sem):
    return pltpu.CompilerParams(dimension_semantics=sem)


def _gelu(x):
    return 0.5 * x * (1.0 + lax.erf(x * (1.0 / math.sqrt(2.0))))


def _gelu_grad(x):
    cdf = 0.5 * (1.0 + lax.erf(x * (1.0 / math.sqrt(2.0))))
    pdf = jnp.exp(-0.5 * x * x) * (1.0 / math.sqrt(2.0 * math.pi))
    return cdf + x * pdf


def _rms(x):
    return lax.rsqrt(jnp.mean(x * x, axis=-1, keepdims=True) + EPS)


def _norm_bwd(dh, x, g):
    r = _rms(x)
    w = dh * g
    dx = r * w - x * (r * r * r) * jnp.mean(w * x, axis=-1, keepdims=True)
    dg = jnp.sum(dh * (x * r), axis=0, keepdims=True)
    return dx, dg


def _rmsnorm(x, g):
    T, D = x.shape
    tm = _pick(T, 512, 16)

    def body(x_ref, g_ref, o_ref):
        xv = x_ref[...]
        o_ref[...] = (xv * _rms(xv) * g_ref[...]).astype(BF16)

    return pl.pallas_call(
        body, name="rmsnorm", grid=(T // tm,),
        in_specs=[pl.BlockSpec((tm, D), lambda i: (i, 0)), pl.BlockSpec((1, D), lambda i: (0, 0))],
        out_specs=pl.BlockSpec((tm, D), lambda i: (i, 0)),
        out_shape=jax.ShapeDtypeStruct((T, D), BF16),
        compiler_params=_params("parallel"),
    )(x, g)


def _ffn_gu(h, wg, wu):
    T, D = h.shape
    F = wg.shape[1]
    tm = _pick(T, 1024, 16)
    tn = _pick(F, 512, 128)

    def body(h_ref, wg_ref, wu_ref, g_ref, u_ref, a_ref):
        hv = h_ref[...]
        g = jnp.dot(hv, wg_ref[...], preferred_element_type=F32)
        u = jnp.dot(hv, wu_ref[...], preferred_element_type=F32)
        g_ref[...] = g.astype(BF16)
        u_ref[...] = u.astype(BF16)
        a_ref[...] = (g * jax.nn.sigmoid(g) * u).astype(BF16)

    out = jax.ShapeDtypeStruct((T, F), BF16)
    tile = pl.BlockSpec((tm, tn), lambda i, j: (i, j))
    return pl.pallas_call(
        body, name="ffn_gu", grid=(T // tm, F // tn),
        in_specs=[pl.BlockSpec((tm, D), lambda i, j: (i, 0)),
                  pl.BlockSpec((D, tn), lambda i, j: (0, j)),
                  pl.BlockSpec((D, tn), lambda i, j: (0, j))],
        out_specs=[tile, tile, tile], out_shape=[out, out, out],
        compiler_params=_params("parallel", "arbitrary"),
    )(h, wg, wu)


def _mm_nn(a, w):
    T, K = a.shape
    N = w.shape[1]
    tm = _pick(T, 1024, 16)
    tn = _pick(N, 512, 128)

    def body(a_ref, w_ref, o_ref):
        o_ref[...] = jnp.dot(a_ref[...], w_ref[...], preferred_element_type=F32).astype(BF16)

    return pl.pallas_call(
        body, name="mm_nn", grid=(T // tm, N // tn),
        in_specs=[pl.BlockSpec((tm, K), lambda i, j: (i, 0)), pl.BlockSpec((K, tn), lambda i, j: (0, j))],
        out_specs=pl.BlockSpec((tm, tn), lambda i, j: (i, j)),
        out_shape=jax.ShapeDtypeStruct((T, N), BF16),
        compiler_params=_params("parallel", "arbitrary"),
    )(a, w)


def _mm_res_norm(a, w, x, gnext, scale):
    T, K = a.shape
    D = w.shape[1]
    tm = _pick(T, 512, 16)
    tk = _pick(K, 512, 128)
    nk = K // tk

    def body(a_ref, w_ref, x_ref, g_ref, xo_ref, ho_ref, acc):
        k = pl.program_id(1)

        @pl.when(k == 0)
        def _():
            acc[...] = jnp.zeros_like(acc)

        acc[...] += jnp.dot(a_ref[...], w_ref[...], preferred_element_type=F32)

        @pl.when(k == nk - 1)
        def _():
            xn = x_ref[...] + scale * acc[...]
            xo_ref[...] = xn
            ho_ref[...] = (xn * _rms(xn) * g_ref[...]).astype(BF16)

    row = pl.BlockSpec((tm, D), lambda i, k: (i, 0))
    return pl.pallas_call(
        body, name="mm_res_norm", grid=(T // tm, nk),
        in_specs=[pl.BlockSpec((tm, tk), lambda i, k: (i, k)), pl.BlockSpec((tk, D), lambda i, k: (k, 0)),
                  row, pl.BlockSpec((1, D), lambda i, k: (0, 0))],
        out_specs=[row, row],
        out_shape=[jax.ShapeDtypeStruct((T, D), F32), jax.ShapeDtypeStruct((T, D), BF16)],
        scratch_shapes=[pltpu.VMEM((tm, D), F32)],
        compiler_params=_params("parallel", "arbitrary"),
    )(a, w, x, gnext)


def _mm_nt(dx, w, scale, gu=None):
    T, D = dx.shape
    N = w.shape[0]
    tm = _pick(T, 512, 16)
    tn = _pick(N, 512, 128)

    def body(*refs):
        if gu is None:
            dx_ref, w_ref, d_ref, dxb_ref, dxs = refs
        else:
            dx_ref, w_ref, g_ref, u_ref, dg_ref, du_ref, dxb_ref, dxs = refs
        j = pl.program_id(1)

        @pl.when(j == 0)
        def _():
            v = (dx_ref[...] * scale).astype(BF16)
            dxs[...] = v
            dxb_ref[...] = v

        d = lax.dot_general(dxs[...], w_ref[...], NT_DIMS, preferred_element_type=F32)
        if gu is None:
            d_ref[...] = d.astype(BF16)
        else:
            g = g_ref[...].astype(F32)
            u = u_ref[...].astype(F32)
            sg = jax.nn.sigmoid(g)
            dg_ref[...] = (d * u * (sg * (1.0 + g * (1.0 - sg)))).astype(BF16)
            du_ref[...] = (d * (g * sg)).astype(BF16)

    row = pl.BlockSpec((tm, D), lambda i, j: (i, 0))
    tile = pl.BlockSpec((tm, tn), lambda i, j: (i, j))
    tile_shape = jax.ShapeDtypeStruct((T, N), BF16)
    row_shape = jax.ShapeDtypeStruct((T, D), BF16)
    in_specs = [row, pl.BlockSpec((tn, D), lambda i, j: (j, 0))]
    if gu is None:
        args, out_specs, out_shape = (dx, w), [tile, row], [tile_shape, row_shape]
    else:
        in_specs += [tile, tile]
        args, out_specs, out_shape = (dx, w, gu[0], gu[1]), [tile, tile, row], [tile_shape, tile_shape, row_shape]
    return pl.pallas_call(
        body, name="mm_nt" if gu is None else "ffn_da", grid=(T // tm, N // tn),
        in_specs=in_specs, out_specs=out_specs, out_shape=out_shape,
        scratch_shapes=[pltpu.VMEM((tm, D), BF16)],
        compiler_params=_params("parallel", "arbitrary"),
    )(*args)


def _mm_tn(a, bs, tm_pref, tn_pref):
    T, M = a.shape
    N = bs[0].shape[1]
    nb = len(bs)
    tm = _pick(M, tm_pref, 128)
    tn = _pick(N, tn_pref, 128)
    tk = _pick(T, 512, 16)
    nk = T // tk

    def body(*refs):
        a_ref, b_refs, o_refs, accs = refs[0], refs[1:1 + nb], refs[1 + nb:1 + 2 * nb], refs[1 + 2 * nb:]
        k = pl.program_id(2)
        av = a_ref[...]
        for b_ref, o_ref, acc in zip(b_refs, o_refs, accs):
            p = lax.dot_general(av, b_ref[...], TN_DIMS, preferred_element_type=F32)

            @pl.when(k == 0)
            def _():
                acc[...] = p

            @pl.when(k > 0)
            def _():
                acc[...] += p

            @pl.when(k == nk - 1)
            def _():
                o_ref[...] = acc[...]

    b_spec = pl.BlockSpec((tk, tn), lambda i, j, k: (k, j))
    o_spec = pl.BlockSpec((tm, tn), lambda i, j, k: (i, j))
    return pl.pallas_call(
        body, name="mm_tn%d" % nb, grid=(M // tm, N // tn, nk),
        in_specs=[pl.BlockSpec((tk, tm), lambda i, j, k: (k, i))] + [b_spec] * nb,
        out_specs=[o_spec] * nb, out_shape=[jax.ShapeDtypeStruct((M, N), F32)] * nb,
        scratch_shapes=[pltpu.VMEM((tm, tn), F32)] * nb,
        compiler_params=_params("parallel", "parallel", "arbitrary"),
    )(a, *bs)


def _dh_norm_bwd(pairs, x, g, dres):
    T, D = x.shape
    K = pairs[0][0].shape[1]
    npair = len(pairs)
    tm = _pick(T, 512, 16)
    tk = _pick(K, 512, 128)
    nk = K // tk

    def body(*refs):
        dw_refs = refs[:2 * npair]
        x_ref, g_ref, dres_ref, dx_ref, dg_ref, acc = refs[2 * npair:]
        i = pl.program_id(0)
        k = pl.program_id(1)

        @pl.when(k == 0)
        def _():
            acc[...] = jnp.zeros_like(acc)

        for n in range(npair):
            acc[...] += lax.dot_general(dw_refs[2 * n][...], dw_refs[2 * n + 1][...], NT_DIMS,
                                        preferred_element_type=F32)

        @pl.when(k == nk - 1)
        def _():
            dxn, dgp = _norm_bwd(acc[...], x_ref[...], g_ref[...])
            dx_ref[...] = dres_ref[...] + dxn

            @pl.when(i == 0)
            def _():
                dg_ref[...] = dgp

            @pl.when(i > 0)
            def _():
                dg_ref[...] += dgp

    row = pl.BlockSpec((tm, D), lambda i, k: (i, 0))
    vec = pl.BlockSpec((1, D), lambda i, k: (0, 0))
    in_specs, args = [], []
    for d, w in pairs:
        in_specs += [pl.BlockSpec((tm, tk), lambda i, k: (i, k)), pl.BlockSpec((D, tk), lambda i, k: (0, k))]
        args += [d, w]
    return pl.pallas_call(
        body, name="dh_norm_bwd%d" % npair, grid=(T // tm, nk),
        in_specs=in_specs + [row, vec, row], out_specs=[row, vec],
        out_shape=[jax.ShapeDtypeStruct((T, D), F32), jax.ShapeDtypeStruct((1, D), F32)],
        scratch_shapes=[pltpu.VMEM((tm, D), F32)],
        compiler_params=_params("arbitrary", "arbitrary"),
    )(*args, x, g, dres)


def _loss_bwd(x, g, tgt):
    T, D = x.shape
    tm = _pick(T, 512, 16)

    def body(x_ref, g_ref, t_ref, loss_ref, dx_ref, dg_ref):
        i = pl.program_id(0)
        xv = x_ref[...]
        gv = g_ref[...]
        e = xv * _rms(xv) * gv - t_ref[...]
        part = jnp.sum(jnp.sum(e * e, axis=-1, keepdims=True), axis=0, keepdims=True) * (0.5 / D)
        dxn, dgp = _norm_bwd(e * (1.0 / D), xv, gv)
        dx_ref[...] = dxn

        @pl.when(i == 0)
        def _():
            loss_ref[...] = jnp.broadcast_to(part, loss_ref.shape)
            dg_ref[...] = dgp

        @pl.when(i > 0)
        def _():
            loss_ref[...] += jnp.broadcast_to(part, loss_ref.shape)
            dg_ref[...] += dgp

    row = pl.BlockSpec((tm, D), lambda i: (i, 0))
    vec = pl.BlockSpec((1, D), lambda i: (0, 0))
    return pl.pallas_call(
        body, name="loss_bwd", grid=(T // tm,),
        in_specs=[row, vec, row],
        out_specs=[pl.BlockSpec((1, 128), lambda i: (0, 0)), row, vec],
        out_shape=[jax.ShapeDtypeStruct((1, 128), F32), jax.ShapeDtypeStruct((T, D), F32),
                   jax.ShapeDtypeStruct((1, D), F32)],
        compiler_params=_params("arbitrary"),
    )(x, g, tgt)


def _sg_fwd(z, ws, bb, gn):
    T = z.shape[0]

    def body(zu_ref, zv_ref, ws_ref, bb_ref, gn_ref, a_ref):
        for h in range(SG_HEADS):
            sl = slice(h * HEAD_DIM, (h + 1) * HEAD_DIM)
            gv = _gelu(zv_ref[:, sl].astype(F32))
            vn = (gv * _rms(gv) * gn_ref[:, sl]).astype(BF16)
            mixed = jnp.dot(ws_ref[h], vn, preferred_element_type=F32) + bb_ref[h]
            a_ref[:, sl] = (_gelu(zu_ref[:, sl].astype(F32)) * mixed).astype(BF16)

    full = lambda shape: pl.BlockSpec(shape, lambda n: (0,) * len(shape))
    return pl.pallas_call(
        body, name="sg_fwd", grid=(T // SG_CHUNK,),
        in_specs=[pl.BlockSpec((SG_CHUNK, SG_WIDTH), lambda n: (n, 0)),
                  pl.BlockSpec((SG_CHUNK, SG_WIDTH), lambda n: (n, 1)),
                  full((SG_HEADS, SG_CHUNK, SG_CHUNK)), full((SG_HEADS, SG_CHUNK, HEAD_DIM)), full((1, SG_WIDTH))],
        out_specs=pl.BlockSpec((SG_CHUNK, SG_WIDTH), lambda n: (n, 0)),
        out_shape=jax.ShapeDtypeStruct((T, SG_WIDTH), BF16),
        compiler_params=_params("parallel"),
    )(z, z, ws, bb, gn)


def _sg_bwd(z, dmix, ws, wst, bb, gn):
    T = z.shape[0]

    def body(zu_ref, zv_ref, da_ref, ws_ref, wst_ref, bb_ref, gn_ref, dzu_ref, dzv_ref, dws_ref, dbb_ref, dgn_ref):
        n = pl.program_id(0)

        @pl.when(n == 0)
        def _():
            dws_ref[...] = jnp.zeros_like(dws_ref)
            dbb_ref[...] = jnp.zeros_like(dbb_ref)
            dgn_ref[...] = jnp.zeros_like(dgn_ref)

        for h in range(SG_HEADS):
            sl = slice(h * HEAD_DIM, (h + 1) * HEAD_DIM)
            u = zu_ref[:, sl].astype(F32)
            v = zv_ref[:, sl].astype(F32)
            da = da_ref[:, sl].astype(F32)
            gain = gn_ref[:, sl]
            gv = _gelu(v)
            r = _rms(gv)
            vn = (gv * r * gain).astype(BF16)
            mixed = jnp.dot(ws_ref[h], vn, preferred_element_type=F32) + bb_ref[h]
            dmixed = da * _gelu(u)
            dzu_ref[:, sl] = (da * mixed * _gelu_grad(u)).astype(BF16)
            dmb = dmixed.astype(BF16)
            dws_ref[h] += lax.dot_general(dmb, vn, NT_DIMS, preferred_element_type=F32)
            dbb_ref[h] += jnp.broadcast_to(jnp.sum(dmixed, axis=-1, keepdims=True), (SG_CHUNK, HEAD_DIM))
            dvn = jnp.dot(wst_ref[h], dmb, preferred_element_type=F32)
            dgv, dg = _norm_bwd(dvn, gv, gain)
            dgn_ref[:, sl] += dg
            dzv_ref[:, sl] = (dgv * _gelu_grad(v)).astype(BF16)

    full = lambda shape: pl.BlockSpec(shape, lambda n: (0,) * len(shape))
    wspec = full((SG_HEADS, SG_CHUNK, SG_CHUNK))
    tile = lambda c: pl.BlockSpec((SG_CHUNK, SG_WIDTH), lambda n: (n, c))
    return pl.pallas_call(
        body, name="sg_bwd", grid=(T // SG_CHUNK,),
        in_specs=[tile(0), tile(1), tile(0), wspec, wspec, full((SG_HEADS, SG_CHUNK, HEAD_DIM)), full((1, SG_WIDTH))],
        out_specs=[tile(0), tile(0), wspec, full((SG_HEADS, SG_CHUNK, HEAD_DIM)), full((1, SG_WIDTH))],
        out_shape=[jax.ShapeDtypeStruct((T, SG_WIDTH), BF16), jax.ShapeDtypeStruct((T, SG_WIDTH), BF16),
                   jax.ShapeDtypeStruct((SG_HEADS, SG_CHUNK, SG_CHUNK), F32),
                   jax.ShapeDtypeStruct((SG_HEADS, SG_CHUNK, HEAD_DIM), F32),
                   jax.ShapeDtypeStruct((1, SG_WIDTH), F32)],
        compiler_params=_params("arbitrary"),
    )(z, z, dmix, ws, wst, bb, gn)


def _pool_specs(T, tp, col):
    step = tp // POOL_HALO
    last = T // POOL_HALO - 1
    return [pl.BlockSpec((POOL_HALO, POOL_WIDTH), lambda i: (jnp.maximum(i * step - 1, 0), col)),
            pl.BlockSpec((tp, POOL_WIDTH), lambda i: (i, col)),
            pl.BlockSpec((POOL_HALO, POOL_WIDTH), lambda i: (jnp.minimum((i + 1) * step, last), col))]


def _pool_band(i, tp, T, win):
    ext = tp + 2 * POOL_HALO
    t = i * tp + lax.broadcasted_iota(jnp.int32, (tp, ext), 0)
    s = i * tp - POOL_HALO + lax.broadcasted_iota(jnp.int32, (tp, ext), 1)
    band = (s >= jnp.maximum(t - win // 2, 0)) & (s < jnp.minimum(t + win // 2, T))
    t1 = i * tp + lax.broadcasted_iota(jnp.int32, (tp, 1), 0)
    cnt = (jnp.minimum(t1 + win // 2, T) - jnp.maximum(t1 - win // 2, 0)).astype(F32)
    return band.astype(BF16), cnt


def _pool_fwd(z, pw, psc):
    T = z.shape[0]
    tp = _pick(T, 256, POOL_HALO)

    def body(pp_ref, pc_ref, pn_ref, w_ref, sc_ref, o_ref):
        i = pl.program_id(0)
        halo = jnp.concatenate([pp_ref[...], pc_ref[...], pn_ref[...]], axis=0)
        for g, win in enumerate(POOL_WINDOWS):
            sl = slice(g * HEAD_DIM, (g + 1) * HEAD_DIM)
            band, cnt = _pool_band(i, tp, T, win)
            ssum = jnp.dot(band, halo[:, sl], preferred_element_type=F32)
            d = ssum / cnt - pc_ref[:, sl].astype(F32)
            y = jnp.dot(d.astype(BF16), w_ref[g], preferred_element_type=F32) * sc_ref[:, sl]
            o_ref[:, sl] = y.astype(BF16)

    full = lambda shape: pl.BlockSpec(shape, lambda i: (0,) * len(shape))
    return pl.pallas_call(
        body, name="pool_fwd", grid=(T // tp,),
        in_specs=_pool_specs(T, tp, 2) + [full((4, HEAD_DIM, HEAD_DIM)), full((1, POOL_WIDTH))],
        out_specs=pl.BlockSpec((tp, POOL_WIDTH), lambda i: (i, 0)),
        out_shape=jax.ShapeDtypeStruct((T, POOL_WIDTH), BF16),
        compiler_params=_params("parallel"),
    )(z, z, z, pw, psc)


def _pool_bwd(z, dmix, pw, psc):
    T = z.shape[0]
    tp = _pick(T, 256, POOL_HALO)
    ext = tp + 2 * POOL_HALO

    def body(pp_ref, pc_ref, pn_ref, dp_ref, dc_ref, dn_ref, w_ref, sc_ref, dz_ref, dw_ref, dsc_ref):
        i = pl.program_id(0)

        @pl.when(i == 0)
        def _():
            dw_ref[...] = jnp.zeros_like(dw_ref)
            dsc_ref[...] = jnp.zeros_like(dsc_ref)

        halo = jnp.concatenate([pp_ref[...], pc_ref[...], pn_ref[...]], axis=0)
        dy_halo = jnp.concatenate([dp_ref[...], dc_ref[...], dn_ref[...]], axis=0)
        th = i * tp - POOL_HALO + lax.broadcasted_iota(jnp.int32, (ext, 1), 0)
        inside = (th >= 0) & (th < T)
        s2 = i * tp + lax.broadcasted_iota(jnp.int32, (tp, ext), 0)
        t2 = i * tp - POOL_HALO + lax.broadcasted_iota(jnp.int32, (tp, ext), 1)
        for g, win in enumerate(POOL_WINDOWS):
            sl = slice(g * HEAD_DIM, (g + 1) * HEAD_DIM)
            sc = sc_ref[:, sl]
            band, cnt = _pool_band(i, tp, T, win)
            ssum = jnp.dot(band, halo[:, sl], preferred_element_type=F32)
            db = (ssum / cnt - pc_ref[:, sl].astype(F32)).astype(BF16)
            yraw = jnp.dot(db, w_ref[g], preferred_element_type=F32)
            dyc = dc_ref[:, sl].astype(F32)
            dsc_ref[:, sl] += jnp.sum(dyc * yraw, axis=0, keepdims=True)
            dw_ref[g] += lax.dot_general(db, (dyc * sc).astype(BF16), TN_DIMS, preferred_element_type=F32)
            dd = lax.dot_general((dy_halo[:, sl].astype(F32) * sc).astype(BF16), w_ref[g], NT_DIMS,
                                 preferred_element_type=F32)
            cnt_h = (jnp.minimum(th + win // 2, T) - jnp.maximum(th - win // 2, 0)).astype(F32)
            ddc = jnp.where(inside, dd / jnp.maximum(cnt_h, 1.0), 0.0)
            hi = ddc.astype(BF16)
            lo = (ddc - hi.astype(F32)).astype(BF16)
            band_t = ((s2 >= jnp.maximum(t2 - win // 2, 0)) & (s2 < jnp.minimum(t2 + win // 2, T))).astype(BF16)
            dpool = (jnp.dot(band_t, hi, preferred_element_type=F32) + jnp.dot(band_t, lo, preferred_element_type=F32)
                     - dd[POOL_HALO:POOL_HALO + tp])
            dz_ref[:, sl] = dpool.astype(BF16)

    full = lambda shape: pl.BlockSpec(shape, lambda i: (0,) * len(shape))
    return pl.pallas_call(
        body, name="pool_bwd", grid=(T // tp,),
        in_specs=_pool_specs(T, tp, 2) + _pool_specs(T, tp, 1) + [full((4, HEAD_DIM, HEAD_DIM)), full((1, POOL_WIDTH))],
        out_specs=[pl.BlockSpec((tp, POOL_WIDTH), lambda i: (i, 0)), full((4, HEAD_DIM, HEAD_DIM)), full((1, POOL_WIDTH))],
        out_shape=[jax.ShapeDtypeStruct((T, POOL_WIDTH), BF16), jax.ShapeDtypeStruct((4, HEAD_DIM, HEAD_DIM), F32),
                   jax.ShapeDtypeStruct((1, POOL_WIDTH), F32)],
        compiler_params=_params("arbitrary"),
    )(z, z, z, dmix, dmix, dmix, pw, psc)


ATT_ROWS = 8
WIN_KEYS = NA_KH * GRID_W


def _col_mask():
    q = lax.broadcasted_iota(jnp.int32, (GRID_W, WIN_KEYS), 0)
    k = lax.broadcasted_iota(jnp.int32, (GRID_W, WIN_KEYS), 1) & (GRID_W - 1)
    start = jnp.clip(q - NA_KW // 2, 0, GRID_W - NA_KW)
    return (k >= start) & (k < start + NA_KW)


def _attn_probs(q, kw, bias, mask):
    s = lax.dot_general(q, kw, NT_DIMS, preferred_element_type=F32) * (HEAD_DIM ** -0.5) + bias
    s = jnp.where(mask, s, NEG)
    p = jnp.exp(s - jnp.max(s, axis=-1, keepdims=True))
    return p / jnp.sum(p, axis=-1, keepdims=True)


def _attn_window(step, a, rows):
    r = step * ATT_ROWS + a
    sr = jnp.clip(r - NA_KH // 2, 0, rows - NA_KH)
    return pl.multiple_of(sr * GRID_W, GRID_W), sr - r + NA_KH - 1


def _attn_fwd(z, ecat):
    T = z.shape[0]
    rows = T // GRID_W
    blk = ATT_ROWS * GRID_W

    def body(q_ref, k_ref, v_ref, e_ref, o_ref):
        step = pl.program_id(1)
        mask = _col_mask()

        def row(a, carry):
            start, dr0 = _attn_window(step, a, rows)
            qs = pl.ds(pl.multiple_of(a * GRID_W, GRID_W), GRID_W)
            pr = _attn_probs(q_ref[qs, :], k_ref[pl.ds(start, WIN_KEYS), :], e_ref[0, dr0], mask)
            o = jnp.dot(pr.astype(BF16), v_ref[pl.ds(start, WIN_KEYS), :], preferred_element_type=F32)
            o_ref[qs, :] = o.astype(BF16)
            return carry

        lax.fori_loop(0, ATT_ROWS, row, 0)

    col = lambda off: pl.BlockSpec((T, HEAD_DIM), lambda h, s: (0, off // HEAD_DIM + h))
    return pl.pallas_call(
        body, name="attn_fwd", grid=(NA_HEADS, rows // ATT_ROWS),
        in_specs=[pl.BlockSpec((blk, HEAD_DIM), lambda h, s: (s, Q_OFF // HEAD_DIM + h)), col(K_OFF), col(V_OFF),
                  pl.BlockSpec((1, NA_KH, GRID_W, WIN_KEYS), lambda h, s: (h, 0, 0, 0))],
        out_specs=pl.BlockSpec((blk, HEAD_DIM), lambda h, s: (s, h)),
        out_shape=jax.ShapeDtypeStruct((T, NA_WIDTH), BF16),
        compiler_params=_params("parallel", "arbitrary"),
    )(z, z, z, ecat)


def _attn_bwd(z, dmix, ecat):
    T = z.shape[0]
    rows = T // GRID_W
    blk = ATT_ROWS * GRID_W
    nstep = rows // ATT_ROWS

    def body(q_ref, k_ref, v_ref, do_ref, e_ref, dq_ref, dk_ref, dv_ref, de_ref, dk_acc, dv_acc):
        step = pl.program_id(1)
        mask = _col_mask()

        @pl.when(step == 0)
        def _():
            dk_acc[...] = jnp.zeros_like(dk_acc)
            dv_acc[...] = jnp.zeros_like(dv_acc)
            de_ref[...] = jnp.zeros_like(de_ref)

        def row(a, carry):
            start, dr0 = _attn_window(step, a, rows)
            qs = pl.ds(pl.multiple_of(a * GRID_W, GRID_W), GRID_W)
            win = pl.ds(start, WIN_KEYS)
            q = q_ref[qs, :]
            kw = k_ref[win, :]
            do = do_ref[qs, :]
            pr = _attn_probs(q, kw, e_ref[0, dr0], mask)
            dp = lax.dot_general(do, v_ref[win, :], NT_DIMS, preferred_element_type=F32)
            dv_acc[win, :] += lax.dot_general(pr.astype(BF16), do, TN_DIMS, preferred_element_type=F32)
            ds = pr * (dp - jnp.sum(dp * pr, axis=-1, keepdims=True))
            de_ref[0, dr0] += ds
            dsb = (ds * (HEAD_DIM ** -0.5)).astype(BF16)
            dq_ref[qs, :] = jnp.dot(dsb, kw, preferred_element_type=F32).astype(BF16)
            dk_acc[win, :] += lax.dot_general(dsb, q, TN_DIMS, preferred_element_type=F32)
            return carry

        lax.fori_loop(0, ATT_ROWS, row, 0)

        @pl.when(step == nstep - 1)
        def _():
            dk_ref[...] = dk_acc[...].astype(BF16)
            dv_ref[...] = dv_acc[...].astype(BF16)

    col = lambda off: pl.BlockSpec((T, HEAD_DIM), lambda h, s: (0, off // HEAD_DIM + h))
    e_spec = pl.BlockSpec((1, NA_KH, GRID_W, WIN_KEYS), lambda h, s: (h, 0, 0, 0))
    out = jax.ShapeDtypeStruct((T, NA_WIDTH), BF16)
    return pl.pallas_call(
        body, name="attn_bwd", grid=(NA_HEADS, nstep),
        in_specs=[pl.BlockSpec((blk, HEAD_DIM), lambda h, s: (s, Q_OFF // HEAD_DIM + h)), col(K_OFF), col(V_OFF),
                  pl.BlockSpec((blk, HEAD_DIM), lambda h, s: (s, (SG_WIDTH + POOL_WIDTH) // HEAD_DIM + h)), e_spec],
        out_specs=[pl.BlockSpec((blk, HEAD_DIM), lambda h, s: (s, h)), col(0), col(0), e_spec],
        out_shape=[out, out, out, jax.ShapeDtypeStruct((NA_HEADS, NA_KH, GRID_W, WIN_KEYS), F32)],
        scratch_shapes=[pltpu.VMEM((T, HEAD_DIM), F32), pltpu.VMEM((T, HEAD_DIM), F32)],
        compiler_params=_params("parallel", "arbitrary"),
    )(z, z, z, dmix, ecat)


def _rpb_onehot():
    col = jnp.arange(GRID_W)
    dc = jnp.clip(col[None, :] - col[:, None] + NA_KW - 1, 0, 2 * NA_KW - 2)
    return (dc[None] == jnp.arange(2 * NA_KW - 1)[:, None, None]).astype(F32)


def _rpb_expand(rpb):
    e = jnp.einsum("hrc,cqk->hrqk", rpb, _rpb_onehot(), precision=lax.Precision.HIGHEST)
    return jnp.stack([jnp.concatenate([e[:, d + j] for j in range(NA_KH)], axis=-1) for d in range(NA_KH)], axis=1)


def _rpb_collect(decat):
    parts = decat.reshape(NA_HEADS, NA_KH, GRID_W, NA_KH, GRID_W)
    de = [sum(parts[:, d, :, r - d] for d in range(NA_KH) if 0 <= r - d < NA_KH) for r in range(2 * NA_KH - 1)]
    return jnp.einsum("hrqk,cqk->hrc", jnp.stack(de, axis=1), _rpb_onehot(), precision=lax.Precision.HIGHEST)


def _adamw(w, g, m, v):
    shape = w.shape
    C = shape[-1]
    R = w.size // C
    tr = _pick(R, max(8, (1 << 18) // C), 8)
    args = [a.reshape(R, C) for a in (w, g, m, v)]

    def body(w_ref, g_ref, m_ref, v_ref, d_ref, mo_ref, vo_ref):
        gv = g_ref[...]
        mn = ADAM_B1 * m_ref[...] + (1.0 - ADAM_B1) * gv
        vn = ADAM_B2 * v_ref[...] + (1.0 - ADAM_B2) * (gv * gv)
        m_hat = mn / (1.0 - ADAM_B1 ** ADAM_STEP)
        v_hat = vn / (1.0 - ADAM_B2 ** ADAM_STEP)
        d_ref[...] = -ADAM_LR * (m_hat / (jnp.sqrt(v_hat) + ADAM_EPS) + ADAM_WD * w_ref[...])
        mo_ref[...] = mn
        vo_ref[...] = vn

    spec = pl.BlockSpec((tr, C), lambda i: (i, 0))
    out = jax.ShapeDtypeStruct((R, C), F32)
    res = pl.pallas_call(
        body, name="adamw", grid=(R // tr,), in_specs=[spec] * 4, out_specs=[spec] * 3, out_shape=[out] * 3,
        compiler_params=_params("parallel"),
    )(*args)
    return [r.reshape(shape) for r in res]


def _add_pairs(buf, recv, c):
    _, R, C = buf.shape
    tr = _pick(R, 512, 16)

    def body(c_ref, a_ref, b_ref, o_ref):
        o_ref[...] = (a_ref[...].astype(F32) + b_ref[...].astype(F32)).astype(BF16)

    return pl.pallas_call(
        body, name="add_pairs",
        grid_spec=pltpu.PrefetchScalarGridSpec(
            num_scalar_prefetch=1, grid=(4, R // tr),
            in_specs=[pl.BlockSpec((1, tr, C), lambda k, i, c_ref: (2 * k + c_ref[0], i, 0)),
                      pl.BlockSpec((1, tr, C), lambda k, i, c_ref: (k, i, 0))],
            out_specs=pl.BlockSpec((1, tr, C), lambda k, i, c_ref: (k, i, 0))),
        out_shape=jax.ShapeDtypeStruct((4, R, C), BF16),
        compiler_params=_params("parallel", "parallel"),
    )(c, buf, recv)


def _sum_chips(part, recv, chip):
    _, R, C = part.shape
    tr = _pick(R, 512, 16)

    def body(chip_ref, p_ref, r0_ref, r1_ref, r2_ref, o_ref):
        o_ref[...] = ((p_ref[0].astype(F32) + r0_ref[0].astype(F32)) + r1_ref[0].astype(F32)) + r2_ref[0].astype(F32)

    slot = lambda k: pl.BlockSpec((1, tr, C), lambda i, chip_ref: (k, i, 0))
    return pl.pallas_call(
        body, name="sum_chips",
        grid_spec=pltpu.PrefetchScalarGridSpec(
            num_scalar_prefetch=1, grid=(R // tr,),
            in_specs=[pl.BlockSpec((1, tr, C), lambda i, chip_ref: (chip_ref[0], i, 0)), slot(0), slot(1), slot(2)],
            out_specs=pl.BlockSpec((tr, C), lambda i, chip_ref: (i, 0))),
        out_shape=jax.ShapeDtypeStruct((R, C), F32),
        compiler_params=_params("parallel"),
    )(chip, part, recv, recv, recv)


def _sum_devices(g):
    _, R, C = g.shape
    tr = _pick(R, 512, 8)

    def body(g_ref, o_ref):
        acc = g_ref[0]
        for k in range(1, N_DEV):
            acc = acc + g_ref[k]
        o_ref[...] = acc

    return pl.pallas_call(
        body, name="sum_devices", grid=(R // tr,),
        in_specs=[pl.BlockSpec((N_DEV, tr, C), lambda i: (0, i, 0))],
        out_specs=pl.BlockSpec((tr, C), lambda i: (i, 0)),
        out_shape=jax.ShapeDtypeStruct((R, C), F32),
        compiler_params=_params("parallel"),
    )(g)


ANY = pl.BlockSpec(memory_space=pl.ANY)


def _position():
    return lax.axis_index("x"), lax.axis_index("y"), lax.axis_index("c")


def _all_gather(xs, name):
    R, C = xs.shape

    def body(x_ref, out_ref, send_sems, recv_sems, local_sem):
        x, y, c = _position()
        me, sibling = (x, y, c), (x, y, 1 - c)
        chips = [(1 - x, y), (x, 1 - y), (1 - x, 1 - y)]

        def slot(px, py, pc):
            return out_ref.at[4 * px + 2 * py + pc]

        def copy(k, block, to, src=None):
            return pltpu.make_async_remote_copy(
                src_ref=slot(*block) if src is None else src, dst_ref=slot(*block),
                send_sem=send_sems.at[k], recv_sem=recv_sems.at[k], device_id=to, device_id_type=MESH)

        mine = pltpu.make_async_copy(x_ref, slot(*me), local_sem)
        mine.start()
        first = [copy(0, me, sibling, src=x_ref)]
        first += [copy(1 + j, me, (*chip, c), src=x_ref) for j, chip in enumerate(chips)]
        for cp in first:
            cp.start()
        passed = [copy(4 + j, (*chip, c), sibling) for j, chip in enumerate(chips)]
        for j, chip in enumerate(chips):
            copy(1 + j, (*chip, c), me).wait_recv()
            passed[j].start()
        copy(0, sibling, me).wait_recv()
        for j, chip in enumerate(chips):
            copy(4 + j, (*chip, 1 - c), me).wait_recv()
        for cp in first + passed:
            cp.wait_send()
        mine.wait()

    return pl.pallas_call(
        body, name=name, in_specs=[ANY], out_specs=ANY,
        out_shape=jax.ShapeDtypeStruct((N_DEV, R, C), xs.dtype),
        scratch_shapes=[pltpu.SemaphoreType.DMA((7,)), pltpu.SemaphoreType.DMA((7,)), pltpu.SemaphoreType.DMA],
    )(xs)


def _sibling_exchange(buf):
    _, R, C = buf.shape

    def body(buf_ref, out_ref, send_sems, recv_sems):
        x, y, c = _position()
        copies = [pltpu.make_async_remote_copy(
            src_ref=buf_ref.at[2 * k + 1 - c], dst_ref=out_ref.at[k], send_sem=send_sems.at[k],
            recv_sem=recv_sems.at[k], device_id=(x, y, 1 - c), device_id_type=MESH) for k in range(4)]
        for cp in copies:
            cp.start()
        for cp in copies:
            cp.wait()

    return pl.pallas_call(
        body, name="sibling_exchange", in_specs=[ANY], out_specs=ANY,
        out_shape=jax.ShapeDtypeStruct((4, R, C), buf.dtype),
        scratch_shapes=[pltpu.SemaphoreType.DMA((4,)), pltpu.SemaphoreType.DMA((4,))],
    )(buf)


def _chip_exchange(part):
    _, R, C = part.shape

    def body(part_ref, out_ref, send_sems, recv_sems):
        x, y, c = _position()
        chips = [(1 - x, y), (x, 1 - y), (1 - x, 1 - y)]
        copies = [pltpu.make_async_remote_copy(
            src_ref=part_ref.at[2 * cx + cy], dst_ref=out_ref.at[j], send_sem=send_sems.at[j],
            recv_sem=recv_sems.at[j], device_id=(cx, cy, c), device_id_type=MESH) for j, (cx, cy) in enumerate(chips)]
        for cp in copies:
            cp.start()
        for cp in copies:
            cp.wait()

    return pl.pallas_call(
        body, name="chip_exchange", in_specs=[ANY], out_specs=ANY,
        out_shape=jax.ShapeDtypeStruct((3, R, C), part.dtype),
        scratch_shapes=[pltpu.SemaphoreType.DMA((3,)), pltpu.SemaphoreType.DMA((3,))],
    )(part)


def _reduce_scatter(buf):
    x, y, c = _position()
    recv = _sibling_exchange(buf)
    part = _add_pairs(buf, recv, jnp.reshape(c, (1,)).astype(jnp.int32))
    got = _chip_exchange(part)
    return _sum_chips(part, got, jnp.reshape(2 * x + y, (1,)).astype(jnp.int32))


BIG = ("ffn1_w_gate", "ffn1_w_up", "ffn1_w_down", "w_in", "w_out", "ffn2_w_gate", "ffn2_w_up", "ffn2_w_down")
COL_SHARDED = ("ffn1_w_gate", "ffn1_w_up", "w_in", "ffn2_w_gate", "ffn2_w_up")
SMALL = ("ffn1_norm", "mix_norm", "sg_norm", "sg_w", "sg_b", "pool_w", "pool_scale", "na_rpb", "ffn2_norm")


def _pack_shards(shards, D):
    return jnp.concatenate([s.reshape(-1, D) for s in shards], axis=0)


def _unpack_full(gathered, shapes, D):
    out, r0 = {}, 0
    for name, (a, b) in shapes.items():
        rows = a * b // D
        blk = gathered[:, r0:r0 + rows, :]
        if name in COL_SHARDED:
            out[name] = blk.reshape(N_DEV, a, b).transpose(1, 0, 2).reshape(a, N_DEV * b)
        else:
            out[name] = blk.reshape(N_DEV * a, b)
        r0 += rows
    return out


def _pack_grads(grads, shapes, D):
    parts = []
    for name, (a, b) in shapes.items():
        g = grads[name].astype(BF16)
        if name in COL_SHARDED:
            g = g.reshape(a, N_DEV, b).transpose(1, 0, 2)
        parts.append(g.reshape(N_DEV, a * b // D, D))
    return jnp.concatenate(parts, axis=1)


def kernel(x, ffn1_norm, ffn1_w_gate, ffn1_w_up, ffn1_w_down, mix_norm, w_in, sg_norm, sg_w, sg_b, pool_w, pool_scale, na_rpb, w_out, ffn2_norm, ffn2_w_gate, ffn2_w_up, ffn2_w_down, final_norm, loss_target, m_ffn1_norm, m_ffn1_w_gate, m_ffn1_w_up, m_ffn1_w_down, m_mix_norm, m_w_in, m_sg_norm, m_sg_w, m_sg_b, m_pool_w, m_pool_scale, m_na_rpb, m_w_out, m_ffn2_norm, m_ffn2_w_gate, m_ffn2_w_up, m_ffn2_w_down, m_final_norm, v_ffn1_norm, v_ffn1_w_gate, v_ffn1_w_up, v_ffn1_w_down, v_mix_norm, v_w_in, v_sg_norm, v_sg_w, v_sg_b, v_pool_w, v_pool_scale, v_na_rpb, v_w_out, v_ffn2_norm, v_ffn2_w_gate, v_ffn2_w_up, v_ffn2_w_down, v_final_norm):
    given = dict(locals())
    T, D = x.shape[1], x.shape[2]
    L = ffn1_norm.shape[0]
    assert x.shape[0] == 1 and D == SG_WIDTH + POOL_WIDTH + NA_WIDTH and w_in.shape[2] * N_DEV == Z_COLS
    assert T % (ATT_ROWS * GRID_W) == 0 and T // GRID_W >= NA_KH
    x0 = x.reshape(T, D)
    tgt = loss_target.reshape(T, D)
    shapes = {n: tuple(given[n].shape[1:]) for n in BIG}

    saved = []
    xc = x0
    h = _rmsnorm(xc, ffn1_norm[0:1])
    for l in range(L):
        gathered = _all_gather(_pack_shards([given[n][l].astype(BF16) for n in BIG], D), "gather_weights")
        W = _unpack_full(gathered, shapes, D)
        s = dict(W=W, x0=xc, h1=h)
        s["g1"], s["u1"], s["a1"] = _ffn_gu(h, W["ffn1_w_gate"], W["ffn1_w_up"])
        xc, h = _mm_res_norm(s["a1"], W["ffn1_w_down"], xc, mix_norm[l:l + 1], 0.5)
        s["x1"], s["h2"] = xc, h
        z = _mm_nn(h, W["w_in"])
        s["ws"] = sg_w[l].astype(BF16)
        s["wst"] = jnp.swapaxes(sg_w[l], 1, 2).astype(BF16)
        s["bb"] = jnp.broadcast_to(sg_b[l][:, :, None], (SG_HEADS, SG_CHUNK, HEAD_DIM))
        s["gn"] = sg_norm[l:l + 1]
        s["pw"] = pool_w[l].astype(BF16)
        s["psc"] = pool_scale[l:l + 1]
        s["ecat"] = _rpb_expand(na_rpb[l])
        mix = jnp.concatenate([_sg_fwd(z, s["ws"], s["bb"], s["gn"]), _pool_fwd(z, s["pw"], s["psc"]),
                               _attn_fwd(z, s["ecat"])], axis=1)
        s["z"], s["mix"] = z, mix
        xc, h = _mm_res_norm(mix, W["w_out"], xc, ffn2_norm[l:l + 1], 1.0)
        s["x2"], s["h3"] = xc, h
        s["g2"], s["u2"], s["a2"] = _ffn_gu(h, W["ffn2_w_gate"], W["ffn2_w_up"])
        gnext = ffn1_norm[l + 1:l + 2] if l + 1 < L else final_norm.reshape(1, D)
        xc, h = _mm_res_norm(s["a2"], W["ffn2_w_down"], xc, gnext, 0.5)
        saved.append(s)

    loss_row, dx, dg_final = _loss_bwd(xc, final_norm.reshape(1, D), tgt)
    loss = lax.psum(loss_row[0, 0], MESH_AXES)

    big_grads = [None] * L
    small = {n: [None] * L for n in SMALL}
    for l in reversed(range(L)):
        s = saved[l]
        W = s["W"]
        gb = {}
        dgt, dut, dyh = _mm_nt(dx, W["ffn2_w_down"], 0.5, gu=(s["g2"], s["u2"]))
        gb["ffn2_w_down"] = _mm_tn(s["a2"], [dyh], 1408, 1024)[0]
        dx, small["ffn2_norm"][l] = _dh_norm_bwd([(dgt, W["ffn2_w_gate"]), (dut, W["ffn2_w_up"])], s["x2"],
                                                 ffn2_norm[l:l + 1], dx)
        gb["ffn2_w_gate"], gb["ffn2_w_up"] = _mm_tn(s["h3"], [dgt, dut], 1024, 1408)

        dmix, dxb = _mm_nt(dx, W["w_out"], 1.0)
        gb["w_out"] = _mm_tn(s["mix"], [dxb], 1024, 1024)[0]
        dzu, dzv, dws, dbb, dgn = _sg_bwd(s["z"], dmix, s["ws"], s["wst"], s["bb"], s["gn"])
        dzp, dpw, dpsc = _pool_bwd(s["z"], dmix, s["pw"], s["psc"])
        dq, dk, dv, decat = _attn_bwd(s["z"], dmix, s["ecat"])
        small["sg_w"][l], small["sg_b"][l], small["sg_norm"][l] = dws, dbb[:, :, 0], dgn[0]
        small["pool_w"][l], small["pool_scale"][l] = dpw, dpsc[0]
        small["na_rpb"][l] = _rpb_collect(decat)
        dz = jnp.concatenate([dzu, dzv, dzp, dq, dk, dv], axis=1)
        dx, small["mix_norm"][l] = _dh_norm_bwd([(dz, W["w_in"])], s["x1"], mix_norm[l:l + 1], dx)
        gb["w_in"] = _mm_tn(s["h2"], [dz], 1024, 1152)[0]

        dgt, dut, dyh = _mm_nt(dx, W["ffn1_w_down"], 0.5, gu=(s["g1"], s["u1"]))
        gb["ffn1_w_down"] = _mm_tn(s["a1"], [dyh], 1408, 1024)[0]
        dx, small["ffn1_norm"][l] = _dh_norm_bwd([(dgt, W["ffn1_w_gate"]), (dut, W["ffn1_w_up"])], s["x0"],
                                                 ffn1_norm[l:l + 1], dx)
        gb["ffn1_w_gate"], gb["ffn1_w_up"] = _mm_tn(s["h1"], [dgt, dut], 1024, 1408)
        big_grads[l] = _reduce_scatter(_pack_grads(gb, shapes, D))

    small_shapes = {n: given[n].shape for n in SMALL}
    small_shapes["final_norm"] = final_norm.shape
    flat = [jnp.stack([jnp.reshape(g, (-1,)) for g in small[n]]).reshape(-1) for n in SMALL] + [dg_final.reshape(-1)]
    sizes = [f.shape[0] for f in flat]
    total = sum(sizes)
    padded = -(-total // 1024) * 1024
    local = jnp.concatenate(flat + [jnp.zeros((padded - total,), F32)]).reshape(-1, 128)
    summed = _sum_devices(_all_gather(local, "gather_small_grads")).reshape(-1)
    grads, off = {}, 0
    for n, size in zip(list(SMALL) + ["final_norm"], sizes):
        grads[n] = summed[off:off + size].reshape(small_shapes[n])
        off += size

    r0 = 0
    for n in BIG:
        a, b = shapes[n]
        rows = a * b // D
        grads[n] = jnp.stack([big_grads[l][r0:r0 + rows].reshape(a, b) for l in range(L)])
        r0 += rows

    names = ['ffn1_norm', 'ffn1_w_gate', 'ffn1_w_up', 'ffn1_w_down', 'mix_norm', 'w_in', 'sg_norm', 'sg_w', 'sg_b',
             'pool_w', 'pool_scale', 'na_rpb', 'w_out', 'ffn2_norm', 'ffn2_w_gate', 'ffn2_w_up', 'ffn2_w_down',
             'final_norm']
    delta, new_m, new_v = {}, {}, {}
    for n in names:
        delta[n], new_m[n], new_v[n] = _adamw(given[n], grads[n], given["m_" + n], given["v_" + n])
    return (loss, dx.reshape(1, T, D), *[grads[n] for n in names], *[delta[n] for n in names],
            *[new_m[n] for n in names], *[new_v[n] for n in names])
```

```python
import functools
import math

import jax
import jax.numpy as jnp
from jax import lax
from jax.experimental import pallas as pl
from jax.experimental.pallas import tpu as pltpu

F32 = jnp.float32
BF16 = jnp.bfloat16
EPS = 1e-6
NEG = -1e30

HEAD_DIM = 128
SG_WIDTH = 512
SG_HEADS = 4
SG_CHUNK = 128
POOL_WINDOWS = (2, 4, 8, 16)
POOL_WIDTH = 512
POOL_HALO = 128
NA_WIDTH = 1024
NA_HEADS = 8
NA_KH = 8
NA_KW = 16
GRID_W = 64
Z_COLS = 2 * SG_WIDTH + POOL_WIDTH + 3 * NA_WIDTH
Q_OFF = 2 * SG_WIDTH + POOL_WIDTH
K_OFF = Q_OFF + NA_WIDTH
V_OFF = K_OFF + NA_WIDTH

ADAM_LR = 0.001
ADAM_B1 = 0.9
ADAM_B2 = 0.999
ADAM_EPS = 1e-08
ADAM_WD = 0.01
ADAM_STEP = 10

N_DEV = 8
MESH_AXES = ("x", "y", "c")
MESH = pl.DeviceIdType.MESH
ANY = pl.BlockSpec(memory_space=pl.ANY)

NT_DIMS = (((1,), (1,)), ((), ()))
TN_DIMS = (((0,), (0,)), ((), ()))


def _pick(n, pref, mult):
    best = None
    t = mult
    while t <= min(n, pref):
        if n % t == 0:
            best = t
        t += mult
    return n if best is None else best


def _gelu(x):
    return 0.5 * x * (1.0 + lax.erf(x * (1.0 / math.sqrt(2.0))))


def _gelu_grad(x):
    cdf = 0.5 * (1.0 + lax.erf(x * (1.0 / math.sqrt(2.0))))
    pdf = jnp.exp(-0.5 * x * x) * (1.0 / math.sqrt(2.0 * math.pi))
    return cdf + x * pdf


def _rms(x):
    return lax.rsqrt(jnp.mean(x * x, axis=-1, keepdims=True) + EPS)


def _norm_bwd(dh, x, g):
    r = _rms(x)
    w = dh * g
    dx = r * w - x * (r * r * r) * jnp.mean(w * x, axis=-1, keepdims=True)
    dg = jnp.sum(dh * (x * r), axis=0, keepdims=True)
    return dx, dg


def _position():
    return lax.axis_index("x"), lax.axis_index("y"), lax.axis_index("c")


def _index(p):
    return 4 * p[0] + 2 * p[1] + p[2]


class _Carry:
    def __init__(self, gathers=(), scatters=()):
        self.gathers, self.scatters = list(gathers), list(scatters)
        self.n_sem = 7 * (len(self.gathers) + len(self.scatters))

    def arrays(self):
        return self.gathers + self.scatters

    def out_shapes(self):
        return ([jax.ShapeDtypeStruct((N_DEV,) + g.shape, g.dtype) for g in self.gathers]
                + [jax.ShapeDtypeStruct(s.shape, s.dtype) for s in self.scatters])

    def scratch(self):
        return [pltpu.SemaphoreType.DMA((self.n_sem,)), pltpu.SemaphoreType.DMA((self.n_sem,)),
                pltpu.SemaphoreType.DMA((len(self.arrays()),))]

    def _gather_copies(self, n, x_ref, out_ref, send, recv, local):
        x, y, c = _position()
        me, sibling = (x, y, c), (x, y, 1 - c)
        chips = [(1 - x, y), (x, 1 - y), (1 - x, 1 - y)]

        def copy(k, block, to, src=None):
            dst = out_ref.at[_index(block)]
            return pltpu.make_async_remote_copy(
                src_ref=dst if src is None else src, dst_ref=dst, send_sem=send.at[7 * n + k],
                recv_sem=recv.at[7 * n + k], device_id=to, device_id_type=MESH)

        return dict(
            mine=pltpu.make_async_copy(x_ref, out_ref.at[_index(me)], local.at[n]),
            first=[copy(0, me, sibling, x_ref)] + [copy(1 + j, me, (*ch, c), x_ref) for j, ch in enumerate(chips)],
            landed=[copy(1 + j, (*ch, c), me) for j, ch in enumerate(chips)],
            passed=[copy(4 + j, (*ch, c), sibling) for j, ch in enumerate(chips)],
            from_sibling=[copy(0, sibling, me)] + [copy(4 + j, (*ch, 1 - c), me) for j, ch in enumerate(chips)])

    def _scatter_copies(self, n, src_ref, out_ref, send, recv, local):
        x, y, c = _position()
        me = (x, y, c)
        sends, recvs = [], []
        for k in range(1, N_DEV):
            flip = lambda v, bit: 1 - v if bit else v
            peer = (flip(x, k & 4), flip(y, k & 2), flip(c, k & 1))
            sems = dict(send_sem=send.at[7 * n + k - 1], recv_sem=recv.at[7 * n + k - 1], device_id=peer,
                        device_id_type=MESH)
            sends.append(pltpu.make_async_remote_copy(src_ref=src_ref.at[_index(peer)], dst_ref=out_ref.at[_index(me)],
                                                      **sems))
            recvs.append(pltpu.make_async_remote_copy(src_ref=src_ref.at[_index(me)], dst_ref=out_ref.at[_index(peer)],
                                                      **sems))
        mine = pltpu.make_async_copy(src_ref.at[_index(me)], out_ref.at[_index(me)], local.at[n])
        return dict(mine=mine, sends=sends, recvs=recvs)

    def _pieces(self, ins, outs, sems):
        send, recv, local = sems
        ng = len(self.gathers)
        gs = [self._gather_copies(n, ins[n], outs[n], send, recv, local) for n in range(ng)]
        ss = [self._scatter_copies(ng + n, ins[ng + n], outs[ng + n], send, recv, local)
              for n in range(len(self.scatters))]
        return gs, ss

    def start(self, ins, outs, sems):
        gs, ss = self._pieces(ins, outs, sems)
        for g in gs:
            g["mine"].start()
            for cp in g["first"]:
                cp.start()
        for s in ss:
            s["mine"].start()
            for cp in s["sends"]:
                cp.start()

    def forward(self, ins, outs, sems):
        gs, _ = self._pieces(ins, outs, sems)
        for g in gs:
            for landed, passed in zip(g["landed"], g["passed"]):
                landed.wait_recv()
                passed.start()

    def finish(self, ins, outs, sems):
        gs, ss = self._pieces(ins, outs, sems)
        for g in gs:
            for cp in g["from_sibling"]:
                cp.wait_recv()
            for cp in g["first"] + g["passed"]:
                cp.wait_send()
            g["mine"].wait()
        for s in ss:
            for cp in s["recvs"]:
                cp.wait_recv()
            for cp in s["sends"]:
                cp.wait_send()
            s["mine"].wait()


def _call(body, *, name, grid, in_specs, out_specs, out_shape, args, scratch_shapes=(), carry=None):
    if carry is None or not carry.arrays():
        outs = pl.pallas_call(
            body, name=name, grid=grid, in_specs=in_specs, out_specs=out_specs, out_shape=out_shape,
            scratch_shapes=list(scratch_shapes),
            compiler_params=pltpu.CompilerParams(dimension_semantics=("arbitrary",) * len(grid)))(*args)
        return list(outs), []
    n_in, n_out, n_scr, n_car = len(in_specs), len(out_specs), len(scratch_shapes), len(carry.arrays())
    steps = math.prod(grid)
    middle = (steps * 6) // 10

    def wrapped(*refs):
        ins, refs = refs[:n_in], refs[n_in:]
        cins, refs = refs[:n_car], refs[n_car:]
        outs, refs = refs[:n_out], refs[n_out:]
        couts, refs = refs[:n_car], refs[n_car:]
        scr, sems = refs[:n_scr], refs[n_scr:]
        step = 0
        for d, size in enumerate(grid):
            step = step * size + pl.program_id(d)

        @pl.when(step == 0)
        def _():
            carry.start(cins, couts, sems)

        body(*ins, *outs, *scr)

        if carry.gathers:
            @pl.when(step == middle)
            def _():
                carry.forward(cins, couts, sems)

        @pl.when(step == steps - 1)
        def _():
            carry.finish(cins, couts, sems)

    outs = pl.pallas_call(
        wrapped, name=name + "_carry", grid=grid, in_specs=list(in_specs) + [ANY] * n_car,
        out_specs=list(out_specs) + [ANY] * n_car, out_shape=list(out_shape) + carry.out_shapes(),
        scratch_shapes=list(scratch_shapes) + carry.scratch(),
        compiler_params=pltpu.CompilerParams(dimension_semantics=("arbitrary",) * len(grid)))(*args, *carry.arrays())
    return list(outs[:n_out]), list(outs[n_out:])


def _exchange(carry, name):
    n_car = len(carry.arrays())

    def body(*refs):
        cins, couts, sems = refs[:n_car], refs[n_car:2 * n_car], refs[2 * n_car:]
        carry.start(cins, couts, sems)
        if carry.gathers:
            carry.forward(cins, couts, sems)
        carry.finish(cins, couts, sems)

    return list(pl.pallas_call(body, name=name, in_specs=[ANY] * n_car, out_specs=[ANY] * n_car,
                               out_shape=carry.out_shapes(), scratch_shapes=carry.scratch())(*carry.arrays()))


def _rmsnorm(x, g):
    T, D = x.shape
    tm = _pick(T, 512, 16)

    def body(x_ref, g_ref, o_ref):
        xv = x_ref[...]
        o_ref[...] = (xv * _rms(xv) * g_ref[...]).astype(BF16)

    return _call(body, name="rmsnorm", grid=(T // tm,),
                 in_specs=[pl.BlockSpec((tm, D), lambda i: (i, 0)), pl.BlockSpec((1, D), lambda i: (0, 0))],
                 out_specs=[pl.BlockSpec((tm, D), lambda i: (i, 0))],
                 out_shape=[jax.ShapeDtypeStruct((T, D), BF16)], args=(x, g))[0][0]


def _ffn_gu(h, wgt, wut, carry=None):
    T, D = h.shape
    F = wgt.shape[0]
    tm = _pick(T, 1024, 16)
    tn = _pick(F, 512, 128)

    def body(h_ref, wg_ref, wu_ref, a_ref, p_ref, q_ref):
        hv = h_ref[...]
        g = lax.dot_general(hv, wg_ref[...], NT_DIMS, preferred_element_type=F32)
        u = lax.dot_general(hv, wu_ref[...], NT_DIMS, preferred_element_type=F32)
        sg = jax.nn.sigmoid(g)
        q = g * sg
        a_ref[...] = (q * u).astype(BF16)
        p_ref[...] = (u * (sg * (1.0 + g * (1.0 - sg)))).astype(BF16)
        q_ref[...] = q.astype(BF16)

    out = jax.ShapeDtypeStruct((T, F), BF16)
    tile = pl.BlockSpec((tm, tn), lambda i, j: (i, j))
    wspec = pl.BlockSpec((tn, D), lambda i, j: (j, 0))
    return _call(body, name="ffn_gu", grid=(T // tm, F // tn),
                 in_specs=[pl.BlockSpec((tm, D), lambda i, j: (i, 0)), wspec, wspec],
                 out_specs=[tile, tile, tile], out_shape=[out, out, out], args=(h, wgt, wut), carry=carry)


def _mm_nt_plain(a, wt, carry=None):
    T, K = a.shape
    N = wt.shape[0]
    tm = _pick(T, 1024, 16)
    tn = _pick(N, 512, 128)

    def body(a_ref, w_ref, o_ref):
        o_ref[...] = lax.dot_general(a_ref[...], w_ref[...], NT_DIMS, preferred_element_type=F32).astype(BF16)

    return _call(body, name="mm_nt_plain", grid=(T // tm, N // tn),
                 in_specs=[pl.BlockSpec((tm, K), lambda i, j: (i, 0)), pl.BlockSpec((tn, K), lambda i, j: (j, 0))],
                 out_specs=[pl.BlockSpec((tm, tn), lambda i, j: (i, j))],
                 out_shape=[jax.ShapeDtypeStruct((T, N), BF16)], args=(a, wt), carry=carry)


def _mm_res_norm(a, w, x, gnext, scale, carry=None):
    T, K = a.shape
    D = w.shape[1]
    tm = _pick(T, 512, 16)
    tk = _pick(K, 512, 128)
    nk = K // tk

    def body(a_ref, w_ref, x_ref, g_ref, xo_ref, ho_ref, acc):
        k = pl.program_id(1)

        @pl.when(k == 0)
        def _():
            acc[...] = jnp.zeros_like(acc)

        acc[...] += jnp.dot(a_ref[...], w_ref[...], preferred_element_type=F32)

        @pl.when(k == nk - 1)
        def _():
            xn = x_ref[...] + scale * acc[...]
            xo_ref[...] = xn
            ho_ref[...] = (xn * _rms(xn) * g_ref[...]).astype(BF16)

    row = pl.BlockSpec((tm, D), lambda i, k: (i, 0))
    return _call(body, name="mm_res_norm", grid=(T // tm, nk),
                 in_specs=[pl.BlockSpec((tm, tk), lambda i, k: (i, k)), pl.BlockSpec((tk, D), lambda i, k: (k, 0)),
                           row, pl.BlockSpec((1, D), lambda i, k: (0, 0))],
                 out_specs=[row, row],
                 out_shape=[jax.ShapeDtypeStruct((T, D), F32), jax.ShapeDtypeStruct((T, D), BF16)],
                 scratch_shapes=[pltpu.VMEM((tm, D), F32)], args=(a, w, x, gnext), carry=carry)


def _mm_nt(dx, w, scale, pq=None, carry=None):
    T, D = dx.shape
    N = w.shape[0]
    tm = _pick(T, 512, 16)
    tn = _pick(N, 512, 128)

    def body(*refs):
        if pq is None:
            dx_ref, w_ref, d_ref, dxb_ref, dxs = refs
        else:
            dx_ref, w_ref, p_ref, q_ref, dg_ref, du_ref, dxb_ref, dxs = refs
        j = pl.program_id(1)

        @pl.when(j == 0)
        def _():
            v = (dx_ref[...] * scale).astype(BF16)
            dxs[...] = v
            dxb_ref[...] = v

        d = lax.dot_general(dxs[...], w_ref[...], NT_DIMS, preferred_element_type=F32)
        if pq is None:
            d_ref[...] = d.astype(BF16)
        else:
            dg_ref[...] = (d * p_ref[...].astype(F32)).astype(BF16)
            du_ref[...] = (d * q_ref[...].astype(F32)).astype(BF16)

    row = pl.BlockSpec((tm, D), lambda i, j: (i, 0))
    tile = pl.BlockSpec((tm, tn), lambda i, j: (i, j))
    tile_shape = jax.ShapeDtypeStruct((T, N), BF16)
    row_shape = jax.ShapeDtypeStruct((T, D), BF16)
    in_specs = [row, pl.BlockSpec((tn, D), lambda i, j: (j, 0))]
    if pq is None:
        args, out_specs, out_shape = (dx, w), [tile, row], [tile_shape, row_shape]
    else:
        in_specs += [tile, tile]
        args, out_specs, out_shape = (dx, w, pq[0], pq[1]), [tile, tile, row], [tile_shape, tile_shape, row_shape]
    return _call(body, name="mm_nt" if pq is None else "ffn_da", grid=(T // tm, N // tn),
                 in_specs=in_specs, out_specs=out_specs, out_shape=out_shape,
                 scratch_shapes=[pltpu.VMEM((tm, D), BF16)], args=args, carry=carry)


def _mm_tn(as_, b, tm_pref, carry=None):
    T, M = as_[0].shape
    N = b.shape[1]
    na = len(as_)
    tm = _pick(M, tm_pref, 128)
    tk = _pick(T, 512, 16)
    nk = T // tk

    def body(*refs):
        a_refs, b_ref, o_refs, accs = refs[:na], refs[na], refs[na + 1:2 * na + 1], refs[2 * na + 1:]
        k = pl.program_id(1)
        bv = b_ref[...]
        for a_ref, o_ref, acc in zip(a_refs, o_refs, accs):
            p = lax.dot_general(a_ref[...], bv, TN_DIMS, preferred_element_type=F32)

            @pl.when(k == 0)
            def _():
                acc[...] = p

            @pl.when(k > 0)
            def _():
                acc[...] += p

            @pl.when(k == nk - 1)
            def _():
                o_ref[...] = acc[...].astype(BF16)

    return _call(body, name="mm_tn%d" % na, grid=(M // tm, nk),
                 in_specs=[pl.BlockSpec((tk, tm), lambda i, k: (k, i))] * na + [pl.BlockSpec((tk, N), lambda i, k: (k, 0))],
                 out_specs=[pl.BlockSpec((tm, N), lambda i, k: (i, 0))] * na,
                 out_shape=[jax.ShapeDtypeStruct((M, N), BF16)] * na,
                 scratch_shapes=[pltpu.VMEM((tm, N), F32)] * na, args=(*as_, b), carry=carry)


def _dh_norm_bwd(pairs, x, g, dres, carry=None):
    T, D = x.shape
    K = pairs[0][0].shape[1]
    npair = len(pairs)
    tm = _pick(T, 512, 16)
    tk = _pick(K, 512, 128)
    nk = K // tk

    def body(*refs):
        dw_refs = refs[:2 * npair]
        x_ref, g_ref, dres_ref, dx_ref, dg_ref, acc = refs[2 * npair:]
        i = pl.program_id(0)
        k = pl.program_id(1)

        @pl.when(k == 0)
        def _():
            acc[...] = jnp.zeros_like(acc)

        for n in range(npair):
            acc[...] += jnp.dot(dw_refs[2 * n][...], dw_refs[2 * n + 1][...], preferred_element_type=F32)

        @pl.when(k == nk - 1)
        def _():
            dxn, dgp = _norm_bwd(acc[...], x_ref[...], g_ref[...])
            dx_ref[...] = dres_ref[...] + dxn

            @pl.when(i == 0)
            def _():
                dg_ref[...] = dgp

            @pl.when(i > 0)
            def _():
                dg_ref[...] += dgp

    row = pl.BlockSpec((tm, D), lambda i, k: (i, 0))
    vec = pl.BlockSpec((1, D), lambda i, k: (0, 0))
    in_specs, args = [], []
    for d, wt in pairs:
        in_specs += [pl.BlockSpec((tm, tk), lambda i, k: (i, k)), pl.BlockSpec((tk, D), lambda i, k: (k, 0))]
        args += [d, wt]
    return _call(body, name="dh_norm_bwd%d" % npair, grid=(T // tm, nk),
                 in_specs=in_specs + [row, vec, row], out_specs=[row, vec],
                 out_shape=[jax.ShapeDtypeStruct((T, D), F32), jax.ShapeDtypeStruct((1, D), F32)],
                 scratch_shapes=[pltpu.VMEM((tm, D), F32)], args=(*args, x, g, dres), carry=carry)


def _loss_bwd(x, g, tgt):
    T, D = x.shape
    tm = _pick(T, 512, 16)

    def body(x_ref, g_ref, t_ref, loss_ref, dx_ref, dg_ref):
        i = pl.program_id(0)
        xv = x_ref[...]
        gv = g_ref[...]
        e = xv * _rms(xv) * gv - t_ref[...]
        part = jnp.sum(jnp.sum(e * e, axis=-1, keepdims=True), axis=0, keepdims=True) * (0.5 / D)
        dxn, dgp = _norm_bwd(e * (1.0 / D), xv, gv)
        dx_ref[...] = dxn

        @pl.when(i == 0)
        def _():
            loss_ref[...] = jnp.broadcast_to(part, loss_ref.shape)
            dg_ref[...] = dgp

        @pl.when(i > 0)
        def _():
            loss_ref[...] += jnp.broadcast_to(part, loss_ref.shape)
            dg_ref[...] += dgp

    row = pl.BlockSpec((tm, D), lambda i: (i, 0))
    vec = pl.BlockSpec((1, D), lambda i: (0, 0))
    return _call(body, name="loss_bwd", grid=(T // tm,), in_specs=[row, vec, row],
                 out_specs=[pl.BlockSpec((1, 128), lambda i: (0, 0)), row, vec],
                 out_shape=[jax.ShapeDtypeStruct((1, 128), F32), jax.ShapeDtypeStruct((T, D), F32),
                            jax.ShapeDtypeStruct((1, D), F32)], args=(x, g, tgt))[0]


def _sg_fwd(z, ws, bb, gn):
    T = z.shape[0]

    def body(zu_ref, zv_ref, ws_ref, bb_ref, gn_ref, a_ref):
        for h in range(SG_HEADS):
            sl = slice(h * HEAD_DIM, (h + 1) * HEAD_DIM)
            gv = _gelu(zv_ref[:, sl].astype(F32))
            vn = (gv * _rms(gv) * gn_ref[:, sl]).astype(BF16)
            mixed = jnp.dot(ws_ref[h], vn, preferred_element_type=F32) + bb_ref[h]
            a_ref[:, sl] = (_gelu(zu_ref[:, sl].astype(F32)) * mixed).astype(BF16)

    full = lambda shape: pl.BlockSpec(shape, lambda n: (0,) * len(shape))
    return _call(body, name="sg_fwd", grid=(T // SG_CHUNK,),
                 in_specs=[pl.BlockSpec((SG_CHUNK, SG_WIDTH), lambda n: (n, 0)),
                           pl.BlockSpec((SG_CHUNK, SG_WIDTH), lambda n: (n, 1)),
                           full((SG_HEADS, SG_CHUNK, SG_CHUNK)), full((SG_HEADS, SG_CHUNK, HEAD_DIM)),
                           full((1, SG_WIDTH))],
                 out_specs=[pl.BlockSpec((SG_CHUNK, SG_WIDTH), lambda n: (n, 0))],
                 out_shape=[jax.ShapeDtypeStruct((T, SG_WIDTH), BF16)], args=(z, z, ws, bb, gn))[0][0]


def _sg_bwd(z, dmix, ws, wst, bb, gn):
    T = z.shape[0]

    def body(zu_ref, zv_ref, da_ref, ws_ref, wst_ref, bb_ref, gn_ref, dzu_ref, dzv_ref, dws_ref, dbb_ref, dgn_ref):
        n = pl.program_id(0)

        @pl.when(n == 0)
        def _():
            dws_ref[...] = jnp.zeros_like(dws_ref)
            dbb_ref[...] = jnp.zeros_like(dbb_ref)
            dgn_ref[...] = jnp.zeros_like(dgn_ref)

        for h in range(SG_HEADS):
            sl = slice(h * HEAD_DIM, (h + 1) * HEAD_DIM)
            u = zu_ref[:, sl].astype(F32)
            v = zv_ref[:, sl].astype(F32)
            da = da_ref[:, sl].astype(F32)
            gain = gn_ref[:, sl]
            gv = _gelu(v)
            r = _rms(gv)
            vn = (gv * r * gain).astype(BF16)
            mixed = jnp.dot(ws_ref[h], vn, preferred_element_type=F32) + bb_ref[h]
            dmixed = da * _gelu(u)
            dzu_ref[:, sl] = (da * mixed * _gelu_grad(u)).astype(BF16)
            dmb = dmixed.astype(BF16)
            dws_ref[h] += lax.dot_general(dmb, vn, NT_DIMS, preferred_element_type=F32)
            dbb_ref[h] += jnp.broadcast_to(jnp.sum(dmixed, axis=-1, keepdims=True), (SG_CHUNK, HEAD_DIM))
            dvn = jnp.dot(wst_ref[h], dmb, preferred_element_type=F32)
            dgv, dg = _norm_bwd(dvn, gv, gain)
            dgn_ref[:, sl] += dg
            dzv_ref[:, sl] = (dgv * _gelu_grad(v)).astype(BF16)

    full = lambda shape: pl.BlockSpec(shape, lambda n: (0,) * len(shape))
    wspec = full((SG_HEADS, SG_CHUNK, SG_CHUNK))
    tile = lambda c: pl.BlockSpec((SG_CHUNK, SG_WIDTH), lambda n: (n, c))
    return _call(body, name="sg_bwd", grid=(T // SG_CHUNK,),
                 in_specs=[tile(0), tile(1), tile(0), wspec, wspec, full((SG_HEADS, SG_CHUNK, HEAD_DIM)),
                           full((1, SG_WIDTH))],
                 out_specs=[tile(0), tile(0), wspec, full((SG_HEADS, SG_CHUNK, HEAD_DIM)), full((1, SG_WIDTH))],
                 out_shape=[jax.ShapeDtypeStruct((T, SG_WIDTH), BF16), jax.ShapeDtypeStruct((T, SG_WIDTH), BF16),
                            jax.ShapeDtypeStruct((SG_HEADS, SG_CHUNK, SG_CHUNK), F32),
                            jax.ShapeDtypeStruct((SG_HEADS, SG_CHUNK, HEAD_DIM), F32),
                            jax.ShapeDtypeStruct((1, SG_WIDTH), F32)], args=(z, z, dmix, ws, wst, bb, gn))[0]


def _pool_specs(T, tp, col):
    step = tp // POOL_HALO
    last = T // POOL_HALO - 1
    return [pl.BlockSpec((POOL_HALO, POOL_WIDTH), lambda i: (jnp.maximum(i * step - 1, 0), col)),
            pl.BlockSpec((tp, POOL_WIDTH), lambda i: (i, col)),
            pl.BlockSpec((POOL_HALO, POOL_WIDTH), lambda i: (jnp.minimum((i + 1) * step, last), col))]


def _pool_band(i, tp, T, win):
    ext = tp + 2 * POOL_HALO
    t = i * tp + lax.broadcasted_iota(jnp.int32, (tp, ext), 0)
    s = i * tp - POOL_HALO + lax.broadcasted_iota(jnp.int32, (tp, ext), 1)
    band = (s >= jnp.maximum(t - win // 2, 0)) & (s < jnp.minimum(t + win // 2, T))
    t1 = i * tp + lax.broadcasted_iota(jnp.int32, (tp, 1), 0)
    cnt = (jnp.minimum(t1 + win // 2, T) - jnp.maximum(t1 - win // 2, 0)).astype(F32)
    return band.astype(BF16), cnt


def _pool_fwd(z, pw, psc):
    T = z.shape[0]
    tp = _pick(T, 256, POOL_HALO)

    def body(pp_ref, pc_ref, pn_ref, w_ref, sc_ref, o_ref):
        i = pl.program_id(0)
        halo = jnp.concatenate([pp_ref[...], pc_ref[...], pn_ref[...]], axis=0)
        for g, win in enumerate(POOL_WINDOWS):
            sl = slice(g * HEAD_DIM, (g + 1) * HEAD_DIM)
            band, cnt = _pool_band(i, tp, T, win)
            ssum = jnp.dot(band, halo[:, sl], preferred_element_type=F32)
            d = ssum / cnt - pc_ref[:, sl].astype(F32)
            y = jnp.dot(d.astype(BF16), w_ref[g], preferred_element_type=F32) * sc_ref[:, sl]
            o_ref[:, sl] = y.astype(BF16)

    full = lambda shape: pl.BlockSpec(shape, lambda i: (0,) * len(shape))
    return _call(body, name="pool_fwd", grid=(T // tp,),
                 in_specs=_pool_specs(T, tp, 2) + [full((4, HEAD_DIM, HEAD_DIM)), full((1, POOL_WIDTH))],
                 out_specs=[pl.BlockSpec((tp, POOL_WIDTH), lambda i: (i, 0))],
                 out_shape=[jax.ShapeDtypeStruct((T, POOL_WIDTH), BF16)], args=(z, z, z, pw, psc))[0][0]


def _pool_bwd(z, dmix, pw, psc):
    T = z.shape[0]
    tp = _pick(T, 256, POOL_HALO)
    ext = tp + 2 * POOL_HALO

    def body(pp_ref, pc_ref, pn_ref, dp_ref, dc_ref, dn_ref, w_ref, sc_ref, dz_ref, dw_ref, dsc_ref):
        i = pl.program_id(0)

        @pl.when(i == 0)
        def _():
            dw_ref[...] = jnp.zeros_like(dw_ref)
            dsc_ref[...] = jnp.zeros_like(dsc_ref)

        halo = jnp.concatenate([pp_ref[...], pc_ref[...], pn_ref[...]], axis=0)
        dy_halo = jnp.concatenate([dp_ref[...], dc_ref[...], dn_ref[...]], axis=0)
        th = i * tp - POOL_HALO + lax.broadcasted_iota(jnp.int32, (ext, 1), 0)
        inside = (th >= 0) & (th < T)
        s2 = i * tp + lax.broadcasted_iota(jnp.int32, (tp, ext), 0)
        t2 = i * tp - POOL_HALO + lax.broadcasted_iota(jnp.int32, (tp, ext), 1)
        for g, win in enumerate(POOL_WINDOWS):
            sl = slice(g * HEAD_DIM, (g + 1) * HEAD_DIM)
            sc = sc_ref[:, sl]
            band, cnt = _pool_band(i, tp, T, win)
            ssum = jnp.dot(band, halo[:, sl], preferred_element_type=F32)
            db = (ssum / cnt - pc_ref[:, sl].astype(F32)).astype(BF16)
            yraw = jnp.dot(db, w_ref[g], preferred_element_type=F32)
            dyc = dc_ref[:, sl].astype(F32)
            dsc_ref[:, sl] += jnp.sum(dyc * yraw, axis=0, keepdims=True)
            dw_ref[g] += lax.dot_general(db, (dyc * sc).astype(BF16), TN_DIMS, preferred_element_type=F32)
            dd = lax.dot_general((dy_halo[:, sl].astype(F32) * sc).astype(BF16), w_ref[g], NT_DIMS,
                                 preferred_element_type=F32)
            cnt_h = (jnp.minimum(th + win // 2, T) - jnp.maximum(th - win // 2, 0)).astype(F32)
            ddc = jnp.where(inside, dd / jnp.maximum(cnt_h, 1.0), 0.0)
            hi = ddc.astype(BF16)
            lo = (ddc - hi.astype(F32)).astype(BF16)
            band_t = ((s2 >= jnp.maximum(t2 - win // 2, 0)) & (s2 < jnp.minimum(t2 + win // 2, T))).astype(BF16)
            dpool = (jnp.dot(band_t, hi, preferred_element_type=F32) + jnp.dot(band_t, lo, preferred_element_type=F32)
                     - dd[POOL_HALO:POOL_HALO + tp])
            dz_ref[:, sl] = dpool.astype(BF16)

    full = lambda shape: pl.BlockSpec(shape, lambda i: (0,) * len(shape))
    return _call(body, name="pool_bwd", grid=(T // tp,),
                 in_specs=_pool_specs(T, tp, 2) + _pool_specs(T, tp, 1)
                 + [full((4, HEAD_DIM, HEAD_DIM)), full((1, POOL_WIDTH))],
                 out_specs=[pl.BlockSpec((tp, POOL_WIDTH), lambda i: (i, 0)), full((4, HEAD_DIM, HEAD_DIM)),
                            full((1, POOL_WIDTH))],
                 out_shape=[jax.ShapeDtypeStruct((T, POOL_WIDTH), BF16),
                            jax.ShapeDtypeStruct((4, HEAD_DIM, HEAD_DIM), F32),
                            jax.ShapeDtypeStruct((1, POOL_WIDTH), F32)], args=(z, z, z, dmix, dmix, dmix, pw, psc))[0]


ATT_ROWS = 8
WIN_KEYS = NA_KH * GRID_W


def _col_mask():
    q = lax.broadcasted_iota(jnp.int32, (GRID_W, WIN_KEYS), 0)
    k = lax.broadcasted_iota(jnp.int32, (GRID_W, WIN_KEYS), 1) & (GRID_W - 1)
    start = jnp.clip(q - NA_KW // 2, 0, GRID_W - NA_KW)
    return (k >= start) & (k < start + NA_KW)


def _attn_probs(q, kw, bias, mask):
    s = lax.dot_general(q, kw, NT_DIMS, preferred_element_type=F32) * (HEAD_DIM ** -0.5) + bias
    s = jnp.where(mask, s, NEG)
    p = jnp.exp(s - jnp.max(s, axis=-1, keepdims=True))
    return p / jnp.sum(p, axis=-1, keepdims=True)


def _attn_window(step, a, rows):
    r = step * ATT_ROWS + a
    sr = jnp.clip(r - NA_KH // 2, 0, rows - NA_KH)
    return pl.multiple_of(sr * GRID_W, GRID_W), sr - r + NA_KH - 1


def _attn_fwd(z, ecat):
    T = z.shape[0]
    rows = T // GRID_W
    blk = ATT_ROWS * GRID_W

    def body(q_ref, k_ref, v_ref, e_ref, o_ref):
        step = pl.program_id(1)
        mask = _col_mask()
        for a in range(ATT_ROWS):
            start, dr0 = _attn_window(step, a, rows)
            qs = slice(a * GRID_W, (a + 1) * GRID_W)
            pr = _attn_probs(q_ref[qs, :], k_ref[pl.ds(start, WIN_KEYS), :], e_ref[0, dr0], mask)
            o = jnp.dot(pr.astype(BF16), v_ref[pl.ds(start, WIN_KEYS), :], preferred_element_type=F32)
            o_ref[qs, :] = o.astype(BF16)

    col = lambda off: pl.BlockSpec((T, HEAD_DIM), lambda h, s: (0, off // HEAD_DIM + h))
    return _call(body, name="attn_fwd", grid=(NA_HEADS, rows // ATT_ROWS),
                 in_specs=[pl.BlockSpec((blk, HEAD_DIM), lambda h, s: (s, Q_OFF // HEAD_DIM + h)), col(K_OFF),
                           col(V_OFF), pl.BlockSpec((1, NA_KH, GRID_W, WIN_KEYS), lambda h, s: (h, 0, 0, 0))],
                 out_specs=[pl.BlockSpec((blk, HEAD_DIM), lambda h, s: (s, h))],
                 out_shape=[jax.ShapeDtypeStruct((T, NA_WIDTH), BF16)], args=(z, z, z, ecat))[0][0]


def _attn_bwd(z, dmix, ecat, carry=None):
    T = z.shape[0]
    rows = T // GRID_W
    blk = ATT_ROWS * GRID_W
    nstep = rows // ATT_ROWS

    def body(q_ref, k_ref, v_ref, do_ref, e_ref, dq_ref, dk_ref, dv_ref, de_ref, dk_acc, dv_acc):
        step = pl.program_id(1)
        mask = _col_mask()

        @pl.when(step == 0)
        def _():
            dk_acc[...] = jnp.zeros_like(dk_acc)
            dv_acc[...] = jnp.zeros_like(dv_acc)
            de_ref[...] = jnp.zeros_like(de_ref)

        for a in range(ATT_ROWS):
            start, dr0 = _attn_window(step, a, rows)
            qs = slice(a * GRID_W, (a + 1) * GRID_W)
            win = pl.ds(start, WIN_KEYS)
            q = q_ref[qs, :]
            kw = k_ref[win, :]
            do = do_ref[qs, :]
            pr = _attn_probs(q, kw, e_ref[0, dr0], mask)
            dp = lax.dot_general(do, v_ref[win, :], NT_DIMS, preferred_element_type=F32)
            dv_acc[win, :] += lax.dot_general(pr.astype(BF16), do, TN_DIMS, preferred_element_type=F32)
            ds = pr * (dp - jnp.sum(dp * pr, axis=-1, keepdims=True))
            de_ref[0, dr0] += ds
            dsb = (ds * (HEAD_DIM ** -0.5)).astype(BF16)
            dq_ref[qs, :] = jnp.dot(dsb, kw, preferred_element_type=F32).astype(BF16)
            dk_acc[win, :] += lax.dot_general(dsb, q, TN_DIMS, preferred_element_type=F32)

        @pl.when(step == nstep - 1)
        def _():
            dk_ref[...] = dk_acc[...].astype(BF16)
            dv_ref[...] = dv_acc[...].astype(BF16)

    col = lambda off: pl.BlockSpec((T, HEAD_DIM), lambda h, s: (0, off // HEAD_DIM + h))
    e_spec = pl.BlockSpec((1, NA_KH, GRID_W, WIN_KEYS), lambda h, s: (h, 0, 0, 0))
    out = jax.ShapeDtypeStruct((T, NA_WIDTH), BF16)
    return _call(body, name="attn_bwd", grid=(NA_HEADS, nstep),
                 in_specs=[pl.BlockSpec((blk, HEAD_DIM), lambda h, s: (s, Q_OFF // HEAD_DIM + h)), col(K_OFF),
                           col(V_OFF),
                           pl.BlockSpec((blk, HEAD_DIM), lambda h, s: (s, (SG_WIDTH + POOL_WIDTH) // HEAD_DIM + h)),
                           e_spec],
                 out_specs=[pl.BlockSpec((blk, HEAD_DIM), lambda h, s: (s, h)), col(0), col(0), e_spec],
                 out_shape=[out, out, out, jax.ShapeDtypeStruct((NA_HEADS, NA_KH, GRID_W, WIN_KEYS), F32)],
                 scratch_shapes=[pltpu.VMEM((T, HEAD_DIM), F32), pltpu.VMEM((T, HEAD_DIM), F32)],
                 args=(z, z, z, dmix, ecat), carry=carry)


def _rpb_onehot():
    col = jnp.arange(GRID_W)
    dc = jnp.clip(col[None, :] - col[:, None] + NA_KW - 1, 0, 2 * NA_KW - 2)
    return (dc[None] == jnp.arange(2 * NA_KW - 1)[:, None, None]).astype(F32)


def _rpb_expand(rpb):
    e = jnp.einsum("hrc,cqk->hrqk", rpb, _rpb_onehot(), precision=lax.Precision.HIGHEST)
    return jnp.stack([jnp.concatenate([e[:, d + j] for j in range(NA_KH)], axis=-1) for d in range(NA_KH)], axis=1)


def _rpb_collect(decat):
    parts = decat.reshape(NA_HEADS, NA_KH, GRID_W, NA_KH, GRID_W)
    de = [sum(parts[:, d, :, r - d] for d in range(NA_KH) if 0 <= r - d < NA_KH) for r in range(2 * NA_KH - 1)]
    return jnp.einsum("hrqk,cqk->hrc", jnp.stack(de, axis=1), _rpb_onehot(), precision=lax.Precision.HIGHEST)


def _adamw(w, g, m, v):
    shape = w.shape
    C = shape[-1]
    R = w.size // C
    tr = _pick(R, max(8, (1 << 18) // C), 8)
    args = [a.reshape(R, C) for a in (w, g, m, v)]

    def body(w_ref, g_ref, m_ref, v_ref, d_ref, mo_ref, vo_ref):
        gv = g_ref[...]
        mn = ADAM_B1 * m_ref[...] + (1.0 - ADAM_B1) * gv
        vn = ADAM_B2 * v_ref[...] + (1.0 - ADAM_B2) * (gv * gv)
        m_hat = mn / (1.0 - ADAM_B1 ** ADAM_STEP)
        v_hat = vn / (1.0 - ADAM_B2 ** ADAM_STEP)
        d_ref[...] = -ADAM_LR * (m_hat / (jnp.sqrt(v_hat) + ADAM_EPS) + ADAM_WD * w_ref[...])
        mo_ref[...] = mn
        vo_ref[...] = vn

    spec = pl.BlockSpec((tr, C), lambda i: (i, 0))
    out = jax.ShapeDtypeStruct((R, C), F32)
    res = _call(body, name="adamw", grid=(R // tr,), in_specs=[spec] * 4, out_specs=[spec] * 3, out_shape=[out] * 3,
                args=args)[0]
    return [r.reshape(shape) for r in res]


def _sum_devices(g):
    _, R, C = g.shape
    tr = _pick(R, max(16, (1 << 18) // C), 16)

    def body(g_ref, o_ref):
        acc = g_ref[0].astype(F32)
        for k in range(1, N_DEV):
            acc = acc + g_ref[k].astype(F32)
        o_ref[...] = acc

    return _call(body, name="sum_devices", grid=(R // tr,),
                 in_specs=[pl.BlockSpec((N_DEV, tr, C), lambda i: (0, i, 0))],
                 out_specs=[pl.BlockSpec((tr, C), lambda i: (i, 0))],
                 out_shape=[jax.ShapeDtypeStruct((R, C), F32)], args=(g,))[0][0]


BIG = ("ffn1_w_gate", "ffn1_w_up", "ffn1_w_down", "w_in", "w_out", "ffn2_w_gate", "ffn2_w_up", "ffn2_w_down")
COL_SHARDED = ("ffn1_w_gate", "ffn1_w_up", "w_in", "ffn2_w_gate", "ffn2_w_up")
SMALL = ("ffn1_norm", "mix_norm", "sg_norm", "sg_w", "sg_b", "pool_w", "pool_scale", "na_rpb", "ffn2_norm")


def kernel(x, ffn1_norm, ffn1_w_gate, ffn1_w_up, ffn1_w_down, mix_norm, w_in, sg_norm, sg_w, sg_b, pool_w, pool_scale, na_rpb, w_out, ffn2_norm, ffn2_w_gate, ffn2_w_up, ffn2_w_down, final_norm, loss_target, m_ffn1_norm, m_ffn1_w_gate, m_ffn1_w_up, m_ffn1_w_down, m_mix_norm, m_w_in, m_sg_norm, m_sg_w, m_sg_b, m_pool_w, m_pool_scale, m_na_rpb, m_w_out, m_ffn2_norm, m_ffn2_w_gate, m_ffn2_w_up, m_ffn2_w_down, m_final_norm, v_ffn1_norm, v_ffn1_w_gate, v_ffn1_w_up, v_ffn1_w_down, v_mix_norm, v_w_in, v_sg_norm, v_sg_w, v_sg_b, v_pool_w, v_pool_scale, v_na_rpb, v_w_out, v_ffn2_norm, v_ffn2_w_gate, v_ffn2_w_up, v_ffn2_w_down, v_final_norm):
    given = dict(locals())
    T, D = x.shape[1], x.shape[2]
    L = ffn1_norm.shape[0]
    assert x.shape[0] == 1 and D == SG_WIDTH + POOL_WIDTH + NA_WIDTH and w_in.shape[2] * N_DEV == Z_COLS
    assert T % (ATT_ROWS * GRID_W) == 0 and T // GRID_W >= NA_KH
    x0 = x.reshape(T, D)
    tgt = loss_target.reshape(T, D)

    def shard_rows(name, l):
        w = given[name][l]
        return (w.T if name in COL_SHARDED else w).astype(BF16)

    def full(gathered):
        return gathered.reshape(-1, D)

    def next_weights(l, names):
        return _Carry(gathers=[shard_rows(n, l + 1) for n in names]) if l + 1 < L else None

    W = dict(zip(BIG, map(full, _exchange(_Carry(gathers=[shard_rows(n, 0) for n in BIG]), "gather_first_layer"))))
    saved = []
    xc = x0
    h = _rmsnorm(xc, ffn1_norm[0:1])
    for l in range(L):
        s = dict(W=W, x0=xc, h1=h)
        Wn = {}

        def take(names, outs):
            Wn.update(zip(names, map(full, outs)))

        (s["a1"], s["p1"], s["q1"]), got = _ffn_gu(h, W["ffn1_w_gate"], W["ffn1_w_up"],
                                                   next_weights(l, ["ffn1_w_gate", "ffn1_w_up"]))
        take(["ffn1_w_gate", "ffn1_w_up"], got)
        (xc, h), got = _mm_res_norm(s["a1"], W["ffn1_w_down"], xc, mix_norm[l:l + 1], 0.5,
                                    next_weights(l, ["ffn1_w_down"]))
        take(["ffn1_w_down"], got)
        s["x1"], s["h2"] = xc, h
        (z,), got = _mm_nt_plain(h, W["w_in"], next_weights(l, ["w_in"]))
        take(["w_in"], got)
        s["ws"] = sg_w[l].astype(BF16)
        s["wst"] = jnp.swapaxes(sg_w[l], 1, 2).astype(BF16)
        s["bb"] = jnp.broadcast_to(sg_b[l][:, :, None], (SG_HEADS, SG_CHUNK, HEAD_DIM))
        s["gn"] = sg_norm[l:l + 1]
        s["pw"] = pool_w[l].astype(BF16)
        s["psc"] = pool_scale[l:l + 1]
        s["ecat"] = _rpb_expand(na_rpb[l])
        mix = jnp.concatenate([_sg_fwd(z, s["ws"], s["bb"], s["gn"]), _pool_fwd(z, s["pw"], s["psc"]),
                               _attn_fwd(z, s["ecat"])], axis=1)
        s["z"], s["mix"] = z, mix
        (xc, h), got = _mm_res_norm(mix, W["w_out"], xc, ffn2_norm[l:l + 1], 1.0, next_weights(l, ["w_out"]))
        take(["w_out"], got)
        s["x2"], s["h3"] = xc, h
        (s["a2"], s["p2"], s["q2"]), got = _ffn_gu(h, W["ffn2_w_gate"], W["ffn2_w_up"],
                                                   next_weights(l, ["ffn2_w_gate", "ffn2_w_up"]))
        take(["ffn2_w_gate", "ffn2_w_up"], got)
        gnext = ffn1_norm[l + 1:l + 2] if l + 1 < L else final_norm.reshape(1, D)
        (xc, h), got = _mm_res_norm(s["a2"], W["ffn2_w_down"], xc, gnext, 0.5, next_weights(l, ["ffn2_w_down"]))
        take(["ffn2_w_down"], got)
        saved.append(s)
        W = Wn

    loss_row, dx, dg_final = _loss_bwd(xc, final_norm.reshape(1, D), tgt)
    loss = lax.psum(loss_row[0, 0], MESH_AXES)

    received = {n: [None] * L for n in BIG}
    small = {n: [None] * L for n in SMALL}
    pending = []

    def slots(g):
        return g.reshape(N_DEV, -1, D)

    def sent(names, l, outs):
        for n, o in zip(names, outs):
            received[n][l] = o

    for l in reversed(range(L)):
        s = saved[l]
        W = s["W"]
        (dgt, dut, dyh), _ = _mm_nt(dx, W["ffn2_w_down"], 0.5, pq=(s["p2"], s["q2"]))
        (g_down2,), _ = _mm_tn([s["a2"]], dyh, 1408)
        (dx, dgn), got = _dh_norm_bwd([(dgt, W["ffn2_w_gate"]), (dut, W["ffn2_w_up"])], s["x2"], ffn2_norm[l:l + 1], dx,
                                      _Carry(scatters=[slots(g_down2)]))
        sent(["ffn2_w_down"], l, got)
        small["ffn2_norm"][l] = dgn
        (g_gate2, g_up2), got = _mm_tn([dgt, dut], s["h3"], 512, _Carry(scatters=[slots(g) for _, _, g in pending]))
        for (n, pl_, _), o in zip(pending, got):
            received[n][pl_] = o
        pending = []

        (dmix, dxb), _ = _mm_nt(dx, W["w_out"], 1.0)
        (g_out,), _ = _mm_tn([s["mix"]], dxb, 1024)
        dzu, dzv, dws, dbb, dgn = _sg_bwd(s["z"], dmix, s["ws"], s["wst"], s["bb"], s["gn"])
        dzp, dpw, dpsc = _pool_bwd(s["z"], dmix, s["pw"], s["psc"])
        (dq, dk, dv, decat), got = _attn_bwd(s["z"], dmix, s["ecat"], _Carry(scatters=[slots(g_gate2), slots(g_up2)]))
        sent(["ffn2_w_gate", "ffn2_w_up"], l, got)
        small["sg_w"][l], small["sg_b"][l], small["sg_norm"][l] = dws, dbb[:, :, 0], dgn[0]
        small["pool_w"][l], small["pool_scale"][l] = dpw, dpsc[0]
        small["na_rpb"][l] = _rpb_collect(decat)
        dz = jnp.concatenate([dzu, dzv, dzp, dq, dk, dv], axis=1)
        (dx, dgn), got = _dh_norm_bwd([(dz, W["w_in"])], s["x1"], mix_norm[l:l + 1], dx, _Carry(scatters=[slots(g_out)]))
        sent(["w_out"], l, got)
        small["mix_norm"][l] = dgn
        (g_in,), _ = _mm_tn([dz], s["h2"], 1152)

        (dgt, dut, dyh), _ = _mm_nt(dx, W["ffn1_w_down"], 0.5, pq=(s["p1"], s["q1"]))
        (g_down1,), got = _mm_tn([s["a1"]], dyh, 1408, _Carry(scatters=[slots(g_in)]))
        sent(["w_in"], l, got)
        (dx, dgn), got = _dh_norm_bwd([(dgt, W["ffn1_w_gate"]), (dut, W["ffn1_w_up"])], s["x0"], ffn1_norm[l:l + 1], dx,
                                      _Carry(scatters=[slots(g_down1)]))
        sent(["ffn1_w_down"], l, got)
        small["ffn1_norm"][l] = dgn
        (g_gate1, g_up1), _ = _mm_tn([dgt, dut], s["h1"], 512)
        pending = [("ffn1_w_gate", l, g_gate1), ("ffn1_w_up", l, g_up1)]

    small_shapes = {n: given[n].shape for n in SMALL}
    small_shapes["final_norm"] = final_norm.shape
    flat = [jnp.stack([jnp.reshape(g, (-1,)) for g in small[n]]).reshape(-1) for n in SMALL] + [dg_final.reshape(-1)]
    sizes = [f.shape[0] for f in flat]
    total = sum(sizes)
    padded = -(-total // 2048) * 2048
    local = jnp.concatenate(flat + [jnp.zeros((padded - total,), F32)]).reshape(-1, 128)
    got = _exchange(_Carry(gathers=[local], scatters=[slots(g) for _, _, g in pending]), "exchange_last")
    for (n, pl_, _), o in zip(pending, got[1:]):
        received[n][pl_] = o
    summed = _sum_devices(got[0]).reshape(-1)
    grads, off = {}, 0
    for n, size in zip(list(SMALL) + ["final_norm"], sizes):
        grads[n] = summed[off:off + size].reshape(small_shapes[n])
        off += size
    for n in BIG:
        per_layer = [_sum_devices(received[n][l]) for l in range(L)]
        grads[n] = jnp.stack([g.T if n in COL_SHARDED else g for g in per_layer])

    names = ['ffn1_norm', 'ffn1_w_gate', 'ffn1_w_up', 'ffn1_w_down', 'mix_norm', 'w_in', 'sg_norm', 'sg_w', 'sg_b',
             'pool_w', 'pool_scale', 'na_rpb', 'w_out', 'ffn2_norm', 'ffn2_w_gate', 'ffn2_w_up', 'ffn2_w_down',
             'final_norm']
    delta, new_m, new_v = {}, {}, {}
    for n in names:
        delta[n], new_m[n], new_v[n] = _adamw(given[n], grads[n], given["m_" + n], given["v_" + n])
    return (loss, dx.reshape(1, T, D), *[grads[n] for n in names], *[delta[n] for n in names],
            *[new_m[n] for n in names], *[new_v[n] for n in names])
```

```python
import functools
import math

import jax
import jax.numpy as jnp
from jax import lax
from jax.experimental import pallas as pl
from jax.experimental.pallas import tpu as pltpu

F32 = jnp.float32
BF16 = jnp.bfloat16
EPS = 1e-6
NEG = -1e30

HEAD_DIM = 128
SG_WIDTH = 512
SG_HEADS = 4
SG_CHUNK = 128
POOL_WINDOWS = (2, 4, 8, 16)
POOL_WIDTH = 512
POOL_HALO = 128
NA_WIDTH = 1024
NA_HEADS = 8
NA_KH = 8
NA_KW = 16
GRID_W = 64
Z_COLS = 2 * SG_WIDTH + POOL_WIDTH + 3 * NA_WIDTH
Q_OFF = 2 * SG_WIDTH + POOL_WIDTH
K_OFF = Q_OFF + NA_WIDTH
V_OFF = K_OFF + NA_WIDTH

ADAM_LR = 0.001
ADAM_B1 = 0.9
ADAM_B2 = 0.999
ADAM_EPS = 1e-08
ADAM_WD = 0.01
ADAM_STEP = 10

N_DEV = 8
MESH_AXES = ("x", "y", "c")
MESH = pl.DeviceIdType.MESH
ANY = pl.BlockSpec(memory_space=pl.ANY)

NT_DIMS = (((1,), (1,)), ((), ()))
TN_DIMS = (((0,), (0,)), ((), ()))

ROWS_PREF = 1024
K_PREF = 1408


def _pick(n, pref, mult):
    best = None
    t = mult
    while t <= min(n, pref):
        if n % t == 0:
            best = t
        t += mult
    return n if best is None else best


def _gelu(x):
    return 0.5 * x * (1.0 + lax.erf(x * (1.0 / math.sqrt(2.0))))


def _gelu_grad(x):
    cdf = 0.5 * (1.0 + lax.erf(x * (1.0 / math.sqrt(2.0))))
    pdf = jnp.exp(-0.5 * x * x) * (1.0 / math.sqrt(2.0 * math.pi))
    return cdf + x * pdf


def _rms(x):
    return lax.rsqrt(jnp.mean(x * x, axis=-1, keepdims=True) + EPS)


def _norm_bwd(dh, x, g):
    r = _rms(x)
    w = dh * g
    dx = r * w - x * (r * r * r) * jnp.mean(w * x, axis=-1, keepdims=True)
    dg = jnp.sum(dh * (x * r), axis=0, keepdims=True)
    return dx, dg


def _position():
    return lax.axis_index("x"), lax.axis_index("y"), lax.axis_index("c")


def _index(p):
    return 4 * p[0] + 2 * p[1] + p[2]


class _Carry:
    def __init__(self, gathers=(), scatters=()):
        self.gathers, self.scatters = list(gathers), list(scatters)
        self.units = sum(g.shape[0] for g in self.gathers) + sum(len(groups) for _, groups in self.scatters)

    def arrays(self):
        return self.gathers + [s for s, _ in self.scatters]

    def out_shapes(self):
        return ([jax.ShapeDtypeStruct((g.shape[0], N_DEV) + g.shape[1:], g.dtype) for g in self.gathers]
                + [jax.ShapeDtypeStruct((len(groups),) + s.shape[1:], s.dtype) for s, groups in self.scatters])

    def scratch(self):
        return [pltpu.SemaphoreType.DMA((7 * self.units,)), pltpu.SemaphoreType.DMA((7 * self.units,)),
                pltpu.SemaphoreType.DMA((self.units,))]

    def _gather_copies(self, n, x_ref, out_ref, send, recv, local):
        x, y, c = _position()
        me, sibling = (x, y, c), (x, y, 1 - c)
        chips = [(1 - x, y), (x, 1 - y), (1 - x, 1 - y)]

        def copy(k, block, to, src=None):
            dst = out_ref.at[_index(block)]
            return pltpu.make_async_remote_copy(
                src_ref=dst if src is None else src, dst_ref=dst, send_sem=send.at[7 * n + k],
                recv_sem=recv.at[7 * n + k], device_id=to, device_id_type=MESH)

        return dict(
            mine=pltpu.make_async_copy(x_ref, out_ref.at[_index(me)], local.at[n]),
            first=[copy(0, me, sibling, x_ref)] + [copy(1 + j, me, (*ch, c), x_ref) for j, ch in enumerate(chips)],
            landed=[copy(1 + j, (*ch, c), me) for j, ch in enumerate(chips)],
            passed=[copy(4 + j, (*ch, c), sibling) for j, ch in enumerate(chips)],
            from_sibling=[copy(0, sibling, me)] + [copy(4 + j, (*ch, 1 - c), me) for j, ch in enumerate(chips)])

    def _scatter_copies(self, n, src_ref, out_ref, send, recv, local):
        x, y, c = _position()
        me = (x, y, c)
        sends, recvs = [], []
        for k in range(1, N_DEV):
            flip = lambda v, bit: 1 - v if bit else v
            peer = (flip(x, k & 4), flip(y, k & 2), flip(c, k & 1))
            sems = dict(send_sem=send.at[7 * n + k - 1], recv_sem=recv.at[7 * n + k - 1], device_id=peer,
                        device_id_type=MESH)
            sends.append(pltpu.make_async_remote_copy(src_ref=src_ref.at[_index(peer)], dst_ref=out_ref.at[_index(me)],
                                                      **sems))
            recvs.append(pltpu.make_async_remote_copy(src_ref=src_ref.at[_index(me)], dst_ref=out_ref.at[_index(peer)],
                                                      **sems))
        mine = pltpu.make_async_copy(src_ref.at[_index(me)], out_ref.at[_index(me)], local.at[n])
        return dict(mine=mine, sends=sends, recvs=recvs)

    def _pieces(self, ins, outs, sems):
        send, recv, local = sems
        gs, ss, unit = [], [], 0
        for n, g in enumerate(self.gathers):
            for t in range(g.shape[0]):
                gs.append(self._gather_copies(unit, ins[n].at[t], outs[n].at[t], send, recv, local))
                unit += 1
        for n, (_, groups) in enumerate(self.scatters, start=len(self.gathers)):
            for j, t in enumerate(groups):
                ss.append(self._scatter_copies(unit, ins[n].at[t], outs[n].at[j], send, recv, local))
                unit += 1
        return gs, ss

    def start(self, ins, outs, sems):
        gs, ss = self._pieces(ins, outs, sems)
        for g in gs:
            g["mine"].start()
            for cp in g["first"]:
                cp.start()
        for s in ss:
            s["mine"].start()
            for cp in s["sends"]:
                cp.start()

    def forward(self, ins, outs, sems):
        gs, _ = self._pieces(ins, outs, sems)
        for g in gs:
            for landed, passed in zip(g["landed"], g["passed"]):
                landed.wait_recv()
                passed.start()

    def finish(self, ins, outs, sems):
        gs, ss = self._pieces(ins, outs, sems)
        for g in gs:
            for cp in g["from_sibling"]:
                cp.wait_recv()
            for cp in g["first"] + g["passed"]:
                cp.wait_send()
            g["mine"].wait()
        for s in ss:
            for cp in s["recvs"]:
                cp.wait_recv()
            for cp in s["sends"]:
                cp.wait_send()
            s["mine"].wait()


def _call(body, *, name, grid, in_specs, out_specs, out_shape, args, scratch_shapes=(), carry=None):
    if carry is None or not carry.arrays():
        outs = pl.pallas_call(
            body, name=name, grid=grid, in_specs=in_specs, out_specs=out_specs, out_shape=out_shape,
            scratch_shapes=list(scratch_shapes),
            compiler_params=pltpu.CompilerParams(dimension_semantics=("arbitrary",) * len(grid)))(*args)
        return list(outs), []
    n_in, n_out, n_scr, n_car = len(in_specs), len(out_specs), len(scratch_shapes), len(carry.arrays())
    steps = math.prod(grid)
    middle = (steps * 6) // 10

    def wrapped(*refs):
        ins, refs = refs[:n_in], refs[n_in:]
        cins, refs = refs[:n_car], refs[n_car:]
        outs, refs = refs[:n_out], refs[n_out:]
        couts, refs = refs[:n_car], refs[n_car:]
        scr, sems = refs[:n_scr], refs[n_scr:]
        step = 0
        for d, size in enumerate(grid):
            step = step * size + pl.program_id(d)

        @pl.when(step == 0)
        def _():
            carry.start(cins, couts, sems)

        body(*ins, *outs, *scr)

        if carry.gathers:
            @pl.when(step == middle)
            def _():
                carry.forward(cins, couts, sems)

        @pl.when(step == steps - 1)
        def _():
            carry.finish(cins, couts, sems)

    outs = pl.pallas_call(
        wrapped, name=name + "_carry", grid=grid, in_specs=list(in_specs) + [ANY] * n_car,
        out_specs=list(out_specs) + [ANY] * n_car, out_shape=list(out_shape) + carry.out_shapes(),
        scratch_shapes=list(scratch_shapes) + carry.scratch(),
        compiler_params=pltpu.CompilerParams(dimension_semantics=("arbitrary",) * len(grid)))(*args, *carry.arrays())
    return list(outs[:n_out]), list(outs[n_out:])


def _exchange(carry, name):
    n_car = len(carry.arrays())

    def body(*refs):
        cins, couts, sems = refs[:n_car], refs[n_car:2 * n_car], refs[2 * n_car:]
        carry.start(cins, couts, sems)
        if carry.gathers:
            carry.forward(cins, couts, sems)
        carry.finish(cins, couts, sems)

    return list(pl.pallas_call(body, name=name, in_specs=[ANY] * n_car, out_specs=[ANY] * n_car,
                               out_shape=carry.out_shapes(), scratch_shapes=carry.scratch())(*carry.arrays()))


def _rmsnorm(x, g):
    T, D = x.shape
    tm = _pick(T, 512, 16)

    def body(x_ref, g_ref, o_ref):
        xv = x_ref[...]
        o_ref[...] = (xv * _rms(xv) * g_ref[...]).astype(BF16)

    return _call(body, name="rmsnorm", grid=(T // tm,),
                 in_specs=[pl.BlockSpec((tm, D), lambda i: (i, 0)), pl.BlockSpec((1, D), lambda i: (0, 0))],
                 out_specs=[pl.BlockSpec((tm, D), lambda i: (i, 0))],
                 out_shape=[jax.ShapeDtypeStruct((T, D), BF16)], args=(x, g))[0][0]


def _ffn_gu(h, wgu, carry=None):
    T, D = h.shape
    F = wgu.shape[1]
    tm = _pick(T, ROWS_PREF, 16)
    tn = _pick(F, 512, 128)

    def body(h_ref, wg_ref, wu_ref, a_ref, pq_ref):
        hv = h_ref[...]
        g = lax.dot_general(hv, wg_ref[0], NT_DIMS, preferred_element_type=F32)
        u = lax.dot_general(hv, wu_ref[0], NT_DIMS, preferred_element_type=F32)
        sg = jax.nn.sigmoid(g)
        q = g * sg
        a_ref[...] = (q * u).astype(BF16)
        pq_ref[0] = (u * (sg * (1.0 + g * (1.0 - sg)))).astype(BF16)
        pq_ref[1] = q.astype(BF16)

    return _call(body, name="ffn_gu", grid=(T // tm, F // tn),
                 in_specs=[pl.BlockSpec((tm, D), lambda i, j: (i, 0)),
                           pl.BlockSpec((1, tn, D), lambda i, j: (0, j, 0)),
                           pl.BlockSpec((1, tn, D), lambda i, j: (1, j, 0))],
                 out_specs=[pl.BlockSpec((tm, tn), lambda i, j: (i, j)),
                            pl.BlockSpec((2, tm, tn), lambda i, j: (0, i, j))],
                 out_shape=[jax.ShapeDtypeStruct((T, F), BF16), jax.ShapeDtypeStruct((2, T, F), BF16)],
                 args=(h, wgu, wgu), carry=carry)


def _mm_nt(a, w, scale, pq=None, carry=None):
    T, K = a.shape
    N = w.shape[0]
    tm = _pick(T, ROWS_PREF, 16)
    tn = _pick(N, 512, 128)

    def body(*refs):
        a_ref, w_ref = refs[:2]
        d = lax.dot_general(a_ref[...], w_ref[...], NT_DIMS, preferred_element_type=F32)
        if scale != 1.0:
            d = d * scale
        if pq is None:
            refs[2][...] = d.astype(BF16)
        else:
            pq_ref, o_ref = refs[2:]
            o_ref[0] = (d * pq_ref[0].astype(F32)).astype(BF16)
            o_ref[1] = (d * pq_ref[1].astype(F32)).astype(BF16)

    in_specs = [pl.BlockSpec((tm, K), lambda i, j: (i, 0)), pl.BlockSpec((tn, K), lambda i, j: (j, 0))]
    if pq is None:
        args, out_spec, out_shape = (a, w), pl.BlockSpec((tm, tn), lambda i, j: (i, j)), (T, N)
    else:
        in_specs.append(pl.BlockSpec((2, tm, tn), lambda i, j: (0, i, j)))
        args, out_spec, out_shape = (a, w, pq), pl.BlockSpec((2, tm, tn), lambda i, j: (0, i, j)), (2, T, N)
    return _call(body, name="mm_nt" if pq is None else "ffn_da", grid=(T // tm, N // tn),
                 in_specs=in_specs, out_specs=[out_spec], out_shape=[jax.ShapeDtypeStruct(out_shape, BF16)],
                 args=args, carry=carry)


def _mm_res_norm(a, w, x, gnext, scale, carry=None):
    T, K = a.shape
    D = w.shape[1]
    tm = _pick(T, 512, 16)
    tk = _pick(K, K_PREF, 128)
    nk = K // tk

    def body(a_ref, w_ref, x_ref, g_ref, xo_ref, ho_ref):
        k = pl.program_id(1)
        p = jnp.dot(a_ref[...], w_ref[...], preferred_element_type=F32)

        @pl.when(k == 0)
        def _():
            xo_ref[...] = p

        @pl.when(k > 0)
        def _():
            xo_ref[...] += p

        @pl.when(k == nk - 1)
        def _():
            xn = x_ref[...] + scale * xo_ref[...]
            xo_ref[...] = xn
            ho_ref[...] = (xn * _rms(xn) * g_ref[...]).astype(BF16)

    row = pl.BlockSpec((tm, D), lambda i, k: (i, 0))
    return _call(body, name="mm_res_norm", grid=(T // tm, nk),
                 in_specs=[pl.BlockSpec((tm, tk), lambda i, k: (i, k)), pl.BlockSpec((tk, D), lambda i, k: (k, 0)),
                           row, pl.BlockSpec((1, D), lambda i, k: (0, 0))],
                 out_specs=[row, row],
                 out_shape=[jax.ShapeDtypeStruct((T, D), F32), jax.ShapeDtypeStruct((T, D), BF16)],
                 args=(a, w, x, gnext), carry=carry)


def _mm_tn(a, b, scale, carry=None):
    G, T, M = a.shape
    N = b.shape[1]
    tm = _pick(M, 512, 128)
    tk = _pick(T, 2048, 16)
    nk = T // tk

    def body(a_ref, b_ref, o_ref, acc):
        k = pl.program_id(2)
        p = lax.dot_general(a_ref[0], b_ref[...], TN_DIMS, preferred_element_type=F32)

        @pl.when(k == 0)
        def _():
            acc[...] = p

        @pl.when(k > 0)
        def _():
            acc[...] += p

        @pl.when(k == nk - 1)
        def _():
            o_ref[0] = (acc[...] * scale).astype(BF16)

    return _call(body, name="mm_tn", grid=(G, M // tm, nk),
                 in_specs=[pl.BlockSpec((1, tk, tm), lambda g, i, k: (g, k, i)),
                           pl.BlockSpec((tk, N), lambda g, i, k: (k, 0))],
                 out_specs=[pl.BlockSpec((1, tm, N), lambda g, i, k: (g, i, 0))],
                 out_shape=[jax.ShapeDtypeStruct((G, M, N), BF16)],
                 scratch_shapes=[pltpu.VMEM((tm, N), F32)], args=(a, b), carry=carry)


def _dh_norm_bwd(d, wt, x, g, dres, carry=None):
    T, D = x.shape
    G, _, K = d.shape
    tm = _pick(T, 512, 16)
    tk = _pick(K, K_PREF, 128)
    nk = K // tk

    def body(d_ref, w_ref, x_ref, g_ref, dres_ref, dx_ref, dxb_ref, dg_ref):
        i = pl.program_id(0)
        k = pl.program_id(1)
        p = jnp.dot(d_ref[0], w_ref[0], preferred_element_type=F32)

        @pl.when(k == 0)
        def _():
            dx_ref[...] = p

        @pl.when(k > 0)
        def _():
            dx_ref[...] += p

        @pl.when(k == G * nk - 1)
        def _():
            dxn, dgp = _norm_bwd(dx_ref[...], x_ref[...], g_ref[...])
            dxv = dres_ref[...] + dxn
            dx_ref[...] = dxv
            dxb_ref[...] = dxv.astype(BF16)

            @pl.when(i == 0)
            def _():
                dg_ref[...] = dgp

            @pl.when(i > 0)
            def _():
                dg_ref[...] += dgp

    row = pl.BlockSpec((tm, D), lambda i, k: (i, 0))
    vec = pl.BlockSpec((1, D), lambda i, k: (0, 0))
    return _call(body, name="dh_norm_bwd", grid=(T // tm, G * nk),
                 in_specs=[pl.BlockSpec((1, tm, tk), lambda i, k: (k // nk, i, k % nk)),
                           pl.BlockSpec((1, tk, D), lambda i, k: (k // nk, k % nk, 0)), row, vec, row],
                 out_specs=[row, row, vec],
                 out_shape=[jax.ShapeDtypeStruct((T, D), F32), jax.ShapeDtypeStruct((T, D), BF16),
                            jax.ShapeDtypeStruct((1, D), F32)],
                 args=(d, wt, x, g, dres), carry=carry)


def _loss_bwd(x, g, tgt):
    T, D = x.shape
    tm = _pick(T, 512, 16)

    def body(x_ref, g_ref, t_ref, loss_ref, dx_ref, dxb_ref, dg_ref):
        i = pl.program_id(0)
        xv = x_ref[...]
        gv = g_ref[...]
        e = xv * _rms(xv) * gv - t_ref[...]
        part = jnp.sum(jnp.sum(e * e, axis=-1, keepdims=True), axis=0, keepdims=True) * (0.5 / D)
        dxn, dgp = _norm_bwd(e * (1.0 / D), xv, gv)
        dx_ref[...] = dxn
        dxb_ref[...] = dxn.astype(BF16)

        @pl.when(i == 0)
        def _():
            loss_ref[...] = jnp.broadcast_to(part, loss_ref.shape)
            dg_ref[...] = dgp

        @pl.when(i > 0)
        def _():
            loss_ref[...] += jnp.broadcast_to(part, loss_ref.shape)
            dg_ref[...] += dgp

    row = pl.BlockSpec((tm, D), lambda i: (i, 0))
    vec = pl.BlockSpec((1, D), lambda i: (0, 0))
    return _call(body, name="loss_bwd", grid=(T // tm,), in_specs=[row, vec, row],
                 out_specs=[pl.BlockSpec((1, 128), lambda i: (0, 0)), row, row, vec],
                 out_shape=[jax.ShapeDtypeStruct((1, 128), F32), jax.ShapeDtypeStruct((T, D), F32),
                            jax.ShapeDtypeStruct((T, D), BF16), jax.ShapeDtypeStruct((1, D), F32)],
                 args=(x, g, tgt))[0]


def _sg_fwd(z, ws, bb, gn):
    T = z.shape[0]

    def body(zu_ref, zv_ref, ws_ref, bb_ref, gn_ref, a_ref):
        for h in range(SG_HEADS):
            sl = slice(h * HEAD_DIM, (h + 1) * HEAD_DIM)
            gv = _gelu(zv_ref[:, sl].astype(F32))
            vn = (gv * _rms(gv) * gn_ref[:, sl]).astype(BF16)
            mixed = jnp.dot(ws_ref[h], vn, preferred_element_type=F32) + bb_ref[h]
            a_ref[:, sl] = (_gelu(zu_ref[:, sl].astype(F32)) * mixed).astype(BF16)

    full = lambda shape: pl.BlockSpec(shape, lambda n: (0,) * len(shape))
    return _call(body, name="sg_fwd", grid=(T // SG_CHUNK,),
                 in_specs=[pl.BlockSpec((SG_CHUNK, SG_WIDTH), lambda n: (n, 0)),
                           pl.BlockSpec((SG_CHUNK, SG_WIDTH), lambda n: (n, 1)),
                           full((SG_HEADS, SG_CHUNK, SG_CHUNK)), full((SG_HEADS, SG_CHUNK, HEAD_DIM)),
                           full((1, SG_WIDTH))],
                 out_specs=[pl.BlockSpec((SG_CHUNK, SG_WIDTH), lambda n: (n, 0))],
                 out_shape=[jax.ShapeDtypeStruct((T, SG_WIDTH), BF16)], args=(z, z, ws, bb, gn))[0][0]


def _sg_bwd(z, dmix, ws, wst, bb, gn):
    T = z.shape[0]

    def body(zu_ref, zv_ref, da_ref, ws_ref, wst_ref, bb_ref, gn_ref, dzu_ref, dzv_ref, dws_ref, dbb_ref, dgn_ref):
        n = pl.program_id(0)

        @pl.when(n == 0)
        def _():
            dws_ref[...] = jnp.zeros_like(dws_ref)
            dbb_ref[...] = jnp.zeros_like(dbb_ref)
            dgn_ref[...] = jnp.zeros_like(dgn_ref)

        for h in range(SG_HEADS):
            sl = slice(h * HEAD_DIM, (h + 1) * HEAD_DIM)
            u = zu_ref[:, sl].astype(F32)
            v = zv_ref[:, sl].astype(F32)
            da = da_ref[:, sl].astype(F32)
            gain = gn_ref[:, sl]
            gv = _gelu(v)
            r = _rms(gv)
            vn = (gv * r * gain).astype(BF16)
            mixed = jnp.dot(ws_ref[h], vn, preferred_element_type=F32) + bb_ref[h]
            dmixed = da * _gelu(u)
            dzu_ref[:, sl] = (da * mixed * _gelu_grad(u)).astype(BF16)
            dmb = dmixed.astype(BF16)
            dws_ref[h] += lax.dot_general(dmb, vn, NT_DIMS, preferred_element_type=F32)
            dbb_ref[h] += jnp.broadcast_to(jnp.sum(dmixed, axis=-1, keepdims=True), (SG_CHUNK, HEAD_DIM))
            dvn = jnp.dot(wst_ref[h], dmb, preferred_element_type=F32)
            dgv, dg = _norm_bwd(dvn, gv, gain)
            dgn_ref[:, sl] += dg
            dzv_ref[:, sl] = (dgv * _gelu_grad(v)).astype(BF16)

    full = lambda shape: pl.BlockSpec(shape, lambda n: (0,) * len(shape))
    wspec = full((SG_HEADS, SG_CHUNK, SG_CHUNK))
    tile = lambda c: pl.BlockSpec((SG_CHUNK, SG_WIDTH), lambda n: (n, c))
    return _call(body, name="sg_bwd", grid=(T // SG_CHUNK,),
                 in_specs=[tile(0), tile(1), tile(0), wspec, wspec, full((SG_HEADS, SG_CHUNK, HEAD_DIM)),
                           full((1, SG_WIDTH))],
                 out_specs=[tile(0), tile(0), wspec, full((SG_HEADS, SG_CHUNK, HEAD_DIM)), full((1, SG_WIDTH))],
                 out_shape=[jax.ShapeDtypeStruct((T, SG_WIDTH), BF16), jax.ShapeDtypeStruct((T, SG_WIDTH), BF16),
                            jax.ShapeDtypeStruct((SG_HEADS, SG_CHUNK, SG_CHUNK), F32),
                            jax.ShapeDtypeStruct((SG_HEADS, SG_CHUNK, HEAD_DIM), F32),
                            jax.ShapeDtypeStruct((1, SG_WIDTH), F32)], args=(z, z, dmix, ws, wst, bb, gn))[0]


def _pool_specs(T, tp, col):
    step = tp // POOL_HALO
    last = T // POOL_HALO - 1
    return [pl.BlockSpec((POOL_HALO, POOL_WIDTH), lambda i: (jnp.maximum(i * step - 1, 0), col)),
            pl.BlockSpec((tp, POOL_WIDTH), lambda i: (i, col)),
            pl.BlockSpec((POOL_HALO, POOL_WIDTH), lambda i: (jnp.minimum((i + 1) * step, last), col))]


def _pool_band(i, tp, T, win):
    ext = tp + 2 * POOL_HALO
    t = i * tp + lax.broadcasted_iota(jnp.int32, (tp, ext), 0)
    s = i * tp - POOL_HALO + lax.broadcasted_iota(jnp.int32, (tp, ext), 1)
    band = (s >= jnp.maximum(t - win // 2, 0)) & (s < jnp.minimum(t + win // 2, T))
    t1 = i * tp + lax.broadcasted_iota(jnp.int32, (tp, 1), 0)
    cnt = (jnp.minimum(t1 + win // 2, T) - jnp.maximum(t1 - win // 2, 0)).astype(F32)
    return band.astype(BF16), cnt


def _pool_fwd(z, pw, psc):
    T = z.shape[0]
    tp = _pick(T, 256, POOL_HALO)

    def body(pp_ref, pc_ref, pn_ref, w_ref, sc_ref, o_ref):
        i = pl.program_id(0)
        halo = jnp.concatenate([pp_ref[...], pc_ref[...], pn_ref[...]], axis=0)
        for g, win in enumerate(POOL_WINDOWS):
            sl = slice(g * HEAD_DIM, (g + 1) * HEAD_DIM)
            band, cnt = _pool_band(i, tp, T, win)
            ssum = jnp.dot(band, halo[:, sl], preferred_element_type=F32)
            d = ssum / cnt - pc_ref[:, sl].astype(F32)
            y = jnp.dot(d.astype(BF16), w_ref[g], preferred_element_type=F32) * sc_ref[:, sl]
            o_ref[:, sl] = y.astype(BF16)

    full = lambda shape: pl.BlockSpec(shape, lambda i: (0,) * len(shape))
    return _call(body, name="pool_fwd", grid=(T // tp,),
                 in_specs=_pool_specs(T, tp, 2) + [full((4, HEAD_DIM, HEAD_DIM)), full((1, POOL_WIDTH))],
                 out_specs=[pl.BlockSpec((tp, POOL_WIDTH), lambda i: (i, 0))],
                 out_shape=[jax.ShapeDtypeStruct((T, POOL_WIDTH), BF16)], args=(z, z, z, pw, psc))[0][0]


def _pool_bwd(z, dmix, pw, psc):
    T = z.shape[0]
    tp = _pick(T, 256, POOL_HALO)
    ext = tp + 2 * POOL_HALO

    def body(pp_ref, pc_ref, pn_ref, dp_ref, dc_ref, dn_ref, w_ref, sc_ref, dz_ref, dw_ref, dsc_ref):
        i = pl.program_id(0)

        @pl.when(i == 0)
        def _():
            dw_ref[...] = jnp.zeros_like(dw_ref)
            dsc_ref[...] = jnp.zeros_like(dsc_ref)

        halo = jnp.concatenate([pp_ref[...], pc_ref[...], pn_ref[...]], axis=0)
        dy_halo = jnp.concatenate([dp_ref[...], dc_ref[...], dn_ref[...]], axis=0)
        th = i * tp - POOL_HALO + lax.broadcasted_iota(jnp.int32, (ext, 1), 0)
        inside = (th >= 0) & (th < T)
        s2 = i * tp + lax.broadcasted_iota(jnp.int32, (tp, ext), 0)
        t2 = i * tp - POOL_HALO + lax.broadcasted_iota(jnp.int32, (tp, ext), 1)
        for g, win in enumerate(POOL_WINDOWS):
            sl = slice(g * HEAD_DIM, (g + 1) * HEAD_DIM)
            sc = sc_ref[:, sl]
            band, cnt = _pool_band(i, tp, T, win)
            ssum = jnp.dot(band, halo[:, sl], preferred_element_type=F32)
            db = (ssum / cnt - pc_ref[:, sl].astype(F32)).astype(BF16)
            yraw = jnp.dot(db, w_ref[g], preferred_element_type=F32)
            dyc = dc_ref[:, sl].astype(F32)
            dsc_ref[:, sl] += jnp.sum(dyc * yraw, axis=0, keepdims=True)
            dw_ref[g] += lax.dot_general(db, (dyc * sc).astype(BF16), TN_DIMS, preferred_element_type=F32)
            dd = lax.dot_general((dy_halo[:, sl].astype(F32) * sc).astype(BF16), w_ref[g], NT_DIMS,
                                 preferred_element_type=F32)
            cnt_h = (jnp.minimum(th + win // 2, T) - jnp.maximum(th - win // 2, 0)).astype(F32)
            ddc = jnp.where(inside, dd / jnp.maximum(cnt_h, 1.0), 0.0)
            hi = ddc.astype(BF16)
            lo = (ddc - hi.astype(F32)).astype(BF16)
            band_t = ((s2 >= jnp.maximum(t2 - win // 2, 0)) & (s2 < jnp.minimum(t2 + win // 2, T))).astype(BF16)
            dpool = (jnp.dot(band_t, hi, preferred_element_type=F32) + jnp.dot(band_t, lo, preferred_element_type=F32)
                     - dd[POOL_HALO:POOL_HALO + tp])
            dz_ref[:, sl] = dpool.astype(BF16)

    full = lambda shape: pl.BlockSpec(shape, lambda i: (0,) * len(shape))
    return _call(body, name="pool_bwd", grid=(T // tp,),
                 in_specs=_pool_specs(T, tp, 2) + _pool_specs(T, tp, 1)
                 + [full((4, HEAD_DIM, HEAD_DIM)), full((1, POOL_WIDTH))],
                 out_specs=[pl.BlockSpec((tp, POOL_WIDTH), lambda i: (i, 0)), full((4, HEAD_DIM, HEAD_DIM)),
                            full((1, POOL_WIDTH))],
                 out_shape=[jax.ShapeDtypeStruct((T, POOL_WIDTH), BF16),
                            jax.ShapeDtypeStruct((4, HEAD_DIM, HEAD_DIM), F32),
                            jax.ShapeDtypeStruct((1, POOL_WIDTH), F32)], args=(z, z, z, dmix, dmix, dmix, pw, psc))[0]


ATT_ROWS = 8
WIN_KEYS = NA_KH * GRID_W


def _col_mask():
    q = lax.broadcasted_iota(jnp.int32, (GRID_W, WIN_KEYS), 0)
    k = lax.broadcasted_iota(jnp.int32, (GRID_W, WIN_KEYS), 1) & (GRID_W - 1)
    start = jnp.clip(q - NA_KW // 2, 0, GRID_W - NA_KW)
    return (k >= start) & (k < start + NA_KW)


def _softmax(s, bias, mask):
    s = jnp.where(mask, s * (HEAD_DIM ** -0.5) + bias, NEG)
    p = jnp.exp(s - jnp.max(s, axis=-1, keepdims=True))
    return p / jnp.sum(p, axis=-1, keepdims=True)


def _attn_window(step, a, rows):
    r = step * ATT_ROWS + a
    sr = jnp.clip(r - NA_KH // 2, 0, rows - NA_KH)
    return pl.ds(pl.multiple_of(sr * GRID_W, GRID_W), WIN_KEYS), sr - r + NA_KH - 1


def _attn_fwd(z, ecat, carry=None):
    T = z.shape[0]
    rows = T // GRID_W
    blk = ATT_ROWS * GRID_W

    def body(q_ref, k_ref, v_ref, e_ref, o_ref, s_scr, p_scr):
        step = pl.program_id(1)
        mask = _col_mask()
        wins = [_attn_window(step, a, rows) for a in range(ATT_ROWS)]
        qs = [slice(a * GRID_W, (a + 1) * GRID_W) for a in range(ATT_ROWS)]
        for a, (win, _) in enumerate(wins):
            s_scr[a] = lax.dot_general(q_ref[qs[a], :], k_ref[win, :], NT_DIMS, preferred_element_type=F32)
        for a, (_, dr0) in enumerate(wins):
            p_scr[a] = _softmax(s_scr[a], e_ref[0, dr0], mask).astype(BF16)
        for a, (win, _) in enumerate(wins):
            o_ref[qs[a], :] = jnp.dot(p_scr[a], v_ref[win, :], preferred_element_type=F32).astype(BF16)

    col = lambda off: pl.BlockSpec((T, HEAD_DIM), lambda h, s: (0, off // HEAD_DIM + h))
    return _call(body, name="attn_fwd", grid=(NA_HEADS, rows // ATT_ROWS),
                 in_specs=[pl.BlockSpec((blk, HEAD_DIM), lambda h, s: (s, Q_OFF // HEAD_DIM + h)), col(K_OFF),
                           col(V_OFF), pl.BlockSpec((1, NA_KH, GRID_W, WIN_KEYS), lambda h, s: (h, 0, 0, 0))],
                 out_specs=[pl.BlockSpec((blk, HEAD_DIM), lambda h, s: (s, h))],
                 out_shape=[jax.ShapeDtypeStruct((T, NA_WIDTH), BF16)],
                 scratch_shapes=[pltpu.VMEM((ATT_ROWS, GRID_W, WIN_KEYS), F32),
                                 pltpu.VMEM((ATT_ROWS, GRID_W, WIN_KEYS), BF16)],
                 args=(z, z, z, ecat), carry=carry)


def _attn_bwd(z, dmix, ecat, carry=None):
    T = z.shape[0]
    rows = T // GRID_W
    blk = ATT_ROWS * GRID_W
    nstep = rows // ATT_ROWS

    def body(q_ref, k_ref, v_ref, do_ref, e_ref, dq_ref, dk_ref, dv_ref, de_ref, dk_acc, dv_acc, s_scr, dp_scr,
             p_scr, ds_scr):
        step = pl.program_id(1)
        mask = _col_mask()

        @pl.when(step == 0)
        def _():
            dk_acc[...] = jnp.zeros_like(dk_acc)
            dv_acc[...] = jnp.zeros_like(dv_acc)
            de_ref[...] = jnp.zeros_like(de_ref)

        wins = [_attn_window(step, a, rows) for a in range(ATT_ROWS)]
        qs = [slice(a * GRID_W, (a + 1) * GRID_W) for a in range(ATT_ROWS)]
        for a, (win, _) in enumerate(wins):
            s_scr[a] = lax.dot_general(q_ref[qs[a], :], k_ref[win, :], NT_DIMS, preferred_element_type=F32)
            dp_scr[a] = lax.dot_general(do_ref[qs[a], :], v_ref[win, :], NT_DIMS, preferred_element_type=F32)
        for a, (_, dr0) in enumerate(wins):
            pr = _softmax(s_scr[a], e_ref[0, dr0], mask)
            dp = dp_scr[a]
            ds = pr * (dp - jnp.sum(dp * pr, axis=-1, keepdims=True))
            de_ref[0, dr0] += ds
            p_scr[a] = pr.astype(BF16)
            ds_scr[a] = (ds * (HEAD_DIM ** -0.5)).astype(BF16)
        for a, (win, _) in enumerate(wins):
            dq_ref[qs[a], :] = jnp.dot(ds_scr[a], k_ref[win, :], preferred_element_type=F32).astype(BF16)
            dv_acc[win, :] += lax.dot_general(p_scr[a], do_ref[qs[a], :], TN_DIMS, preferred_element_type=F32)
            dk_acc[win, :] += lax.dot_general(ds_scr[a], q_ref[qs[a], :], TN_DIMS, preferred_element_type=F32)

        @pl.when(step == nstep - 1)
        def _():
            dk_ref[...] = dk_acc[...].astype(BF16)
            dv_ref[...] = dv_acc[...].astype(BF16)

    col = lambda off: pl.BlockSpec((T, HEAD_DIM), lambda h, s: (0, off // HEAD_DIM + h))
    e_spec = pl.BlockSpec((1, NA_KH, GRID_W, WIN_KEYS), lambda h, s: (h, 0, 0, 0))
    out = jax.ShapeDtypeStruct((T, NA_WIDTH), BF16)
    stage = lambda dtype: pltpu.VMEM((ATT_ROWS, GRID_W, WIN_KEYS), dtype)
    return _call(body, name="attn_bwd", grid=(NA_HEADS, nstep),
                 in_specs=[pl.BlockSpec((blk, HEAD_DIM), lambda h, s: (s, Q_OFF // HEAD_DIM + h)), col(K_OFF),
                           col(V_OFF),
                           pl.BlockSpec((blk, HEAD_DIM), lambda h, s: (s, (SG_WIDTH + POOL_WIDTH) // HEAD_DIM + h)),
                           e_spec],
                 out_specs=[pl.BlockSpec((blk, HEAD_DIM), lambda h, s: (s, h)), col(0), col(0), e_spec],
                 out_shape=[out, out, out, jax.ShapeDtypeStruct((NA_HEADS, NA_KH, GRID_W, WIN_KEYS), F32)],
                 scratch_shapes=[pltpu.VMEM((T, HEAD_DIM), F32), pltpu.VMEM((T, HEAD_DIM), F32),
                                 stage(F32), stage(F32), stage(BF16), stage(BF16)],
                 args=(z, z, z, dmix, ecat), carry=carry)


def _rpb_tables():
    col = jnp.arange(GRID_W)
    dc = jnp.clip(col[None, :] - col[:, None] + NA_KW - 1, 0, 2 * NA_KW - 2)
    by_col = (dc[None] == jnp.arange(2 * NA_KW - 1)[:, None, None]).astype(F32)
    d, j = jnp.arange(NA_KH)[:, None], jnp.arange(NA_KH)[None, :]
    by_row = (jnp.arange(2 * NA_KH - 1)[:, None, None] == (d + j)[None]).astype(F32)
    return by_col, by_row


def _rpb_expand(rpb):
    by_col, by_row = _rpb_tables()
    e = jnp.einsum("hrc,cqk->hrqk", rpb, by_col, precision=lax.Precision.HIGHEST)
    ecat = jnp.einsum("hrqk,rdj->hdqjk", e, by_row, precision=lax.Precision.HIGHEST)
    return ecat.reshape(NA_HEADS, NA_KH, GRID_W, WIN_KEYS)


def _rpb_collect(decat):
    by_col, by_row = _rpb_tables()
    de = jnp.einsum("hdqjk,rdj->hrqk", decat.reshape(NA_HEADS, NA_KH, GRID_W, NA_KH, GRID_W), by_row,
                    precision=lax.Precision.HIGHEST)
    return jnp.einsum("hrqk,cqk->hrc", de, by_col, precision=lax.Precision.HIGHEST)


def _adamw(w, g, m, v):
    shape = w.shape
    C = shape[-1]
    R = w.size // C
    tr = _pick(R, max(8, (1 << 18) // C), 8)
    args = [a.reshape(R, C) for a in (w, g, m, v)]

    def body(w_ref, g_ref, m_ref, v_ref, d_ref, mo_ref, vo_ref):
        gv = g_ref[...]
        mn = ADAM_B1 * m_ref[...] + (1.0 - ADAM_B1) * gv
        vn = ADAM_B2 * v_ref[...] + (1.0 - ADAM_B2) * (gv * gv)
        m_hat = mn / (1.0 - ADAM_B1 ** ADAM_STEP)
        v_hat = vn / (1.0 - ADAM_B2 ** ADAM_STEP)
        d_ref[...] = -ADAM_LR * (m_hat / (jnp.sqrt(v_hat) + ADAM_EPS) + ADAM_WD * w_ref[...])
        mo_ref[...] = mn
        vo_ref[...] = vn

    spec = pl.BlockSpec((tr, C), lambda i: (i, 0))
    out = jax.ShapeDtypeStruct((R, C), F32)
    res = _call(body, name="adamw", grid=(R // tr,), in_specs=[spec] * 4, out_specs=[spec] * 3, out_shape=[out] * 3,
                args=args)[0]
    return [r.reshape(shape) for r in res]


def _sum_devices(g):
    G, _, R, C = g.shape
    tr = _pick(R, max(16, (1 << 18) // C), 16)

    def body(g_ref, o_ref):
        acc = g_ref[0, 0].astype(F32)
        for k in range(1, N_DEV):
            acc = acc + g_ref[0, k].astype(F32)
        o_ref[0] = acc

    return _call(body, name="sum_devices", grid=(G, R // tr),
                 in_specs=[pl.BlockSpec((1, N_DEV, tr, C), lambda t, i: (t, 0, i, 0))],
                 out_specs=[pl.BlockSpec((1, tr, C), lambda t, i: (t, i, 0))],
                 out_shape=[jax.ShapeDtypeStruct((G, R, C), F32)], args=(g,))[0][0]


SMALL = ("ffn1_norm", "mix_norm", "sg_norm", "sg_w", "sg_b", "pool_w", "pool_scale", "na_rpb", "ffn2_norm")
GROUPS = (("gu1", ("ffn1_w_gate", "ffn1_w_up"), True), ("down1", ("ffn1_w_down",), False), ("w_in", ("w_in",), True),
          ("w_out", ("w_out",), False), ("gu2", ("ffn2_w_gate", "ffn2_w_up"), True), ("down2", ("ffn2_w_down",), False))


def kernel(x, ffn1_norm, ffn1_w_gate, ffn1_w_up, ffn1_w_down, mix_norm, w_in, sg_norm, sg_w, sg_b, pool_w, pool_scale, na_rpb, w_out, ffn2_norm, ffn2_w_gate, ffn2_w_up, ffn2_w_down, final_norm, loss_target, m_ffn1_norm, m_ffn1_w_gate, m_ffn1_w_up, m_ffn1_w_down, m_mix_norm, m_w_in, m_sg_norm, m_sg_w, m_sg_b, m_pool_w, m_pool_scale, m_na_rpb, m_w_out, m_ffn2_norm, m_ffn2_w_gate, m_ffn2_w_up, m_ffn2_w_down, m_final_norm, v_ffn1_norm, v_ffn1_w_gate, v_ffn1_w_up, v_ffn1_w_down, v_mix_norm, v_w_in, v_sg_norm, v_sg_w, v_sg_b, v_pool_w, v_pool_scale, v_na_rpb, v_w_out, v_ffn2_norm, v_ffn2_w_gate, v_ffn2_w_up, v_ffn2_w_down, v_final_norm):
    given = dict(locals())
    T, D = x.shape[1], x.shape[2]
    L = ffn1_norm.shape[0]
    assert x.shape[0] == 1 and D == SG_WIDTH + POOL_WIDTH + NA_WIDTH and w_in.shape[2] * N_DEV == Z_COLS
    assert T % (ATT_ROWS * GRID_W) == 0 and T // GRID_W >= NA_KH
    x0 = x.reshape(T, D)
    tgt = loss_target.reshape(T, D)
    members = {grp: (names, cols) for grp, names, cols in GROUPS}

    def shard(grp, l):
        names, cols = members[grp]
        return jnp.stack([(given[n][l].T if cols else given[n][l]).astype(BF16) for n in names])

    def gather(l, *grps):
        return _Carry(gathers=[shard(grp, l) for grp in grps]) if l < L else None

    def full(gathered):
        return gathered.reshape(gathered.shape[0], -1, D)

    W = {"gu1": full(_exchange(gather(0, "gu1"), "gather_first")[0])}
    saved = []
    xc = x0
    h = _rmsnorm(xc, ffn1_norm[0:1])
    for l in range(L):
        s = dict(x0=xc, h1=h)
        (s["a1"], s["pq1"]), got = _ffn_gu(h, W["gu1"], gather(l, "down1", "w_in"))
        W["down1"], W["w_in"] = map(full, got)
        (xc, h), got = _mm_res_norm(s["a1"], W["down1"][0], xc, mix_norm[l:l + 1], 0.5, gather(l, "w_out"))
        W["w_out"] = full(got[0])
        s["x1"], s["h2"] = xc, h
        (z,), got = _mm_nt(h, W["w_in"][0], 1.0, carry=gather(l, "down2"))
        W["down2"] = full(got[0])
        s["ws"] = sg_w[l].astype(BF16)
        s["wst"] = jnp.swapaxes(sg_w[l], 1, 2).astype(BF16)
        s["bb"] = jnp.broadcast_to(sg_b[l][:, :, None], (SG_HEADS, SG_CHUNK, HEAD_DIM))
        s["gn"] = sg_norm[l:l + 1]
        s["pw"] = pool_w[l].astype(BF16)
        s["psc"] = pool_scale[l:l + 1]
        s["ecat"] = _rpb_expand(na_rpb[l])
        (att,), got = _attn_fwd(z, s["ecat"], gather(l, "gu2"))
        W["gu2"] = full(got[0])
        mix = jnp.concatenate([_sg_fwd(z, s["ws"], s["bb"], s["gn"]), _pool_fwd(z, s["pw"], s["psc"]), att], axis=1)
        s["z"], s["mix"] = z, mix
        (xc, h), _ = _mm_res_norm(mix, W["w_out"][0], xc, ffn2_norm[l:l + 1], 1.0)
        s["x2"], s["h3"] = xc, h
        (s["a2"], s["pq2"]), got = _ffn_gu(h, W["gu2"], gather(l + 1, "gu1"))
        s["W"] = W
        W = {"gu1": full(got[0])} if got else {}
        gnext = ffn1_norm[l + 1:l + 2] if l + 1 < L else final_norm.reshape(1, D)
        (xc, h), _ = _mm_res_norm(s["a2"], s["W"]["down2"][0], xc, gnext, 0.5)
        saved.append(s)

    loss_row, dx, dxb, dg_final = _loss_bwd(xc, final_norm.reshape(1, D), tgt)
    loss = lax.psum(loss_row[0, 0], MESH_AXES)

    received = {grp: [[None] * len(names) for _ in range(L)] for grp, names, _ in GROUPS}
    small = {n: [None] * L for n in SMALL}
    pending = None

    def slots(g, *groups):
        return g.reshape(g.shape[0], N_DEV, -1, D), groups or tuple(range(g.shape[0]))

    for l in reversed(range(L)):
        s = saved[l]
        W = s["W"]
        (dgu,), _ = _mm_nt(dxb, W["down2"][0], 0.5, pq=s["pq2"])
        (g_down2,), got = _mm_tn(s["a2"][None], dxb, 0.5,
                                 _Carry(scatters=[slots(pending[1], 1)]) if pending else None)
        if pending:
            received["gu1"][pending[0]][1] = got[0]
        (dx, dxb, dgn), got = _dh_norm_bwd(dgu, W["gu2"], s["x2"], ffn2_norm[l:l + 1], dx,
                                           _Carry(scatters=[slots(g_down2)]))
        received["down2"][l][0] = got[0]
        small["ffn2_norm"][l] = dgn
        (g_gu2,), _ = _mm_tn(dgu, s["h3"], 1.0)

        (dmix,), _ = _mm_nt(dxb, W["w_out"][0], 1.0)
        (g_out,), _ = _mm_tn(s["mix"][None], dxb, 1.0)
        dzu, dzv, dws, dbb, dgn = _sg_bwd(s["z"], dmix, s["ws"], s["wst"], s["bb"], s["gn"])
        dzp, dpw, dpsc = _pool_bwd(s["z"], dmix, s["pw"], s["psc"])
        (dq, dk, dv, decat), got = _attn_bwd(s["z"], dmix, s["ecat"], _Carry(scatters=[slots(g_gu2, 0)]))
        received["gu2"][l][0] = got[0]
        small["sg_w"][l], small["sg_b"][l], small["sg_norm"][l] = dws, dbb[:, :, 0], dgn[0]
        small["pool_w"][l], small["pool_scale"][l] = dpw, dpsc[0]
        small["na_rpb"][l] = _rpb_collect(decat)
        dz = jnp.concatenate([dzu, dzv, dzp, dq, dk, dv], axis=1)
        (dx, dxb, dgn), got = _dh_norm_bwd(dz[None], W["w_in"], s["x1"], mix_norm[l:l + 1], dx,
                                           _Carry(scatters=[slots(g_gu2, 1)]))
        received["gu2"][l][1] = got[0]
        small["mix_norm"][l] = dgn
        (g_in,), got = _mm_tn(dz[None], s["h2"], 1.0, _Carry(scatters=[slots(g_out)]))
        received["w_out"][l][0] = got[0]

        (dgu,), _ = _mm_nt(dxb, W["down1"][0], 0.5, pq=s["pq1"])
        (g_down1,), got = _mm_tn(s["a1"][None], dxb, 0.5, _Carry(scatters=[slots(g_in)]))
        received["w_in"][l][0] = got[0]
        (g_gu1,), got = _mm_tn(dgu, s["h1"], 1.0, _Carry(scatters=[slots(g_down1)]))
        received["down1"][l][0] = got[0]
        (dx, dxb, dgn), got = _dh_norm_bwd(dgu, W["gu1"], s["x0"], ffn1_norm[l:l + 1], dx,
                                           _Carry(scatters=[slots(g_gu1, 0)]))
        received["gu1"][l][0] = got[0]
        small["ffn1_norm"][l] = dgn
        pending = (l, g_gu1)

    small_shapes = {n: given[n].shape for n in SMALL}
    small_shapes["final_norm"] = final_norm.shape
    flat = [jnp.stack([jnp.reshape(g, (-1,)) for g in small[n]]).reshape(-1) for n in SMALL] + [dg_final.reshape(-1)]
    sizes = [f.shape[0] for f in flat]
    total = sum(sizes)
    padded = -(-total // 2048) * 2048
    local = jnp.concatenate(flat + [jnp.zeros((padded - total,), F32)]).reshape(1, -1, 128)
    got = _exchange(_Carry(gathers=[local], scatters=[slots(pending[1], 1)]), "exchange_last")
    received["gu1"][pending[0]][1] = got[1]
    summed = _sum_devices(got[0]).reshape(-1)
    grads, off = {}, 0
    for n, size in zip(list(SMALL) + ["final_norm"], sizes):
        grads[n] = summed[off:off + size].reshape(small_shapes[n])
        off += size
    for grp, names, cols in GROUPS:
        for t, n in enumerate(names):
            per_layer = [_sum_devices(received[grp][l][t])[0] for l in range(L)]
            grads[n] = jnp.stack([g.T if cols else g for g in per_layer])

    names = ['ffn1_norm', 'ffn1_w_gate', 'ffn1_w_up', 'ffn1_w_down', 'mix_norm', 'w_in', 'sg_norm', 'sg_w', 'sg_b',
             'pool_w', 'pool_scale', 'na_rpb', 'w_out', 'ffn2_norm', 'ffn2_w_gate', 'ffn2_w_up', 'ffn2_w_down',
             'final_norm']
    delta, new_m, new_v = {}, {}, {}
    for n in names:
        delta[n], new_m[n], new_v[n] = _adamw(given[n], grads[n], given["m_" + n], given["v_" + n])
    return (loss, dx.reshape(1, T, D), *[grads[n] for n in names], *[delta[n] for n in names],
            *[new_m[n] for n in names], *[new_v[n] for n in names])
```

```python
import functools
import math

import jax
import jax.numpy as jnp
from jax import lax
from jax.experimental import pallas as pl
from jax.experimental.pallas import tpu as pltpu

F32 = jnp.float32
BF16 = jnp.bfloat16
EPS = 1e-6
NEG = -1e30

HEAD_DIM = 128
SG_WIDTH = 512
SG_HEADS = 4
SG_CHUNK = 128
POOL_WINDOWS = (2, 4, 8, 16)
POOL_WIDTH = 512
POOL_HALO = 128
NA_WIDTH = 1024
NA_HEADS = 8
NA_KH = 8
NA_KW = 16
GRID_W = 64
Z_COLS = 2 * SG_WIDTH + POOL_WIDTH + 3 * NA_WIDTH
Q_OFF = 2 * SG_WIDTH + POOL_WIDTH
K_OFF = Q_OFF + NA_WIDTH
V_OFF = K_OFF + NA_WIDTH

ADAM_LR = 0.001
ADAM_B1 = 0.9
ADAM_B2 = 0.999
ADAM_EPS = 1e-08
ADAM_WD = 0.01
ADAM_STEP = 10

N_DEV = 8
MESH_AXES = ("x", "y", "c")
MESH = pl.DeviceIdType.MESH
ANY = pl.BlockSpec(memory_space=pl.ANY)

NT_DIMS = (((1,), (1,)), ((), ()))
TN_DIMS = (((0,), (0,)), ((), ()))

ROWS_PREF = 1024
MXU_DEPTH = 256
K_PREF = 1024


def _pick(n, pref, mult):
    best = None
    t = mult
    while t <= min(n, pref):
        if n % t == 0:
            best = t
        t += mult
    return n if best is None else best


def _gelu(x):
    return 0.5 * x * (1.0 + lax.erf(x * (1.0 / math.sqrt(2.0))))


def _gelu_grad(x):
    cdf = 0.5 * (1.0 + lax.erf(x * (1.0 / math.sqrt(2.0))))
    pdf = jnp.exp(-0.5 * x * x) * (1.0 / math.sqrt(2.0 * math.pi))
    return cdf + x * pdf


def _rms(x):
    return lax.rsqrt(jnp.mean(x * x, axis=-1, keepdims=True) + EPS)


def _norm_bwd(dh, x, g):
    r = _rms(x)
    w = dh * g
    dx = r * w - x * (r * r * r) * jnp.mean(w * x, axis=-1, keepdims=True)
    dg = jnp.sum(dh * (x * r), axis=0, keepdims=True)
    return dx, dg


def _position():
    return lax.axis_index("x"), lax.axis_index("y"), lax.axis_index("c")


def _index(p):
    return 4 * p[0] + 2 * p[1] + p[2]


class _Carry:
    def __init__(self, gathers=(), scatters=()):
        self.gathers, self.scatters = list(gathers), list(scatters)
        self.units = sum(g.shape[0] for g in self.gathers) + sum(len(groups) for _, groups in self.scatters)

    def arrays(self):
        return self.gathers + [s for s, _ in self.scatters]

    def out_shapes(self):
        return ([jax.ShapeDtypeStruct((g.shape[0], N_DEV) + g.shape[1:], g.dtype) for g in self.gathers]
                + [jax.ShapeDtypeStruct((len(groups),) + s.shape[1:], s.dtype) for s, groups in self.scatters])

    def scratch(self):
        return [pltpu.SemaphoreType.DMA((7 * self.units,)), pltpu.SemaphoreType.DMA((7 * self.units,)),
                pltpu.SemaphoreType.DMA((self.units,))]

    def _gather_copies(self, n, x_ref, out_ref, send, recv, local):
        x, y, c = _position()
        me, sibling = (x, y, c), (x, y, 1 - c)
        chips = [(1 - x, y), (x, 1 - y), (1 - x, 1 - y)]

        def copy(k, block, to, src=None):
            dst = out_ref.at[_index(block)]
            return pltpu.make_async_remote_copy(
                src_ref=dst if src is None else src, dst_ref=dst, send_sem=send.at[7 * n + k],
                recv_sem=recv.at[7 * n + k], device_id=to, device_id_type=MESH)

        return dict(
            mine=pltpu.make_async_copy(x_ref, out_ref.at[_index(me)], local.at[n]),
            first=[copy(0, me, sibling, x_ref)] + [copy(1 + j, me, (*ch, c), x_ref) for j, ch in enumerate(chips)],
            landed=[copy(1 + j, (*ch, c), me) for j, ch in enumerate(chips)],
            passed=[copy(4 + j, (*ch, c), sibling) for j, ch in enumerate(chips)],
            from_sibling=[copy(0, sibling, me)] + [copy(4 + j, (*ch, 1 - c), me) for j, ch in enumerate(chips)])

    def _scatter_copies(self, n, src_ref, out_ref, send, recv, local):
        x, y, c = _position()
        me = (x, y, c)
        sends, recvs = [], []
        for k in range(1, N_DEV):
            flip = lambda v, bit: 1 - v if bit else v
            peer = (flip(x, k & 4), flip(y, k & 2), flip(c, k & 1))
            sems = dict(send_sem=send.at[7 * n + k - 1], recv_sem=recv.at[7 * n + k - 1], device_id=peer,
                        device_id_type=MESH)
            sends.append(pltpu.make_async_remote_copy(src_ref=src_ref.at[_index(peer)], dst_ref=out_ref.at[_index(me)],
                                                      **sems))
            recvs.append(pltpu.make_async_remote_copy(src_ref=src_ref.at[_index(me)], dst_ref=out_ref.at[_index(peer)],
                                                      **sems))
        mine = pltpu.make_async_copy(src_ref.at[_index(me)], out_ref.at[_index(me)], local.at[n])
        return dict(mine=mine, sends=sends, recvs=recvs)

    def _pieces(self, ins, outs, sems):
        send, recv, local = sems
        gs, ss, unit = [], [], 0
        for n, g in enumerate(self.gathers):
            for t in range(g.shape[0]):
                gs.append(self._gather_copies(unit, ins[n].at[t], outs[n].at[t], send, recv, local))
                unit += 1
        for n, (_, groups) in enumerate(self.scatters, start=len(self.gathers)):
            for j, t in enumerate(groups):
                ss.append(self._scatter_copies(unit, ins[n].at[t], outs[n].at[j], send, recv, local))
                unit += 1
        return gs, ss

    def start(self, ins, outs, sems):
        gs, ss = self._pieces(ins, outs, sems)
        for g in gs:
            g["mine"].start()
            for cp in g["first"]:
                cp.start()
        for s in ss:
            s["mine"].start()
            for cp in s["sends"]:
                cp.start()

    def forward(self, ins, outs, sems):
        gs, _ = self._pieces(ins, outs, sems)
        for g in gs:
            for landed, passed in zip(g["landed"], g["passed"]):
                landed.wait_recv()
                passed.start()

    def finish(self, ins, outs, sems):
        gs, ss = self._pieces(ins, outs, sems)
        for g in gs:
            for cp in g["from_sibling"]:
                cp.wait_recv()
            for cp in g["first"] + g["passed"]:
                cp.wait_send()
            g["mine"].wait()
        for s in ss:
            for cp in s["recvs"]:
                cp.wait_recv()
            for cp in s["sends"]:
                cp.wait_send()
            s["mine"].wait()


def _call(body, *, name, grid, in_specs, out_specs, out_shape, args, scratch_shapes=(), carry=None):
    if carry is None or not carry.arrays():
        outs = pl.pallas_call(
            body, name=name, grid=grid, in_specs=in_specs, out_specs=out_specs, out_shape=out_shape,
            scratch_shapes=list(scratch_shapes),
            compiler_params=pltpu.CompilerParams(dimension_semantics=("arbitrary",) * len(grid)))(*args)
        return list(outs), []
    n_in, n_out, n_scr, n_car = len(in_specs), len(out_specs), len(scratch_shapes), len(carry.arrays())
    steps = math.prod(grid)
    middle = (steps * 6) // 10

    def wrapped(*refs):
        ins, refs = refs[:n_in], refs[n_in:]
        cins, refs = refs[:n_car], refs[n_car:]
        outs, refs = refs[:n_out], refs[n_out:]
        couts, refs = refs[:n_car], refs[n_car:]
        scr, sems = refs[:n_scr], refs[n_scr:]
        step = 0
        for d, size in enumerate(grid):
            step = step * size + pl.program_id(d)

        @pl.when(step == 0)
        def _():
            carry.start(cins, couts, sems)

        body(*ins, *outs, *scr)

        if carry.gathers:
            @pl.when(step == middle)
            def _():
                carry.forward(cins, couts, sems)

        @pl.when(step == steps - 1)
        def _():
            carry.finish(cins, couts, sems)

    outs = pl.pallas_call(
        wrapped, name=name + "_carry", grid=grid, in_specs=list(in_specs) + [ANY] * n_car,
        out_specs=list(out_specs) + [ANY] * n_car, out_shape=list(out_shape) + carry.out_shapes(),
        scratch_shapes=list(scratch_shapes) + carry.scratch(),
        compiler_params=pltpu.CompilerParams(dimension_semantics=("arbitrary",) * len(grid)))(*args, *carry.arrays())
    return list(outs[:n_out]), list(outs[n_out:])


def _exchange(carry, name):
    n_car = len(carry.arrays())

    def body(*refs):
        cins, couts, sems = refs[:n_car], refs[n_car:2 * n_car], refs[2 * n_car:]
        carry.start(cins, couts, sems)
        if carry.gathers:
            carry.forward(cins, couts, sems)
        carry.finish(cins, couts, sems)

    return list(pl.pallas_call(body, name=name, in_specs=[ANY] * n_car, out_specs=[ANY] * n_car,
                               out_shape=carry.out_shapes(), scratch_shapes=carry.scratch())(*carry.arrays()))


def _rmsnorm(x, g):
    T, D = x.shape
    tm = _pick(T, 512, 16)

    def body(x_ref, g_ref, o_ref):
        xv = x_ref[...]
        o_ref[...] = (xv * _rms(xv) * g_ref[...]).astype(BF16)

    return _call(body, name="rmsnorm", grid=(T // tm,),
                 in_specs=[pl.BlockSpec((tm, D), lambda i: (i, 0)), pl.BlockSpec((1, D), lambda i: (0, 0))],
                 out_specs=[pl.BlockSpec((tm, D), lambda i: (i, 0))],
                 out_shape=[jax.ShapeDtypeStruct((T, D), BF16)], args=(x, g))[0][0]


def _ffn_gu(h, wgu, carry=None):
    T, D = h.shape
    F = wgu.shape[1]
    tm = _pick(T, ROWS_PREF, 16)
    tn = _pick(F, 512, 128)

    def body(h_ref, wg_ref, wu_ref, a_ref, pq_ref):
        hv = h_ref[...]
        g = lax.dot_general(hv, wg_ref[0], NT_DIMS, preferred_element_type=F32)
        u = lax.dot_general(hv, wu_ref[0], NT_DIMS, preferred_element_type=F32)
        sg = jax.nn.sigmoid(g)
        q = g * sg
        a_ref[...] = (q * u).astype(BF16)
        pq_ref[0] = (u * (sg * (1.0 + g * (1.0 - sg)))).astype(BF16)
        pq_ref[1] = q.astype(BF16)

    return _call(body, name="ffn_gu", grid=(T // tm, F // tn),
                 in_specs=[pl.BlockSpec((tm, D), lambda i, j: (i, 0)),
                           pl.BlockSpec((1, tn, D), lambda i, j: (0, j, 0)),
                           pl.BlockSpec((1, tn, D), lambda i, j: (1, j, 0))],
                 out_specs=[pl.BlockSpec((tm, tn), lambda i, j: (i, j)),
                            pl.BlockSpec((2, tm, tn), lambda i, j: (0, i, j))],
                 out_shape=[jax.ShapeDtypeStruct((T, F), BF16), jax.ShapeDtypeStruct((2, T, F), BF16)],
                 args=(h, wgu, wgu), carry=carry)


def _mm_nt(a, w, scale, pq=None, carry=None):
    T, K = a.shape
    N = w.shape[0]
    tm = _pick(T, ROWS_PREF, 16)
    tn = _pick(N, 512, 128)

    def body(*refs):
        a_ref, w_ref = refs[:2]
        d = lax.dot_general(a_ref[...], w_ref[...], NT_DIMS, preferred_element_type=F32)
        if scale != 1.0:
            d = d * scale
        if pq is None:
            refs[2][...] = d.astype(BF16)
        else:
            pq_ref, o_ref = refs[2:]
            o_ref[0] = (d * pq_ref[0].astype(F32)).astype(BF16)
            o_ref[1] = (d * pq_ref[1].astype(F32)).astype(BF16)

    in_specs = [pl.BlockSpec((tm, K), lambda i, j: (i, 0)), pl.BlockSpec((tn, K), lambda i, j: (j, 0))]
    if pq is None:
        args, out_spec, out_shape = (a, w), pl.BlockSpec((tm, tn), lambda i, j: (i, j)), (T, N)
    else:
        in_specs.append(pl.BlockSpec((2, tm, tn), lambda i, j: (0, i, j)))
        args, out_spec, out_shape = (a, w, pq), pl.BlockSpec((2, tm, tn), lambda i, j: (0, i, j)), (2, T, N)
    return _call(body, name="mm_nt" if pq is None else "ffn_da", grid=(T // tm, N // tn),
                 in_specs=in_specs, out_specs=[out_spec], out_shape=[jax.ShapeDtypeStruct(out_shape, BF16)],
                 args=args, carry=carry)


def _z_proj(h, wt, carry=None):
    T, K = h.shape
    tm = _pick(T, ROWS_PREF, 16)
    tn = 4 * HEAD_DIM
    flat = Q_OFF // tn

    def body(h_ref, w_ref, z_ref, qkv_ref):
        j = pl.program_id(1)
        zv = lax.dot_general(h_ref[...], w_ref[...], NT_DIMS, preferred_element_type=F32).astype(BF16)

        @pl.when(j < flat)
        def _():
            z_ref[...] = zv

        @pl.when(j >= flat)
        def _():
            for c in range(tn // HEAD_DIM):
                qkv_ref[c] = zv[:, c * HEAD_DIM:(c + 1) * HEAD_DIM]

    return _call(body, name="z_proj", grid=(T // tm, Z_COLS // tn),
                 in_specs=[pl.BlockSpec((tm, K), lambda i, j: (i, 0)), pl.BlockSpec((tn, K), lambda i, j: (j, 0))],
                 out_specs=[pl.BlockSpec((tm, tn), lambda i, j: (i, jnp.minimum(j, flat - 1))),
                            pl.BlockSpec((tn // HEAD_DIM, tm, HEAD_DIM), lambda i, j: (jnp.maximum(j - flat, 0), i, 0))],
                 out_shape=[jax.ShapeDtypeStruct((T, Q_OFF), BF16),
                            jax.ShapeDtypeStruct((3 * NA_HEADS, T, HEAD_DIM), BF16)],
                 args=(h, wt), carry=carry)


def _mm_res_norm(a, w, x, gnext, scale, carry=None):
    T, K = a.shape
    D = w.shape[1]
    tm = _pick(T, 512, 16)
    tk = _pick(K, K_PREF, MXU_DEPTH)
    nk = K // tk

    def body(a_ref, w_ref, x_ref, g_ref, xo_ref, ho_ref):
        k = pl.program_id(1)
        p = jnp.dot(a_ref[...], w_ref[...], preferred_element_type=F32)

        @pl.when(k == 0)
        def _():
            xo_ref[...] = p

        @pl.when(k > 0)
        def _():
            xo_ref[...] += p

        @pl.when(k == nk - 1)
        def _():
            xn = x_ref[...] + scale * xo_ref[...]
            xo_ref[...] = xn
            ho_ref[...] = (xn * _rms(xn) * g_ref[...]).astype(BF16)

    row = pl.BlockSpec((tm, D), lambda i, k: (i, 0))
    return _call(body, name="mm_res_norm", grid=(T // tm, nk),
                 in_specs=[pl.BlockSpec((tm, tk), lambda i, k: (i, k)), pl.BlockSpec((tk, D), lambda i, k: (k, 0)),
                           row, pl.BlockSpec((1, D), lambda i, k: (0, 0))],
                 out_specs=[row, row],
                 out_shape=[jax.ShapeDtypeStruct((T, D), F32), jax.ShapeDtypeStruct((T, D), BF16)],
                 args=(a, w, x, gnext), carry=carry)


def _mm_tn(a, b, scale, carry=None):
    G, T, M = a.shape
    N = b.shape[1]
    tm = _pick(M, 512, 128)
    tk = _pick(T, 2048, 16)
    nk = T // tk

    def body(a_ref, b_ref, o_ref, acc):
        k = pl.program_id(2)
        p = lax.dot_general(a_ref[0], b_ref[...], TN_DIMS, preferred_element_type=F32)

        @pl.when(k == 0)
        def _():
            acc[...] = p

        @pl.when(k > 0)
        def _():
            acc[...] += p

        @pl.when(k == nk - 1)
        def _():
            o_ref[0] = (acc[...] * scale).astype(BF16)

    return _call(body, name="mm_tn", grid=(G, M // tm, nk),
                 in_specs=[pl.BlockSpec((1, tk, tm), lambda g, i, k: (g, k, i)),
                           pl.BlockSpec((tk, N), lambda g, i, k: (k, 0))],
                 out_specs=[pl.BlockSpec((1, tm, N), lambda g, i, k: (g, i, 0))],
                 out_shape=[jax.ShapeDtypeStruct((G, M, N), BF16)],
                 scratch_shapes=[pltpu.VMEM((tm, N), F32)], args=(a, b), carry=carry)


def _dh_norm_bwd(d, wt, x, g, dres, carry=None):
    T, D = x.shape
    G, _, K = d.shape
    tm = _pick(T, 512, 16)
    tk = _pick(K, K_PREF, MXU_DEPTH)
    nk = K // tk

    def body(d_ref, w_ref, x_ref, g_ref, dres_ref, dx_ref, dxb_ref, dg_ref):
        i = pl.program_id(0)
        k = pl.program_id(1)
        p = jnp.dot(d_ref[0], w_ref[0], preferred_element_type=F32)

        @pl.when(k == 0)
        def _():
            dx_ref[...] = p

        @pl.when(k > 0)
        def _():
            dx_ref[...] += p

        @pl.when(k == G * nk - 1)
        def _():
            dxn, dgp = _norm_bwd(dx_ref[...], x_ref[...], g_ref[...])
            dxv = dres_ref[...] + dxn
            dx_ref[...] = dxv
            dxb_ref[...] = dxv.astype(BF16)

            @pl.when(i == 0)
            def _():
                dg_ref[...] = dgp

            @pl.when(i > 0)
            def _():
                dg_ref[...] += dgp

    row = pl.BlockSpec((tm, D), lambda i, k: (i, 0))
    vec = pl.BlockSpec((1, D), lambda i, k: (0, 0))
    return _call(body, name="dh_norm_bwd", grid=(T // tm, G * nk),
                 in_specs=[pl.BlockSpec((1, tm, tk), lambda i, k: (k // nk, i, k % nk)),
                           pl.BlockSpec((1, tk, D), lambda i, k: (k // nk, k % nk, 0)), row, vec, row],
                 out_specs=[row, row, vec],
                 out_shape=[jax.ShapeDtypeStruct((T, D), F32), jax.ShapeDtypeStruct((T, D), BF16),
                            jax.ShapeDtypeStruct((1, D), F32)],
                 args=(d, wt, x, g, dres), carry=carry)


def _loss_bwd(x, g, tgt):
    T, D = x.shape
    tm = _pick(T, 512, 16)

    def body(x_ref, g_ref, t_ref, loss_ref, dx_ref, dxb_ref, dg_ref):
        i = pl.program_id(0)
        xv = x_ref[...]
        gv = g_ref[...]
        e = xv * _rms(xv) * gv - t_ref[...]
        part = jnp.sum(jnp.sum(e * e, axis=-1, keepdims=True), axis=0, keepdims=True) * (0.5 / D)
        dxn, dgp = _norm_bwd(e * (1.0 / D), xv, gv)
        dx_ref[...] = dxn
        dxb_ref[...] = dxn.astype(BF16)

        @pl.when(i == 0)
        def _():
            loss_ref[...] = jnp.broadcast_to(part, loss_ref.shape)
            dg_ref[...] = dgp

        @pl.when(i > 0)
        def _():
            loss_ref[...] += jnp.broadcast_to(part, loss_ref.shape)
            dg_ref[...] += dgp

    row = pl.BlockSpec((tm, D), lambda i: (i, 0))
    vec = pl.BlockSpec((1, D), lambda i: (0, 0))
    return _call(body, name="loss_bwd", grid=(T // tm,), in_specs=[row, vec, row],
                 out_specs=[pl.BlockSpec((1, 128), lambda i: (0, 0)), row, row, vec],
                 out_shape=[jax.ShapeDtypeStruct((1, 128), F32), jax.ShapeDtypeStruct((T, D), F32),
                            jax.ShapeDtypeStruct((T, D), BF16), jax.ShapeDtypeStruct((1, D), F32)],
                 args=(x, g, tgt))[0]


def _sg_fwd(z, ws, bb, gn):
    T = z.shape[0]

    def body(zu_ref, zv_ref, ws_ref, bb_ref, gn_ref, a_ref):
        for h in range(SG_HEADS):
            sl = slice(h * HEAD_DIM, (h + 1) * HEAD_DIM)
            gv = _gelu(zv_ref[:, sl].astype(F32))
            vn = (gv * _rms(gv) * gn_ref[:, sl]).astype(BF16)
            mixed = jnp.dot(ws_ref[h], vn, preferred_element_type=F32) + bb_ref[h]
            a_ref[:, sl] = (_gelu(zu_ref[:, sl].astype(F32)) * mixed).astype(BF16)

    full = lambda shape: pl.BlockSpec(shape, lambda n: (0,) * len(shape))
    return _call(body, name="sg_fwd", grid=(T // SG_CHUNK,),
                 in_specs=[pl.BlockSpec((SG_CHUNK, SG_WIDTH), lambda n: (n, 0)),
                           pl.BlockSpec((SG_CHUNK, SG_WIDTH), lambda n: (n, 1)),
                           full((SG_HEADS, SG_CHUNK, SG_CHUNK)), full((SG_HEADS, SG_CHUNK, HEAD_DIM)),
                           full((1, SG_WIDTH))],
                 out_specs=[pl.BlockSpec((SG_CHUNK, SG_WIDTH), lambda n: (n, 0))],
                 out_shape=[jax.ShapeDtypeStruct((T, SG_WIDTH), BF16)], args=(z, z, ws, bb, gn))[0][0]


def _sg_bwd(z, dmix, ws, wst, bb, gn):
    T = z.shape[0]

    def body(zu_ref, zv_ref, da_ref, ws_ref, wst_ref, bb_ref, gn_ref, dzu_ref, dzv_ref, dws_ref, dbb_ref, dgn_ref):
        n = pl.program_id(0)

        @pl.when(n == 0)
        def _():
            dws_ref[...] = jnp.zeros_like(dws_ref)
            dbb_ref[...] = jnp.zeros_like(dbb_ref)
            dgn_ref[...] = jnp.zeros_like(dgn_ref)

        for h in range(SG_HEADS):
            sl = slice(h * HEAD_DIM, (h + 1) * HEAD_DIM)
            u = zu_ref[:, sl].astype(F32)
            v = zv_ref[:, sl].astype(F32)
            da = da_ref[:, sl].astype(F32)
            gain = gn_ref[:, sl]
            gv = _gelu(v)
            r = _rms(gv)
            vn = (gv * r * gain).astype(BF16)
            mixed = jnp.dot(ws_ref[h], vn, preferred_element_type=F32) + bb_ref[h]
            dmixed = da * _gelu(u)
            dzu_ref[:, sl] = (da * mixed * _gelu_grad(u)).astype(BF16)
            dmb = dmixed.astype(BF16)
            dws_ref[h] += lax.dot_general(dmb, vn, NT_DIMS, preferred_element_type=F32)
            dbb_ref[h] += jnp.broadcast_to(jnp.sum(dmixed, axis=-1, keepdims=True), (SG_CHUNK, HEAD_DIM))
            dvn = jnp.dot(wst_ref[h], dmb, preferred_element_type=F32)
            dgv, dg = _norm_bwd(dvn, gv, gain)
            dgn_ref[:, sl] += dg
            dzv_ref[:, sl] = (dgv * _gelu_grad(v)).astype(BF16)

    full = lambda shape: pl.BlockSpec(shape, lambda n: (0,) * len(shape))
    wspec = full((SG_HEADS, SG_CHUNK, SG_CHUNK))
    tile = lambda c: pl.BlockSpec((SG_CHUNK, SG_WIDTH), lambda n: (n, c))
    return _call(body, name="sg_bwd", grid=(T // SG_CHUNK,),
                 in_specs=[tile(0), tile(1), tile(0), wspec, wspec, full((SG_HEADS, SG_CHUNK, HEAD_DIM)),
                           full((1, SG_WIDTH))],
                 out_specs=[tile(0), tile(0), wspec, full((SG_HEADS, SG_CHUNK, HEAD_DIM)), full((1, SG_WIDTH))],
                 out_shape=[jax.ShapeDtypeStruct((T, SG_WIDTH), BF16), jax.ShapeDtypeStruct((T, SG_WIDTH), BF16),
                            jax.ShapeDtypeStruct((SG_HEADS, SG_CHUNK, SG_CHUNK), F32),
                            jax.ShapeDtypeStruct((SG_HEADS, SG_CHUNK, HEAD_DIM), F32),
                            jax.ShapeDtypeStruct((1, SG_WIDTH), F32)], args=(z, z, dmix, ws, wst, bb, gn))[0]


def _pool_specs(T, tp, col):
    step = tp // POOL_HALO
    last = T // POOL_HALO - 1
    return [pl.BlockSpec((POOL_HALO, POOL_WIDTH), lambda i: (jnp.maximum(i * step - 1, 0), col)),
            pl.BlockSpec((tp, POOL_WIDTH), lambda i: (i, col)),
            pl.BlockSpec((POOL_HALO, POOL_WIDTH), lambda i: (jnp.minimum((i + 1) * step, last), col))]


def _pool_band(i, tp, T, win):
    ext = tp + 2 * POOL_HALO
    t = i * tp + lax.broadcasted_iota(jnp.int32, (tp, ext), 0)
    s = i * tp - POOL_HALO + lax.broadcasted_iota(jnp.int32, (tp, ext), 1)
    band = (s >= jnp.maximum(t - win // 2, 0)) & (s < jnp.minimum(t + win // 2, T))
    t1 = i * tp + lax.broadcasted_iota(jnp.int32, (tp, 1), 0)
    cnt = (jnp.minimum(t1 + win // 2, T) - jnp.maximum(t1 - win // 2, 0)).astype(F32)
    return band.astype(BF16), cnt


def _pool_fwd(z, pw, psc):
    T = z.shape[0]
    tp = _pick(T, 256, POOL_HALO)

    def body(pp_ref, pc_ref, pn_ref, w_ref, sc_ref, o_ref):
        i = pl.program_id(0)
        halo = jnp.concatenate([pp_ref[...], pc_ref[...], pn_ref[...]], axis=0)
        for g, win in enumerate(POOL_WINDOWS):
            sl = slice(g * HEAD_DIM, (g + 1) * HEAD_DIM)
            band, cnt = _pool_band(i, tp, T, win)
            ssum = jnp.dot(band, halo[:, sl], preferred_element_type=F32)
            d = ssum / cnt - pc_ref[:, sl].astype(F32)
            y = jnp.dot(d.astype(BF16), w_ref[g], preferred_element_type=F32) * sc_ref[:, sl]
            o_ref[:, sl] = y.astype(BF16)

    full = lambda shape: pl.BlockSpec(shape, lambda i: (0,) * len(shape))
    return _call(body, name="pool_fwd", grid=(T // tp,),
                 in_specs=_pool_specs(T, tp, 2) + [full((4, HEAD_DIM, HEAD_DIM)), full((1, POOL_WIDTH))],
                 out_specs=[pl.BlockSpec((tp, POOL_WIDTH), lambda i: (i, 0))],
                 out_shape=[jax.ShapeDtypeStruct((T, POOL_WIDTH), BF16)], args=(z, z, z, pw, psc))[0][0]


def _pool_bwd(z, dmix, pw, psc):
    T = z.shape[0]
    tp = _pick(T, 256, POOL_HALO)
    ext = tp + 2 * POOL_HALO

    def body(pp_ref, pc_ref, pn_ref, dp_ref, dc_ref, dn_ref, w_ref, sc_ref, dz_ref, dw_ref, dsc_ref):
        i = pl.program_id(0)

        @pl.when(i == 0)
        def _():
            dw_ref[...] = jnp.zeros_like(dw_ref)
            dsc_ref[...] = jnp.zeros_like(dsc_ref)

        halo = jnp.concatenate([pp_ref[...], pc_ref[...], pn_ref[...]], axis=0)
        dy_halo = jnp.concatenate([dp_ref[...], dc_ref[...], dn_ref[...]], axis=0)
        th = i * tp - POOL_HALO + lax.broadcasted_iota(jnp.int32, (ext, 1), 0)
        inside = (th >= 0) & (th < T)
        s2 = i * tp + lax.broadcasted_iota(jnp.int32, (tp, ext), 0)
        t2 = i * tp - POOL_HALO + lax.broadcasted_iota(jnp.int32, (tp, ext), 1)
        for g, win in enumerate(POOL_WINDOWS):
            sl = slice(g * HEAD_DIM, (g + 1) * HEAD_DIM)
            sc = sc_ref[:, sl]
            band, cnt = _pool_band(i, tp, T, win)
            ssum = jnp.dot(band, halo[:, sl], preferred_element_type=F32)
            db = (ssum / cnt - pc_ref[:, sl].astype(F32)).astype(BF16)
            yraw = jnp.dot(db, w_ref[g], preferred_element_type=F32)
            dyc = dc_ref[:, sl].astype(F32)
            dsc_ref[:, sl] += jnp.sum(dyc * yraw, axis=0, keepdims=True)
            dw_ref[g] += lax.dot_general(db, (dyc * sc).astype(BF16), TN_DIMS, preferred_element_type=F32)
            dd = lax.dot_general((dy_halo[:, sl].astype(F32) * sc).astype(BF16), w_ref[g], NT_DIMS,
                                 preferred_element_type=F32)
            cnt_h = (jnp.minimum(th + win // 2, T) - jnp.maximum(th - win // 2, 0)).astype(F32)
            ddc = jnp.where(inside, dd / jnp.maximum(cnt_h, 1.0), 0.0)
            hi = ddc.astype(BF16)
            lo = (ddc - hi.astype(F32)).astype(BF16)
            band_t = ((s2 >= jnp.maximum(t2 - win // 2, 0)) & (s2 < jnp.minimum(t2 + win // 2, T))).astype(BF16)
            dpool = (jnp.dot(band_t, hi, preferred_element_type=F32) + jnp.dot(band_t, lo, preferred_element_type=F32)
                     - dd[POOL_HALO:POOL_HALO + tp])
            dz_ref[:, sl] = dpool.astype(BF16)

    full = lambda shape: pl.BlockSpec(shape, lambda i: (0,) * len(shape))
    return _call(body, name="pool_bwd", grid=(T // tp,),
                 in_specs=_pool_specs(T, tp, 2) + _pool_specs(T, tp, 1)
                 + [full((4, HEAD_DIM, HEAD_DIM)), full((1, POOL_WIDTH))],
                 out_specs=[pl.BlockSpec((tp, POOL_WIDTH), lambda i: (i, 0)), full((4, HEAD_DIM, HEAD_DIM)),
                            full((1, POOL_WIDTH))],
                 out_shape=[jax.ShapeDtypeStruct((T, POOL_WIDTH), BF16),
                            jax.ShapeDtypeStruct((4, HEAD_DIM, HEAD_DIM), F32),
                            jax.ShapeDtypeStruct((1, POOL_WIDTH), F32)], args=(z, z, z, dmix, dmix, dmix, pw, psc))[0]


ATT_ROWS = 8
WIN_KEYS = NA_KH * GRID_W


def _col_mask():
    q = lax.broadcasted_iota(jnp.int32, (GRID_W, WIN_KEYS), 0)
    k = lax.broadcasted_iota(jnp.int32, (GRID_W, WIN_KEYS), 1) & (GRID_W - 1)
    start = jnp.clip(q - NA_KW // 2, 0, GRID_W - NA_KW)
    return (k >= start) & (k < start + NA_KW)


def _softmax(s, bias, mask):
    s = jnp.where(mask, s * (HEAD_DIM ** -0.5) + bias, NEG)
    p = jnp.exp(s - jnp.max(s, axis=-1, keepdims=True))
    return p / jnp.sum(p, axis=-1, keepdims=True)


def _attn_window(step, a, rows):
    r = step * ATT_ROWS + a
    sr = jnp.clip(r - NA_KH // 2, 0, rows - NA_KH)
    return pl.ds(pl.multiple_of(sr * GRID_W, GRID_W), WIN_KEYS), sr - r + NA_KH - 1


def _head_specs(T, blk):
    whole = lambda first: pl.BlockSpec((1, T, HEAD_DIM), lambda h, s: (first + h, 0, 0))
    return [pl.BlockSpec((1, blk, HEAD_DIM), lambda h, s: (h, s, 0)), whole(NA_HEADS), whole(2 * NA_HEADS)]


def _attn_fwd(qkv, ecat, carry=None):
    T = qkv.shape[1]
    rows = T // GRID_W
    blk = ATT_ROWS * GRID_W

    def body(q_ref, k_ref, v_ref, e_ref, o_ref, s_scr, p_scr):
        step = pl.program_id(1)
        mask = _col_mask()
        wins = [_attn_window(step, a, rows) for a in range(ATT_ROWS)]
        qs = [slice(a * GRID_W, (a + 1) * GRID_W) for a in range(ATT_ROWS)]
        for a, (win, _) in enumerate(wins):
            s_scr[a] = lax.dot_general(q_ref[0, qs[a], :], k_ref[0, win, :], NT_DIMS, preferred_element_type=F32)
        for a, (_, dr0) in enumerate(wins):
            p_scr[a] = _softmax(s_scr[a], e_ref[0, dr0], mask).astype(BF16)
        for a, (win, _) in enumerate(wins):
            o_ref[0, qs[a], :] = jnp.dot(p_scr[a], v_ref[0, win, :], preferred_element_type=F32).astype(BF16)

    return _call(body, name="attn_fwd", grid=(NA_HEADS, rows // ATT_ROWS),
                 in_specs=_head_specs(T, blk)
                 + [pl.BlockSpec((1, NA_KH, GRID_W, WIN_KEYS), lambda h, s: (h, 0, 0, 0))],
                 out_specs=[pl.BlockSpec((1, blk, HEAD_DIM), lambda h, s: (h, s, 0))],
                 out_shape=[jax.ShapeDtypeStruct((NA_HEADS, T, HEAD_DIM), BF16)],
                 scratch_shapes=[pltpu.VMEM((ATT_ROWS, GRID_W, WIN_KEYS), F32),
                                 pltpu.VMEM((ATT_ROWS, GRID_W, WIN_KEYS), BF16)],
                 args=(qkv, qkv, qkv, ecat), carry=carry)


def _attn_bwd(qkv, dmix, ecat, carry=None):
    T = qkv.shape[1]
    rows = T // GRID_W
    blk = ATT_ROWS * GRID_W
    nstep = rows // ATT_ROWS

    def body(q_ref, k_ref, v_ref, do_ref, e_ref, dq_ref, dk_ref, dv_ref, de_ref, dk_acc, dv_acc, s_scr, dp_scr,
             p_scr, ds_scr):
        step = pl.program_id(1)
        mask = _col_mask()

        @pl.when(step == 0)
        def _():
            dk_acc[...] = jnp.zeros_like(dk_acc)
            dv_acc[...] = jnp.zeros_like(dv_acc)
            de_ref[...] = jnp.zeros_like(de_ref)

        wins = [_attn_window(step, a, rows) for a in range(ATT_ROWS)]
        qs = [slice(a * GRID_W, (a + 1) * GRID_W) for a in range(ATT_ROWS)]
        for a, (win, _) in enumerate(wins):
            s_scr[a] = lax.dot_general(q_ref[0, qs[a], :], k_ref[0, win, :], NT_DIMS, preferred_element_type=F32)
            dp_scr[a] = lax.dot_general(do_ref[qs[a], :], v_ref[0, win, :], NT_DIMS, preferred_element_type=F32)
        for a, (_, dr0) in enumerate(wins):
            pr = _softmax(s_scr[a], e_ref[0, dr0], mask)
            dp = dp_scr[a]
            ds = pr * (dp - jnp.sum(dp * pr, axis=-1, keepdims=True))
            de_ref[0, dr0] += ds
            p_scr[a] = pr.astype(BF16)
            ds_scr[a] = (ds * (HEAD_DIM ** -0.5)).astype(BF16)
        for a, (win, _) in enumerate(wins):
            dq_ref[0, qs[a], :] = jnp.dot(ds_scr[a], k_ref[0, win, :], preferred_element_type=F32).astype(BF16)
            dv_acc[win, :] += lax.dot_general(p_scr[a], do_ref[qs[a], :], TN_DIMS, preferred_element_type=F32)
            dk_acc[win, :] += lax.dot_general(ds_scr[a], q_ref[0, qs[a], :], TN_DIMS, preferred_element_type=F32)

        @pl.when(step == nstep - 1)
        def _():
            dk_ref[0] = dk_acc[...].astype(BF16)
            dv_ref[0] = dv_acc[...].astype(BF16)

    whole = pl.BlockSpec((1, T, HEAD_DIM), lambda h, s: (h, 0, 0))
    e_spec = pl.BlockSpec((1, NA_KH, GRID_W, WIN_KEYS), lambda h, s: (h, 0, 0, 0))
    out = jax.ShapeDtypeStruct((NA_HEADS, T, HEAD_DIM), BF16)
    stage = lambda dtype: pltpu.VMEM((ATT_ROWS, GRID_W, WIN_KEYS), dtype)
    return _call(body, name="attn_bwd", grid=(NA_HEADS, nstep),
                 in_specs=_head_specs(T, blk)
                 + [pl.BlockSpec((blk, HEAD_DIM), lambda h, s: (s, (SG_WIDTH + POOL_WIDTH) // HEAD_DIM + h)), e_spec],
                 out_specs=[pl.BlockSpec((1, blk, HEAD_DIM), lambda h, s: (h, s, 0)), whole, whole, e_spec],
                 out_shape=[out, out, out, jax.ShapeDtypeStruct((NA_HEADS, NA_KH, GRID_W, WIN_KEYS), F32)],
                 scratch_shapes=[pltpu.VMEM((T, HEAD_DIM), F32), pltpu.VMEM((T, HEAD_DIM), F32),
                                 stage(F32), stage(F32), stage(BF16), stage(BF16)],
                 args=(qkv, qkv, qkv, dmix, ecat), carry=carry)


def _rpb_tables():
    col = jnp.arange(GRID_W)
    dc = jnp.clip(col[None, :] - col[:, None] + NA_KW - 1, 0, 2 * NA_KW - 2)
    by_col = (dc[None] == jnp.arange(2 * NA_KW - 1)[:, None, None]).astype(F32)
    d, j = jnp.arange(NA_KH)[:, None], jnp.arange(NA_KH)[None, :]
    by_row = (jnp.arange(2 * NA_KH - 1)[:, None, None] == (d + j)[None]).astype(F32)
    return by_col, by_row


def _rpb_expand(rpb):
    by_col, by_row = _rpb_tables()
    e = jnp.einsum("hrc,cqk->hrqk", rpb, by_col, precision=lax.Precision.HIGHEST)
    ecat = jnp.einsum("hrqk,rdj->hdqjk", e, by_row, precision=lax.Precision.HIGHEST)
    return ecat.reshape(NA_HEADS, NA_KH, GRID_W, WIN_KEYS)


def _rpb_collect(decat):
    by_col, by_row = _rpb_tables()
    de = jnp.einsum("hdqjk,rdj->hrqk", decat.reshape(NA_HEADS, NA_KH, GRID_W, NA_KH, GRID_W), by_row,
                    precision=lax.Precision.HIGHEST)
    return jnp.einsum("hrqk,cqk->hrc", de, by_col, precision=lax.Precision.HIGHEST)


def _adamw(w, g, m, v):
    shape = w.shape
    C = shape[-1]
    R = w.size // C
    tr = _pick(R, max(8, (1 << 18) // C), 8)
    args = [a.reshape(R, C) for a in (w, g, m, v)]

    def body(w_ref, g_ref, m_ref, v_ref, d_ref, mo_ref, vo_ref):
        gv = g_ref[...]
        mn = ADAM_B1 * m_ref[...] + (1.0 - ADAM_B1) * gv
        vn = ADAM_B2 * v_ref[...] + (1.0 - ADAM_B2) * (gv * gv)
        m_hat = mn / (1.0 - ADAM_B1 ** ADAM_STEP)
        v_hat = vn / (1.0 - ADAM_B2 ** ADAM_STEP)
        d_ref[...] = -ADAM_LR * (m_hat / (jnp.sqrt(v_hat) + ADAM_EPS) + ADAM_WD * w_ref[...])
        mo_ref[...] = mn
        vo_ref[...] = vn

    spec = pl.BlockSpec((tr, C), lambda i: (i, 0))
    out = jax.ShapeDtypeStruct((R, C), F32)
    res = _call(body, name="adamw", grid=(R // tr,), in_specs=[spec] * 4, out_specs=[spec] * 3, out_shape=[out] * 3,
                args=args)[0]
    return [r.reshape(shape) for r in res]


def _sum_devices(g):
    G, _, R, C = g.shape
    tr = _pick(R, max(16, (1 << 18) // C), 16)

    def body(g_ref, o_ref):
        acc = g_ref[0, 0].astype(F32)
        for k in range(1, N_DEV):
            acc = acc + g_ref[0, k].astype(F32)
        o_ref[0] = acc

    return _call(body, name="sum_devices", grid=(G, R // tr),
                 in_specs=[pl.BlockSpec((1, N_DEV, tr, C), lambda t, i: (t, 0, i, 0))],
                 out_specs=[pl.BlockSpec((1, tr, C), lambda t, i: (t, i, 0))],
                 out_shape=[jax.ShapeDtypeStruct((G, R, C), F32)], args=(g,))[0][0]


def _sum_devices_into(g, prev, l, L, transpose):
    _, _, R, C = g.shape
    if transpose:
        tc = _pick(C, 256, 128)
        grid, shape = (C // tc,), (L, C, R)
        in_spec = pl.BlockSpec((1, N_DEV, R, tc), lambda i: (0, 0, 0, i))
        out_spec = pl.BlockSpec((1, tc, R), lambda i: (l, i, 0))
    else:
        tr = _pick(R, max(16, (1 << 18) // C), 16)
        grid, shape = (R // tr,), (L, R, C)
        in_spec = pl.BlockSpec((1, N_DEV, tr, C), lambda i: (0, 0, i, 0))
        out_spec = pl.BlockSpec((1, tr, C), lambda i: (l, i, 0))

    def body(g_ref, *rest):
        acc = g_ref[0, 0].astype(F32)
        for k in range(1, N_DEV):
            acc = acc + g_ref[0, k].astype(F32)
        rest[-1][0] = acc.T if transpose else acc

    first = prev is None
    return pl.pallas_call(
        body, name="sum_devices_into", grid=grid, in_specs=[in_spec] if first else [in_spec, ANY],
        out_specs=out_spec, out_shape=jax.ShapeDtypeStruct(shape, F32),
        input_output_aliases={} if first else {1: 0},
        compiler_params=pltpu.CompilerParams(dimension_semantics=("arbitrary",)))(*((g,) if first else (g, prev)))


SMALL = ("ffn1_norm", "mix_norm", "sg_norm", "sg_w", "sg_b", "pool_w", "pool_scale", "na_rpb", "ffn2_norm")
GROUPS = (("gu1", ("ffn1_w_gate", "ffn1_w_up"), True), ("down1", ("ffn1_w_down",), False), ("w_in", ("w_in",), True),
          ("w_out", ("w_out",), False), ("gu2", ("ffn2_w_gate", "ffn2_w_up"), True), ("down2", ("ffn2_w_down",), False))


def kernel(x, ffn1_norm, ffn1_w_gate, ffn1_w_up, ffn1_w_down, mix_norm, w_in, sg_norm, sg_w, sg_b, pool_w, pool_scale, na_rpb, w_out, ffn2_norm, ffn2_w_gate, ffn2_w_up, ffn2_w_down, final_norm, loss_target, m_ffn1_norm, m_ffn1_w_gate, m_ffn1_w_up, m_ffn1_w_down, m_mix_norm, m_w_in, m_sg_norm, m_sg_w, m_sg_b, m_pool_w, m_pool_scale, m_na_rpb, m_w_out, m_ffn2_norm, m_ffn2_w_gate, m_ffn2_w_up, m_ffn2_w_down, m_final_norm, v_ffn1_norm, v_ffn1_w_gate, v_ffn1_w_up, v_ffn1_w_down, v_mix_norm, v_w_in, v_sg_norm, v_sg_w, v_sg_b, v_pool_w, v_pool_scale, v_na_rpb, v_w_out, v_ffn2_norm, v_ffn2_w_gate, v_ffn2_w_up, v_ffn2_w_down, v_final_norm):
    given = dict(locals())
    T, D = x.shape[1], x.shape[2]
    L = ffn1_norm.shape[0]
    assert x.shape[0] == 1 and D == SG_WIDTH + POOL_WIDTH + NA_WIDTH and w_in.shape[2] * N_DEV == Z_COLS
    assert T % (ATT_ROWS * GRID_W) == 0 and T // GRID_W >= NA_KH
    x0 = x.reshape(T, D)
    tgt = loss_target.reshape(T, D)
    members = {grp: (names, cols) for grp, names, cols in GROUPS}

    def shard(grp, l):
        names, cols = members[grp]
        return jnp.stack([(given[n][l].T if cols else given[n][l]).astype(BF16) for n in names])

    def gather(l, *grps):
        return _Carry(gathers=[shard(grp, l) for grp in grps]) if l < L else None

    def full(gathered):
        return gathered.reshape(gathered.shape[0], -1, D)

    def by_token(heads):
        return heads.transpose(1, 0, 2).reshape(T, NA_WIDTH)

    W = {"gu1": full(_exchange(gather(0, "gu1"), "gather_first")[0])}
    saved = []
    xc = x0
    h = _rmsnorm(xc, ffn1_norm[0:1])
    for l in range(L):
        s = dict(x0=xc, h1=h)
        (s["a1"], s["pq1"]), got = _ffn_gu(h, W["gu1"], gather(l, "down1", "w_in"))
        W["down1"], W["w_in"] = map(full, got)
        (xc, h), got = _mm_res_norm(s["a1"], W["down1"][0], xc, mix_norm[l:l + 1], 0.5, gather(l, "w_out"))
        W["w_out"] = full(got[0])
        s["x1"], s["h2"] = xc, h
        (z, qkv), got = _z_proj(h, W["w_in"][0], gather(l, "down2"))
        W["down2"] = full(got[0])
        s["ws"] = sg_w[l].astype(BF16)
        s["wst"] = jnp.swapaxes(sg_w[l], 1, 2).astype(BF16)
        s["bb"] = jnp.broadcast_to(sg_b[l][:, :, None], (SG_HEADS, SG_CHUNK, HEAD_DIM))
        s["gn"] = sg_norm[l:l + 1]
        s["pw"] = pool_w[l].astype(BF16)
        s["psc"] = pool_scale[l:l + 1]
        s["ecat"] = _rpb_expand(na_rpb[l])
        (att,), got = _attn_fwd(qkv, s["ecat"], gather(l, "gu2"))
        W["gu2"] = full(got[0])
        mix = jnp.concatenate([_sg_fwd(z, s["ws"], s["bb"], s["gn"]), _pool_fwd(z, s["pw"], s["psc"]),
                               by_token(att)], axis=1)
        s["z"], s["qkv"], s["mix"] = z, qkv, mix
        (xc, h), _ = _mm_res_norm(mix, W["w_out"][0], xc, ffn2_norm[l:l + 1], 1.0)
        s["x2"], s["h3"] = xc, h
        (s["a2"], s["pq2"]), got = _ffn_gu(h, W["gu2"], gather(l + 1, "gu1"))
        s["W"] = W
        W = {"gu1": full(got[0])} if got else {}
        gnext = ffn1_norm[l + 1:l + 2] if l + 1 < L else final_norm.reshape(1, D)
        (xc, h), _ = _mm_res_norm(s["a2"], s["W"]["down2"][0], xc, gnext, 0.5)
        saved.append(s)

    loss_row, dx, dxb, dg_final = _loss_bwd(xc, final_norm.reshape(1, D), tgt)
    loss = lax.psum(loss_row[0, 0], MESH_AXES)

    received = {grp: [[None] * len(names) for _ in range(L)] for grp, names, _ in GROUPS}
    small = {n: [None] * L for n in SMALL}

    def slots(g, *groups):
        return g.reshape(g.shape[0], N_DEV, -1, D), groups or tuple(range(g.shape[0]))

    for l in reversed(range(L)):
        s = saved[l]
        W = s["W"]
        (dgu,), _ = _mm_nt(dxb, W["down2"][0], 0.5, pq=s["pq2"])
        (g_down2,), _ = _mm_tn(s["a2"][None], dxb, 0.5)
        (dx, dxb, dgn), got = _dh_norm_bwd(dgu, W["gu2"], s["x2"], ffn2_norm[l:l + 1], dx,
                                           _Carry(scatters=[slots(g_down2)]))
        received["down2"][l][0] = got[0]
        small["ffn2_norm"][l] = dgn
        (g_gu2,), _ = _mm_tn(dgu, s["h3"], 1.0)

        (dmix,), _ = _mm_nt(dxb, W["w_out"][0], 1.0)
        (g_out,), _ = _mm_tn(s["mix"][None], dxb, 1.0)
        dzu, dzv, dws, dbb, dgn = _sg_bwd(s["z"], dmix, s["ws"], s["wst"], s["bb"], s["gn"])
        dzp, dpw, dpsc = _pool_bwd(s["z"], dmix, s["pw"], s["psc"])
        (dq, dk, dv, decat), got = _attn_bwd(s["qkv"], dmix, s["ecat"], _Carry(scatters=[slots(g_gu2, 0)]))
        received["gu2"][l][0] = got[0]
        small["sg_w"][l], small["sg_b"][l], small["sg_norm"][l] = dws, dbb[:, :, 0], dgn[0]
        small["pool_w"][l], small["pool_scale"][l] = dpw, dpsc[0]
        small["na_rpb"][l] = _rpb_collect(decat)
        dz = jnp.concatenate([dzu, dzv, dzp, by_token(dq), by_token(dk), by_token(dv)], axis=1)
        (dx, dxb, dgn), got = _dh_norm_bwd(dz[None], W["w_in"], s["x1"], mix_norm[l:l + 1], dx,
                                           _Carry(scatters=[slots(g_gu2, 1)]))
        received["gu2"][l][1] = got[0]
        small["mix_norm"][l] = dgn
        (g_in,), got = _mm_tn(dz[None], s["h2"], 1.0, _Carry(scatters=[slots(g_out)]))
        received["w_out"][l][0] = got[0]

        (dgu,), _ = _mm_nt(dxb, W["down1"][0], 0.5, pq=s["pq1"])
        (g_down1,), got = _mm_tn(s["a1"][None], dxb, 0.5, _Carry(scatters=[slots(g_in)]))
        received["w_in"][l][0] = got[0]
        (g_gu1,), got = _mm_tn(dgu, s["h1"], 1.0, _Carry(scatters=[slots(g_down1)]))
        received["down1"][l][0] = got[0]
        (dx, dxb, dgn), got = _dh_norm_bwd(dgu, W["gu1"], s["x0"], ffn1_norm[l:l + 1], dx,
                                           _Carry(scatters=[slots(g_gu1, 0), slots(g_gu1, 1)]))
        received["gu1"][l][0], received["gu1"][l][1] = got
        small["ffn1_norm"][l] = dgn

    small_shapes = {n: given[n].shape for n in SMALL}
    small_shapes["final_norm"] = final_norm.shape
    flat = [jnp.stack([jnp.reshape(g, (-1,)) for g in small[n]]).reshape(-1) for n in SMALL] + [dg_final.reshape(-1)]
    sizes = [f.shape[0] for f in flat]
    total = sum(sizes)
    padded = -(-total // 2048) * 2048
    local = jnp.concatenate(flat + [jnp.zeros((padded - total,), F32)]).reshape(1, -1, 128)
    summed = _sum_devices(_exchange(_Carry(gathers=[local]), "gather_small_grads")[0]).reshape(-1)
    grads, off = {}, 0
    for n, size in zip(list(SMALL) + ["final_norm"], sizes):
        grads[n] = summed[off:off + size].reshape(small_shapes[n])
        off += size
    for grp, names, cols in GROUPS:
        for t, n in enumerate(names):
            grads[n] = None
            for l in range(L):
                grads[n] = _sum_devices_into(received[grp][l][t], grads[n], l, L, cols)

    names = ['ffn1_norm', 'ffn1_w_gate', 'ffn1_w_up', 'ffn1_w_down', 'mix_norm', 'w_in', 'sg_norm', 'sg_w', 'sg_b',
             'pool_w', 'pool_scale', 'na_rpb', 'w_out', 'ffn2_norm', 'ffn2_w_gate', 'ffn2_w_up', 'ffn2_w_down',
             'final_norm']
    delta, new_m, new_v = {}, {}, {}
    for n in names:
        delta[n], new_m[n], new_v[n] = _adamw(given[n], grads[n], given["m_" + n], given["v_" + n])
    return (loss, dx.reshape(1, T, D), *[grads[n] for n in names], *[delta[n] for n in names],
            *[new_m[n] for n in names], *[new_v[n] for n in names])
```

```python
import functools
import math

import jax
import jax.numpy as jnp
from jax import lax
from jax.experimental import pallas as pl
from jax.experimental.pallas import tpu as pltpu

F32 = jnp.float32
BF16 = jnp.bfloat16
EPS = 1e-6
NEG = -1e30

HEAD_DIM = 128
SG_WIDTH = 512
SG_HEADS = 4
SG_CHUNK = 128
POOL_WINDOWS = (2, 4, 8, 16)
POOL_WIDTH = 512
POOL_HALO = 128
NA_WIDTH = 1024
NA_HEADS = 8
NA_KH = 8
NA_KW = 16
GRID_W = 64
Z_COLS = 2 * SG_WIDTH + POOL_WIDTH + 3 * NA_WIDTH
Q_OFF = 2 * SG_WIDTH + POOL_WIDTH
K_OFF = Q_OFF + NA_WIDTH
V_OFF = K_OFF + NA_WIDTH

ADAM_LR = 0.001
ADAM_B1 = 0.9
ADAM_B2 = 0.999
ADAM_EPS = 1e-08
ADAM_WD = 0.01
ADAM_STEP = 10

N_DEV = 8
MESH_AXES = ("x", "y", "c")
MESH = pl.DeviceIdType.MESH
ANY = pl.BlockSpec(memory_space=pl.ANY)

NT_DIMS = (((1,), (1,)), ((), ()))
TN_DIMS = (((0,), (0,)), ((), ()))

ROWS_PREF = 1024
MXU_DEPTH = 256
K_PREF = 512


def _pick(n, pref, mult):
    best = None
    t = mult
    while t <= min(n, pref):
        if n % t == 0:
            best = t
        t += mult
    return n if best is None else best


def _gelu(x):
    return 0.5 * x * (1.0 + lax.erf(x * (1.0 / math.sqrt(2.0))))


def _gelu_grad(x):
    cdf = 0.5 * (1.0 + lax.erf(x * (1.0 / math.sqrt(2.0))))
    pdf = jnp.exp(-0.5 * x * x) * (1.0 / math.sqrt(2.0 * math.pi))
    return cdf + x * pdf


def _rms(x):
    return lax.rsqrt(jnp.mean(x * x, axis=-1, keepdims=True) + EPS)


def _norm_bwd(dh, x, g):
    r = _rms(x)
    w = dh * g
    dx = r * w - x * (r * r * r) * jnp.mean(w * x, axis=-1, keepdims=True)
    dg = jnp.sum(dh * (x * r), axis=0, keepdims=True)
    return dx, dg


def _position():
    return lax.axis_index("x"), lax.axis_index("y"), lax.axis_index("c")


def _index(p):
    return 4 * p[0] + 2 * p[1] + p[2]


class _Carry:
    def __init__(self, gathers=(), scatters=()):
        self.gathers, self.scatters = list(gathers), list(scatters)
        self.units = sum(g.shape[0] for g in self.gathers) + sum(len(groups) for _, groups in self.scatters)

    def arrays(self):
        return self.gathers + [s for s, _ in self.scatters]

    def out_shapes(self):
        return ([jax.ShapeDtypeStruct((g.shape[0], N_DEV) + g.shape[1:], g.dtype) for g in self.gathers]
                + [jax.ShapeDtypeStruct((len(groups),) + s.shape[1:], s.dtype) for s, groups in self.scatters])

    def scratch(self):
        return [pltpu.SemaphoreType.DMA((7 * self.units,)), pltpu.SemaphoreType.DMA((7 * self.units,)),
                pltpu.SemaphoreType.DMA((self.units,))]

    def _gather_copies(self, n, x_ref, out_ref, send, recv, local):
        x, y, c = _position()
        me, sibling = (x, y, c), (x, y, 1 - c)
        chips = [(1 - x, y), (x, 1 - y), (1 - x, 1 - y)]

        def copy(k, block, to, src=None):
            dst = out_ref.at[_index(block)]
            return pltpu.make_async_remote_copy(
                src_ref=dst if src is None else src, dst_ref=dst, send_sem=send.at[7 * n + k],
                recv_sem=recv.at[7 * n + k], device_id=to, device_id_type=MESH)

        return dict(
            mine=pltpu.make_async_copy(x_ref, out_ref.at[_index(me)], local.at[n]),
            first=[copy(0, me, sibling, x_ref)] + [copy(1 + j, me, (*ch, c), x_ref) for j, ch in enumerate(chips)],
            landed=[copy(1 + j, (*ch, c), me) for j, ch in enumerate(chips)],
            passed=[copy(4 + j, (*ch, c), sibling) for j, ch in enumerate(chips)],
            from_sibling=[copy(0, sibling, me)] + [copy(4 + j, (*ch, 1 - c), me) for j, ch in enumerate(chips)])

    def _scatter_copies(self, n, src_ref, out_ref, send, recv, local):
        x, y, c = _position()
        me = (x, y, c)
        sends, recvs = [], []
        for k in range(1, N_DEV):
            flip = lambda v, bit: 1 - v if bit else v
            peer = (flip(x, k & 4), flip(y, k & 2), flip(c, k & 1))
            sems = dict(send_sem=send.at[7 * n + k - 1], recv_sem=recv.at[7 * n + k - 1], device_id=peer,
                        device_id_type=MESH)
            sends.append(pltpu.make_async_remote_copy(src_ref=src_ref.at[_index(peer)], dst_ref=out_ref.at[_index(me)],
                                                      **sems))
            recvs.append(pltpu.make_async_remote_copy(src_ref=src_ref.at[_index(me)], dst_ref=out_ref.at[_index(peer)],
                                                      **sems))
        mine = pltpu.make_async_copy(src_ref.at[_index(me)], out_ref.at[_index(me)], local.at[n])
        return dict(mine=mine, sends=sends, recvs=recvs)

    def _pieces(self, ins, outs, sems):
        send, recv, local = sems
        gs, ss, unit = [], [], 0
        for n, g in enumerate(self.gathers):
            for t in range(g.shape[0]):
                gs.append(self._gather_copies(unit, ins[n].at[t], outs[n].at[t], send, recv, local))
                unit += 1
        for n, (_, groups) in enumerate(self.scatters, start=len(self.gathers)):
            for j, t in enumerate(groups):
                ss.append(self._scatter_copies(unit, ins[n].at[t], outs[n].at[j], send, recv, local))
                unit += 1
        return gs, ss

    def start(self, ins, outs, sems):
        gs, ss = self._pieces(ins, outs, sems)
        for g in gs:
            g["mine"].start()
            for cp in g["first"]:
                cp.start()
        for s in ss:
            s["mine"].start()
            for cp in s["sends"]:
                cp.start()

    def forward(self, ins, outs, sems):
        gs, _ = self._pieces(ins, outs, sems)
        for g in gs:
            for landed, passed in zip(g["landed"], g["passed"]):
                landed.wait_recv()
                passed.start()

    def finish(self, ins, outs, sems):
        gs, ss = self._pieces(ins, outs, sems)
        for g in gs:
            for cp in g["from_sibling"]:
                cp.wait_recv()
            for cp in g["first"] + g["passed"]:
                cp.wait_send()
            g["mine"].wait()
        for s in ss:
            for cp in s["recvs"]:
                cp.wait_recv()
            for cp in s["sends"]:
                cp.wait_send()
            s["mine"].wait()


def _call(body, *, name, grid, in_specs, out_specs, out_shape, args, scratch_shapes=(), carry=None):
    if carry is None or not carry.arrays():
        outs = pl.pallas_call(
            body, name=name, grid=grid, in_specs=in_specs, out_specs=out_specs, out_shape=out_shape,
            scratch_shapes=list(scratch_shapes),
            compiler_params=pltpu.CompilerParams(dimension_semantics=("arbitrary",) * len(grid)))(*args)
        return list(outs), []
    n_in, n_out, n_scr, n_car = len(in_specs), len(out_specs), len(scratch_shapes), len(carry.arrays())
    steps = math.prod(grid)
    middle = (steps * 6) // 10

    def wrapped(*refs):
        ins, refs = refs[:n_in], refs[n_in:]
        cins, refs = refs[:n_car], refs[n_car:]
        outs, refs = refs[:n_out], refs[n_out:]
        couts, refs = refs[:n_car], refs[n_car:]
        scr, sems = refs[:n_scr], refs[n_scr:]
        step = 0
        for d, size in enumerate(grid):
            step = step * size + pl.program_id(d)

        @pl.when(step == 0)
        def _():
            carry.start(cins, couts, sems)

        body(*ins, *outs, *scr)

        if carry.gathers:
            @pl.when(step == middle)
            def _():
                carry.forward(cins, couts, sems)

        @pl.when(step == steps - 1)
        def _():
            carry.finish(cins, couts, sems)

    outs = pl.pallas_call(
        wrapped, name=name + "_carry", grid=grid, in_specs=list(in_specs) + [ANY] * n_car,
        out_specs=list(out_specs) + [ANY] * n_car, out_shape=list(out_shape) + carry.out_shapes(),
        scratch_shapes=list(scratch_shapes) + carry.scratch(),
        compiler_params=pltpu.CompilerParams(dimension_semantics=("arbitrary",) * len(grid)))(*args, *carry.arrays())
    return list(outs[:n_out]), list(outs[n_out:])


def _exchange(carry, name):
    n_car = len(carry.arrays())

    def body(*refs):
        cins, couts, sems = refs[:n_car], refs[n_car:2 * n_car], refs[2 * n_car:]
        carry.start(cins, couts, sems)
        if carry.gathers:
            carry.forward(cins, couts, sems)
        carry.finish(cins, couts, sems)

    return list(pl.pallas_call(body, name=name, in_specs=[ANY] * n_car, out_specs=[ANY] * n_car,
                               out_shape=carry.out_shapes(), scratch_shapes=carry.scratch())(*carry.arrays()))


def _rmsnorm(x, g):
    T, D = x.shape
    tm = _pick(T, 512, 16)

    def body(x_ref, g_ref, o_ref):
        xv = x_ref[...]
        o_ref[...] = (xv * _rms(xv) * g_ref[...]).astype(BF16)

    return _call(body, name="rmsnorm", grid=(T // tm,),
                 in_specs=[pl.BlockSpec((tm, D), lambda i: (i, 0)), pl.BlockSpec((1, D), lambda i: (0, 0))],
                 out_specs=[pl.BlockSpec((tm, D), lambda i: (i, 0))],
                 out_shape=[jax.ShapeDtypeStruct((T, D), BF16)], args=(x, g))[0][0]


def _ffn_gu(h, wgu, carry=None):
    T, D = h.shape
    F = wgu.shape[1]
    tm = _pick(T, ROWS_PREF, 16)
    tn = _pick(F, 512, 128)

    def body(h_ref, wg_ref, wu_ref, a_ref, pq_ref):
        hv = h_ref[...]
        g = lax.dot_general(hv, wg_ref[0], NT_DIMS, preferred_element_type=F32)
        u = lax.dot_general(hv, wu_ref[0], NT_DIMS, preferred_element_type=F32)
        sg = jax.nn.sigmoid(g)
        q = g * sg
        a_ref[...] = (q * u).astype(BF16)
        pq_ref[0] = (u * (sg * (1.0 + g * (1.0 - sg)))).astype(BF16)
        pq_ref[1] = q.astype(BF16)

    return _call(body, name="ffn_gu", grid=(T // tm, F // tn),
                 in_specs=[pl.BlockSpec((tm, D), lambda i, j: (i, 0)),
                           pl.BlockSpec((1, tn, D), lambda i, j: (0, j, 0)),
                           pl.BlockSpec((1, tn, D), lambda i, j: (1, j, 0))],
                 out_specs=[pl.BlockSpec((tm, tn), lambda i, j: (i, j)),
                            pl.BlockSpec((2, tm, tn), lambda i, j: (0, i, j))],
                 out_shape=[jax.ShapeDtypeStruct((T, F), BF16), jax.ShapeDtypeStruct((2, T, F), BF16)],
                 args=(h, wgu, wgu), carry=carry)


def _mm_nt(a, w, scale, pq=None, carry=None):
    T, K = a.shape
    N = w.shape[0]
    tm = _pick(T, ROWS_PREF, 16)
    tn = _pick(N, 512, 128)

    def body(*refs):
        a_ref, w_ref = refs[:2]
        d = lax.dot_general(a_ref[...], w_ref[...], NT_DIMS, preferred_element_type=F32)
        if scale != 1.0:
            d = d * scale
        if pq is None:
            refs[2][...] = d.astype(BF16)
        else:
            pq_ref, o_ref = refs[2:]
            o_ref[0] = (d * pq_ref[0].astype(F32)).astype(BF16)
            o_ref[1] = (d * pq_ref[1].astype(F32)).astype(BF16)

    in_specs = [pl.BlockSpec((tm, K), lambda i, j: (i, 0)), pl.BlockSpec((tn, K), lambda i, j: (j, 0))]
    if pq is None:
        args, out_spec, out_shape = (a, w), pl.BlockSpec((tm, tn), lambda i, j: (i, j)), (T, N)
    else:
        in_specs.append(pl.BlockSpec((2, tm, tn), lambda i, j: (0, i, j)))
        args, out_spec, out_shape = (a, w, pq), pl.BlockSpec((2, tm, tn), lambda i, j: (0, i, j)), (2, T, N)
    return _call(body, name="mm_nt" if pq is None else "ffn_da", grid=(T // tm, N // tn),
                 in_specs=in_specs, out_specs=[out_spec], out_shape=[jax.ShapeDtypeStruct(out_shape, BF16)],
                 args=args, carry=carry)


def _z_proj(h, wt, carry=None):
    T, K = h.shape
    tm = _pick(T, ROWS_PREF, 16)
    tn = 4 * HEAD_DIM
    flat = Q_OFF // tn

    def body(h_ref, w_ref, z_ref, qkv_ref):
        j = pl.program_id(1)
        zv = lax.dot_general(h_ref[...], w_ref[...], NT_DIMS, preferred_element_type=F32).astype(BF16)

        @pl.when(j < flat)
        def _():
            z_ref[...] = zv

        @pl.when(j >= flat)
        def _():
            for c in range(tn // HEAD_DIM):
                qkv_ref[c] = zv[:, c * HEAD_DIM:(c + 1) * HEAD_DIM]

    return _call(body, name="z_proj", grid=(T // tm, Z_COLS // tn),
                 in_specs=[pl.BlockSpec((tm, K), lambda i, j: (i, 0)), pl.BlockSpec((tn, K), lambda i, j: (j, 0))],
                 out_specs=[pl.BlockSpec((tm, tn), lambda i, j: (i, jnp.minimum(j, flat - 1))),
                            pl.BlockSpec((tn // HEAD_DIM, tm, HEAD_DIM), lambda i, j: (jnp.maximum(j - flat, 0), i, 0))],
                 out_shape=[jax.ShapeDtypeStruct((T, Q_OFF), BF16),
                            jax.ShapeDtypeStruct((3 * NA_HEADS, T, HEAD_DIM), BF16)],
                 args=(h, wt), carry=carry)


def _mm_res_norm(a, w, x, gnext, scale, carry=None):
    T, K = a.shape
    D = w.shape[1]
    tm = _pick(T, 512, 16)
    tk = _pick(K, K_PREF, MXU_DEPTH)
    nk = K // tk

    def body(a_ref, w_ref, x_ref, g_ref, xo_ref, ho_ref, acc):
        k = pl.program_id(1)

        @pl.when(k == 0)
        def _():
            acc[...] = jnp.zeros_like(acc)

        acc[...] += jnp.dot(a_ref[...], w_ref[...], preferred_element_type=F32)

        @pl.when(k == nk - 1)
        def _():
            xn = x_ref[...] + scale * acc[...]
            xo_ref[...] = xn
            ho_ref[...] = (xn * _rms(xn) * g_ref[...]).astype(BF16)

    row = pl.BlockSpec((tm, D), lambda i, k: (i, 0))
    return _call(body, name="mm_res_norm", grid=(T // tm, nk),
                 in_specs=[pl.BlockSpec((tm, tk), lambda i, k: (i, k)), pl.BlockSpec((tk, D), lambda i, k: (k, 0)),
                           row, pl.BlockSpec((1, D), lambda i, k: (0, 0))],
                 out_specs=[row, row],
                 out_shape=[jax.ShapeDtypeStruct((T, D), F32), jax.ShapeDtypeStruct((T, D), BF16)],
                 scratch_shapes=[pltpu.VMEM((tm, D), F32)], args=(a, w, x, gnext), carry=carry)


def _mm_tn(a, b, scale, carry=None):
    G, T, M = a.shape
    N = b.shape[1]
    tm = _pick(M, 512, 128)
    tk = _pick(T, 2048, 16)
    nk = T // tk

    def body(a_ref, b_ref, o_ref, acc):
        k = pl.program_id(2)
        p = lax.dot_general(a_ref[0], b_ref[...], TN_DIMS, preferred_element_type=F32)

        @pl.when(k == 0)
        def _():
            acc[...] = p

        @pl.when(k > 0)
        def _():
            acc[...] += p

        @pl.when(k == nk - 1)
        def _():
            o_ref[0] = (acc[...] * scale).astype(BF16)

    return _call(body, name="mm_tn", grid=(G, M // tm, nk),
                 in_specs=[pl.BlockSpec((1, tk, tm), lambda g, i, k: (g, k, i)),
                           pl.BlockSpec((tk, N), lambda g, i, k: (k, 0))],
                 out_specs=[pl.BlockSpec((1, tm, N), lambda g, i, k: (g, i, 0))],
                 out_shape=[jax.ShapeDtypeStruct((G, M, N), BF16)],
                 scratch_shapes=[pltpu.VMEM((tm, N), F32)], args=(a, b), carry=carry)


def _dh_norm_bwd(d, wt, x, g, dres, carry=None):
    T, D = x.shape
    G, _, K = d.shape
    tm = _pick(T, 512, 16)
    tk = _pick(K, K_PREF, MXU_DEPTH)
    nk = K // tk

    def body(d_ref, w_ref, x_ref, g_ref, dres_ref, dx_ref, dxb_ref, dg_ref, acc):
        i = pl.program_id(0)
        k = pl.program_id(1)

        @pl.when(k == 0)
        def _():
            acc[...] = jnp.zeros_like(acc)

        acc[...] += jnp.dot(d_ref[0], w_ref[0], preferred_element_type=F32)

        @pl.when(k == G * nk - 1)
        def _():
            dxn, dgp = _norm_bwd(acc[...], x_ref[...], g_ref[...])
            dxv = dres_ref[...] + dxn
            dx_ref[...] = dxv
            dxb_ref[...] = dxv.astype(BF16)

            @pl.when(i == 0)
            def _():
                dg_ref[...] = dgp

            @pl.when(i > 0)
            def _():
                dg_ref[...] += dgp

    row = pl.BlockSpec((tm, D), lambda i, k: (i, 0))
    vec = pl.BlockSpec((1, D), lambda i, k: (0, 0))
    return _call(body, name="dh_norm_bwd", grid=(T // tm, G * nk),
                 in_specs=[pl.BlockSpec((1, tm, tk), lambda i, k: (k // nk, i, k % nk)),
                           pl.BlockSpec((1, tk, D), lambda i, k: (k // nk, k % nk, 0)), row, vec, row],
                 out_specs=[row, row, vec],
                 out_shape=[jax.ShapeDtypeStruct((T, D), F32), jax.ShapeDtypeStruct((T, D), BF16),
                            jax.ShapeDtypeStruct((1, D), F32)],
                 scratch_shapes=[pltpu.VMEM((tm, D), F32)], args=(d, wt, x, g, dres), carry=carry)


def _loss_bwd(x, g, tgt):
    T, D = x.shape
    tm = _pick(T, 512, 16)

    def body(x_ref, g_ref, t_ref, loss_ref, dx_ref, dxb_ref, dg_ref):
        i = pl.program_id(0)
        xv = x_ref[...]
        gv = g_ref[...]
        e = xv * _rms(xv) * gv - t_ref[...]
        part = jnp.sum(jnp.sum(e * e, axis=-1, keepdims=True), axis=0, keepdims=True) * (0.5 / D)
        dxn, dgp = _norm_bwd(e * (1.0 / D), xv, gv)
        dx_ref[...] = dxn
        dxb_ref[...] = dxn.astype(BF16)

        @pl.when(i == 0)
        def _():
            loss_ref[...] = jnp.broadcast_to(part, loss_ref.shape)
            dg_ref[...] = dgp

        @pl.when(i > 0)
        def _():
            loss_ref[...] += jnp.broadcast_to(part, loss_ref.shape)
            dg_ref[...] += dgp

    row = pl.BlockSpec((tm, D), lambda i: (i, 0))
    vec = pl.BlockSpec((1, D), lambda i: (0, 0))
    return _call(body, name="loss_bwd", grid=(T // tm,), in_specs=[row, vec, row],
                 out_specs=[pl.BlockSpec((1, 128), lambda i: (0, 0)), row, row, vec],
                 out_shape=[jax.ShapeDtypeStruct((1, 128), F32), jax.ShapeDtypeStruct((T, D), F32),
                            jax.ShapeDtypeStruct((T, D), BF16), jax.ShapeDtypeStruct((1, D), F32)],
                 args=(x, g, tgt))[0]


def _sg_fwd(z, ws, bb, gn):
    T = z.shape[0]

    def body(zu_ref, zv_ref, ws_ref, bb_ref, gn_ref, a_ref):
        for h in range(SG_HEADS):
            sl = slice(h * HEAD_DIM, (h + 1) * HEAD_DIM)
            gv = _gelu(zv_ref[:, sl].astype(F32))
            vn = (gv * _rms(gv) * gn_ref[:, sl]).astype(BF16)
            mixed = jnp.dot(ws_ref[h], vn, preferred_element_type=F32) + bb_ref[h]
            a_ref[:, sl] = (_gelu(zu_ref[:, sl].astype(F32)) * mixed).astype(BF16)

    full = lambda shape: pl.BlockSpec(shape, lambda n: (0,) * len(shape))
    return _call(body, name="sg_fwd", grid=(T // SG_CHUNK,),
                 in_specs=[pl.BlockSpec((SG_CHUNK, SG_WIDTH), lambda n: (n, 0)),
                           pl.BlockSpec((SG_CHUNK, SG_WIDTH), lambda n: (n, 1)),
                           full((SG_HEADS, SG_CHUNK, SG_CHUNK)), full((SG_HEADS, SG_CHUNK, HEAD_DIM)),
                           full((1, SG_WIDTH))],
                 out_specs=[pl.BlockSpec((SG_CHUNK, SG_WIDTH), lambda n: (n, 0))],
                 out_shape=[jax.ShapeDtypeStruct((T, SG_WIDTH), BF16)], args=(z, z, ws, bb, gn))[0][0]


def _sg_bwd(z, dmix, ws, wst, bb, gn):
    T = z.shape[0]

    def body(zu_ref, zv_ref, da_ref, ws_ref, wst_ref, bb_ref, gn_ref, dzu_ref, dzv_ref, dws_ref, dbb_ref, dgn_ref):
        n = pl.program_id(0)

        @pl.when(n == 0)
        def _():
            dws_ref[...] = jnp.zeros_like(dws_ref)
            dbb_ref[...] = jnp.zeros_like(dbb_ref)
            dgn_ref[...] = jnp.zeros_like(dgn_ref)

        for h in range(SG_HEADS):
            sl = slice(h * HEAD_DIM, (h + 1) * HEAD_DIM)
            u = zu_ref[:, sl].astype(F32)
            v = zv_ref[:, sl].astype(F32)
            da = da_ref[:, sl].astype(F32)
            gain = gn_ref[:, sl]
            gv = _gelu(v)
            r = _rms(gv)
            vn = (gv * r * gain).astype(BF16)
            mixed = jnp.dot(ws_ref[h], vn, preferred_element_type=F32) + bb_ref[h]
            dmixed = da * _gelu(u)
            dzu_ref[:, sl] = (da * mixed * _gelu_grad(u)).astype(BF16)
            dmb = dmixed.astype(BF16)
            dws_ref[h] += lax.dot_general(dmb, vn, NT_DIMS, preferred_element_type=F32)
            dbb_ref[h] += jnp.broadcast_to(jnp.sum(dmixed, axis=-1, keepdims=True), (SG_CHUNK, HEAD_DIM))
            dvn = jnp.dot(wst_ref[h], dmb, preferred_element_type=F32)
            dgv, dg = _norm_bwd(dvn, gv, gain)
            dgn_ref[:, sl] += dg
            dzv_ref[:, sl] = (dgv * _gelu_grad(v)).astype(BF16)

    full = lambda shape: pl.BlockSpec(shape, lambda n: (0,) * len(shape))
    wspec = full((SG_HEADS, SG_CHUNK, SG_CHUNK))
    tile = lambda c: pl.BlockSpec((SG_CHUNK, SG_WIDTH), lambda n: (n, c))
    return _call(body, name="sg_bwd", grid=(T // SG_CHUNK,),
                 in_specs=[tile(0), tile(1), tile(0), wspec, wspec, full((SG_HEADS, SG_CHUNK, HEAD_DIM)),
                           full((1, SG_WIDTH))],
                 out_specs=[tile(0), tile(0), wspec, full((SG_HEADS, SG_CHUNK, HEAD_DIM)), full((1, SG_WIDTH))],
                 out_shape=[jax.ShapeDtypeStruct((T, SG_WIDTH), BF16), jax.ShapeDtypeStruct((T, SG_WIDTH), BF16),
                            jax.ShapeDtypeStruct((SG_HEADS, SG_CHUNK, SG_CHUNK), F32),
                            jax.ShapeDtypeStruct((SG_HEADS, SG_CHUNK, HEAD_DIM), F32),
                            jax.ShapeDtypeStruct((1, SG_WIDTH), F32)], args=(z, z, dmix, ws, wst, bb, gn))[0]


def _pool_specs(T, tp, col):
    step = tp // POOL_HALO
    last = T // POOL_HALO - 1
    return [pl.BlockSpec((POOL_HALO, POOL_WIDTH), lambda i: (jnp.maximum(i * step - 1, 0), col)),
            pl.BlockSpec((tp, POOL_WIDTH), lambda i: (i, col)),
            pl.BlockSpec((POOL_HALO, POOL_WIDTH), lambda i: (jnp.minimum((i + 1) * step, last), col))]


def _pool_band(i, tp, T, win):
    ext = tp + 2 * POOL_HALO
    t = i * tp + lax.broadcasted_iota(jnp.int32, (tp, ext), 0)
    s = i * tp - POOL_HALO + lax.broadcasted_iota(jnp.int32, (tp, ext), 1)
    band = (s >= jnp.maximum(t - win // 2, 0)) & (s < jnp.minimum(t + win // 2, T))
    t1 = i * tp + lax.broadcasted_iota(jnp.int32, (tp, 1), 0)
    cnt = (jnp.minimum(t1 + win // 2, T) - jnp.maximum(t1 - win // 2, 0)).astype(F32)
    return band.astype(BF16), cnt


def _pool_fwd(z, pw, psc):
    T = z.shape[0]
    tp = _pick(T, 256, POOL_HALO)

    def body(pp_ref, pc_ref, pn_ref, w_ref, sc_ref, o_ref):
        i = pl.program_id(0)
        halo = jnp.concatenate([pp_ref[...], pc_ref[...], pn_ref[...]], axis=0)
        for g, win in enumerate(POOL_WINDOWS):
            sl = slice(g * HEAD_DIM, (g + 1) * HEAD_DIM)
            band, cnt = _pool_band(i, tp, T, win)
            ssum = jnp.dot(band, halo[:, sl], preferred_element_type=F32)
            d = ssum / cnt - pc_ref[:, sl].astype(F32)
            y = jnp.dot(d.astype(BF16), w_ref[g], preferred_element_type=F32) * sc_ref[:, sl]
            o_ref[:, sl] = y.astype(BF16)

    full = lambda shape: pl.BlockSpec(shape, lambda i: (0,) * len(shape))
    return _call(body, name="pool_fwd", grid=(T // tp,),
                 in_specs=_pool_specs(T, tp, 2) + [full((4, HEAD_DIM, HEAD_DIM)), full((1, POOL_WIDTH))],
                 out_specs=[pl.BlockSpec((tp, POOL_WIDTH), lambda i: (i, 0))],
                 out_shape=[jax.ShapeDtypeStruct((T, POOL_WIDTH), BF16)], args=(z, z, z, pw, psc))[0][0]


def _pool_bwd(z, dmix, pw, psc):
    T = z.shape[0]
    tp = _pick(T, 256, POOL_HALO)
    ext = tp + 2 * POOL_HALO

    def body(pp_ref, pc_ref, pn_ref, dp_ref, dc_ref, dn_ref, w_ref, sc_ref, dz_ref, dw_ref, dsc_ref):
        i = pl.program_id(0)

        @pl.when(i == 0)
        def _():
            dw_ref[...] = jnp.zeros_like(dw_ref)
            dsc_ref[...] = jnp.zeros_like(dsc_ref)

        halo = jnp.concatenate([pp_ref[...], pc_ref[...], pn_ref[...]], axis=0)
        dy_halo = jnp.concatenate([dp_ref[...], dc_ref[...], dn_ref[...]], axis=0)
        th = i * tp - POOL_HALO + lax.broadcasted_iota(jnp.int32, (ext, 1), 0)
        inside = (th >= 0) & (th < T)
        s2 = i * tp + lax.broadcasted_iota(jnp.int32, (tp, ext), 0)
        t2 = i * tp - POOL_HALO + lax.broadcasted_iota(jnp.int32, (tp, ext), 1)
        for g, win in enumerate(POOL_WINDOWS):
            sl = slice(g * HEAD_DIM, (g + 1) * HEAD_DIM)
            sc = sc_ref[:, sl]
            band, cnt = _pool_band(i, tp, T, win)
            ssum = jnp.dot(band, halo[:, sl], preferred_element_type=F32)
            db = (ssum / cnt - pc_ref[:, sl].astype(F32)).astype(BF16)
            yraw = jnp.dot(db, w_ref[g], preferred_element_type=F32)
            dyc = dc_ref[:, sl].astype(F32)
            dsc_ref[:, sl] += jnp.sum(dyc * yraw, axis=0, keepdims=True)
            dw_ref[g] += lax.dot_general(db, (dyc * sc).astype(BF16), TN_DIMS, preferred_element_type=F32)
            dd = lax.dot_general((dy_halo[:, sl].astype(F32) * sc).astype(BF16), w_ref[g], NT_DIMS,
                                 preferred_element_type=F32)
            cnt_h = (jnp.minimum(th + win // 2, T) - jnp.maximum(th - win // 2, 0)).astype(F32)
            ddc = jnp.where(inside, dd / jnp.maximum(cnt_h, 1.0), 0.0)
            hi = ddc.astype(BF16)
            lo = (ddc - hi.astype(F32)).astype(BF16)
            band_t = ((s2 >= jnp.maximum(t2 - win // 2, 0)) & (s2 < jnp.minimum(t2 + win // 2, T))).astype(BF16)
            dpool = (jnp.dot(band_t, hi, preferred_element_type=F32) + jnp.dot(band_t, lo, preferred_element_type=F32)
                     - dd[POOL_HALO:POOL_HALO + tp])
            dz_ref[:, sl] = dpool.astype(BF16)

    full = lambda shape: pl.BlockSpec(shape, lambda i: (0,) * len(shape))
    return _call(body, name="pool_bwd", grid=(T // tp,),
                 in_specs=_pool_specs(T, tp, 2) + _pool_specs(T, tp, 1)
                 + [full((4, HEAD_DIM, HEAD_DIM)), full((1, POOL_WIDTH))],
                 out_specs=[pl.BlockSpec((tp, POOL_WIDTH), lambda i: (i, 0)), full((4, HEAD_DIM, HEAD_DIM)),
                            full((1, POOL_WIDTH))],
                 out_shape=[jax.ShapeDtypeStruct((T, POOL_WIDTH), BF16),
                            jax.ShapeDtypeStruct((4, HEAD_DIM, HEAD_DIM), F32),
                            jax.ShapeDtypeStruct((1, POOL_WIDTH), F32)], args=(z, z, z, dmix, dmix, dmix, pw, psc))[0]


ATT_ROWS = 8
WIN_KEYS = NA_KH * GRID_W


def _col_mask():
    q = lax.broadcasted_iota(jnp.int32, (GRID_W, WIN_KEYS), 0)
    k = lax.broadcasted_iota(jnp.int32, (GRID_W, WIN_KEYS), 1) & (GRID_W - 1)
    start = jnp.clip(q - NA_KW // 2, 0, GRID_W - NA_KW)
    return (k >= start) & (k < start + NA_KW)


def _softmax(s, bias, mask):
    s = jnp.where(mask, s * (HEAD_DIM ** -0.5) + bias, NEG)
    p = jnp.exp(s - jnp.max(s, axis=-1, keepdims=True))
    return p / jnp.sum(p, axis=-1, keepdims=True)


def _attn_window(step, a, rows):
    r = step * ATT_ROWS + a
    sr = jnp.clip(r - NA_KH // 2, 0, rows - NA_KH)
    return pl.ds(pl.multiple_of(sr * GRID_W, GRID_W), WIN_KEYS), sr - r + NA_KH - 1


def _head_specs(T, blk):
    whole = lambda first: pl.BlockSpec((1, T, HEAD_DIM), lambda h, s: (first + h, 0, 0))
    return [pl.BlockSpec((1, blk, HEAD_DIM), lambda h, s: (h, s, 0)), whole(NA_HEADS), whole(2 * NA_HEADS)]


def _attn_fwd(qkv, ecat, carry=None):
    T = qkv.shape[1]
    rows = T // GRID_W
    blk = ATT_ROWS * GRID_W

    def body(q_ref, k_ref, v_ref, e_ref, o_ref, s_scr, p_scr):
        step = pl.program_id(1)
        mask = _col_mask()
        wins = [_attn_window(step, a, rows) for a in range(ATT_ROWS)]
        qs = [slice(a * GRID_W, (a + 1) * GRID_W) for a in range(ATT_ROWS)]
        for a, (win, _) in enumerate(wins):
            s_scr[a] = lax.dot_general(q_ref[0, qs[a], :], k_ref[0, win, :], NT_DIMS, preferred_element_type=F32)
        for a, (_, dr0) in enumerate(wins):
            p_scr[a] = _softmax(s_scr[a], e_ref[0, dr0], mask).astype(BF16)
        for a, (win, _) in enumerate(wins):
            o_ref[0, qs[a], :] = jnp.dot(p_scr[a], v_ref[0, win, :], preferred_element_type=F32).astype(BF16)

    return _call(body, name="attn_fwd", grid=(NA_HEADS, rows // ATT_ROWS),
                 in_specs=_head_specs(T, blk)
                 + [pl.BlockSpec((1, NA_KH, GRID_W, WIN_KEYS), lambda h, s: (h, 0, 0, 0))],
                 out_specs=[pl.BlockSpec((1, blk, HEAD_DIM), lambda h, s: (h, s, 0))],
                 out_shape=[jax.ShapeDtypeStruct((NA_HEADS, T, HEAD_DIM), BF16)],
                 scratch_shapes=[pltpu.VMEM((ATT_ROWS, GRID_W, WIN_KEYS), F32),
                                 pltpu.VMEM((ATT_ROWS, GRID_W, WIN_KEYS), BF16)],
                 args=(qkv, qkv, qkv, ecat), carry=carry)


def _attn_bwd(qkv, dmix, ecat, carry=None):
    T = qkv.shape[1]
    rows = T // GRID_W
    blk = ATT_ROWS * GRID_W
    nstep = rows // ATT_ROWS

    def body(q_ref, k_ref, v_ref, do_ref, e_ref, dq_ref, dk_ref, dv_ref, de_ref, dk_acc, dv_acc, s_scr, dp_scr,
             p_scr, ds_scr):
        step = pl.program_id(1)
        mask = _col_mask()

        @pl.when(step == 0)
        def _():
            dk_acc[...] = jnp.zeros_like(dk_acc)
            dv_acc[...] = jnp.zeros_like(dv_acc)
            de_ref[...] = jnp.zeros_like(de_ref)

        wins = [_attn_window(step, a, rows) for a in range(ATT_ROWS)]
        qs = [slice(a * GRID_W, (a + 1) * GRID_W) for a in range(ATT_ROWS)]
        for a, (win, _) in enumerate(wins):
            s_scr[a] = lax.dot_general(q_ref[0, qs[a], :], k_ref[0, win, :], NT_DIMS, preferred_element_type=F32)
            dp_scr[a] = lax.dot_general(do_ref[qs[a], :], v_ref[0, win, :], NT_DIMS, preferred_element_type=F32)
        for a, (_, dr0) in enumerate(wins):
            pr = _softmax(s_scr[a], e_ref[0, dr0], mask)
            dp = dp_scr[a]
            ds = pr * (dp - jnp.sum(dp * pr, axis=-1, keepdims=True))
            de_ref[0, dr0] += ds
            p_scr[a] = pr.astype(BF16)
            ds_scr[a] = (ds * (HEAD_DIM ** -0.5)).astype(BF16)
        for a, (win, _) in enumerate(wins):
            dq_ref[0, qs[a], :] = jnp.dot(ds_scr[a], k_ref[0, win, :], preferred_element_type=F32).astype(BF16)
            dv_acc[win, :] += lax.dot_general(p_scr[a], do_ref[qs[a], :], TN_DIMS, preferred_element_type=F32)
            dk_acc[win, :] += lax.dot_general(ds_scr[a], q_ref[0, qs[a], :], TN_DIMS, preferred_element_type=F32)

        @pl.when(step == nstep - 1)
        def _():
            dk_ref[0] = dk_acc[...].astype(BF16)
            dv_ref[0] = dv_acc[...].astype(BF16)

    whole = pl.BlockSpec((1, T, HEAD_DIM), lambda h, s: (h, 0, 0))
    e_spec = pl.BlockSpec((1, NA_KH, GRID_W, WIN_KEYS), lambda h, s: (h, 0, 0, 0))
    out = jax.ShapeDtypeStruct((NA_HEADS, T, HEAD_DIM), BF16)
    stage = lambda dtype: pltpu.VMEM((ATT_ROWS, GRID_W, WIN_KEYS), dtype)
    return _call(body, name="attn_bwd", grid=(NA_HEADS, nstep),
                 in_specs=_head_specs(T, blk)
                 + [pl.BlockSpec((blk, HEAD_DIM), lambda h, s: (s, (SG_WIDTH + POOL_WIDTH) // HEAD_DIM + h)), e_spec],
                 out_specs=[pl.BlockSpec((1, blk, HEAD_DIM), lambda h, s: (h, s, 0)), whole, whole, e_spec],
                 out_shape=[out, out, out, jax.ShapeDtypeStruct((NA_HEADS, NA_KH, GRID_W, WIN_KEYS), F32)],
                 scratch_shapes=[pltpu.VMEM((T, HEAD_DIM), F32), pltpu.VMEM((T, HEAD_DIM), F32),
                                 stage(F32), stage(F32), stage(BF16), stage(BF16)],
                 args=(qkv, qkv, qkv, dmix, ecat), carry=carry)


def _rpb_tables():
    col = jnp.arange(GRID_W)
    dc = jnp.clip(col[None, :] - col[:, None] + NA_KW - 1, 0, 2 * NA_KW - 2)
    by_col = (dc[None] == jnp.arange(2 * NA_KW - 1)[:, None, None]).astype(F32)
    d, j = jnp.arange(NA_KH)[:, None], jnp.arange(NA_KH)[None, :]
    by_row = (jnp.arange(2 * NA_KH - 1)[:, None, None] == (d + j)[None]).astype(F32)
    return by_col, by_row


def _rpb_expand(rpb):
    by_col, by_row = _rpb_tables()
    e = jnp.einsum("hrc,cqk->hrqk", rpb, by_col, precision=lax.Precision.HIGHEST)
    ecat = jnp.einsum("hrqk,rdj->hdqjk", e, by_row, precision=lax.Precision.HIGHEST)
    return ecat.reshape(NA_HEADS, NA_KH, GRID_W, WIN_KEYS)


def _rpb_collect(decat):
    by_col, by_row = _rpb_tables()
    de = jnp.einsum("hdqjk,rdj->hrqk", decat.reshape(NA_HEADS, NA_KH, GRID_W, NA_KH, GRID_W), by_row,
                    precision=lax.Precision.HIGHEST)
    return jnp.einsum("hrqk,cqk->hrc", de, by_col, precision=lax.Precision.HIGHEST)


def _adamw(w, g, m, v):
    shape = w.shape
    C = shape[-1]
    R = w.size // C
    tr = _pick(R, max(8, (1 << 18) // C), 8)
    args = [a.reshape(R, C) for a in (w, g, m, v)]

    def body(w_ref, g_ref, m_ref, v_ref, d_ref, mo_ref, vo_ref):
        gv = g_ref[...]
        mn = ADAM_B1 * m_ref[...] + (1.0 - ADAM_B1) * gv
        vn = ADAM_B2 * v_ref[...] + (1.0 - ADAM_B2) * (gv * gv)
        m_hat = mn / (1.0 - ADAM_B1 ** ADAM_STEP)
        v_hat = vn / (1.0 - ADAM_B2 ** ADAM_STEP)
        d_ref[...] = -ADAM_LR * (m_hat / (jnp.sqrt(v_hat) + ADAM_EPS) + ADAM_WD * w_ref[...])
        mo_ref[...] = mn
        vo_ref[...] = vn

    spec = pl.BlockSpec((tr, C), lambda i: (i, 0))
    out = jax.ShapeDtypeStruct((R, C), F32)
    res = _call(body, name="adamw", grid=(R // tr,), in_specs=[spec] * 4, out_specs=[spec] * 3, out_shape=[out] * 3,
                args=args)[0]
    return [r.reshape(shape) for r in res]


def _sum_devices(g):
    G, _, R, C = g.shape
    tr = _pick(R, max(16, (1 << 18) // C), 16)

    def body(g_ref, o_ref):
        acc = g_ref[0, 0].astype(F32)
        for k in range(1, N_DEV):
            acc = acc + g_ref[0, k].astype(F32)
        o_ref[0] = acc

    return _call(body, name="sum_devices", grid=(G, R // tr),
                 in_specs=[pl.BlockSpec((1, N_DEV, tr, C), lambda t, i: (t, 0, i, 0))],
                 out_specs=[pl.BlockSpec((1, tr, C), lambda t, i: (t, i, 0))],
                 out_shape=[jax.ShapeDtypeStruct((G, R, C), F32)], args=(g,))[0][0]


def _sum_devices_into(g, prev, l, L, transpose):
    _, _, R, C = g.shape
    if transpose:
        tc = _pick(C, 256, 128)
        grid, shape = (C // tc,), (L, C, R)
        in_spec = pl.BlockSpec((1, N_DEV, R, tc), lambda i: (0, 0, 0, i))
        out_spec = pl.BlockSpec((1, tc, R), lambda i: (l, i, 0))
    else:
        tr = _pick(R, max(16, (1 << 18) // C), 16)
        grid, shape = (R // tr,), (L, R, C)
        in_spec = pl.BlockSpec((1, N_DEV, tr, C), lambda i: (0, 0, i, 0))
        out_spec = pl.BlockSpec((1, tr, C), lambda i: (l, i, 0))

    def body(g_ref, *rest):
        acc = g_ref[0, 0].astype(F32)
        for k in range(1, N_DEV):
            acc = acc + g_ref[0, k].astype(F32)
        rest[-1][0] = acc.T if transpose else acc

    first = prev is None
    return pl.pallas_call(
        body, name="sum_devices_into", grid=grid, in_specs=[in_spec] if first else [in_spec, ANY],
        out_specs=out_spec, out_shape=jax.ShapeDtypeStruct(shape, F32),
        input_output_aliases={} if first else {1: 0},
        compiler_params=pltpu.CompilerParams(dimension_semantics=("arbitrary",)))(*((g,) if first else (g, prev)))


SMALL = ("ffn1_norm", "mix_norm", "sg_norm", "sg_w", "sg_b", "pool_w", "pool_scale", "na_rpb", "ffn2_norm")
GROUPS = (("gu1", ("ffn1_w_gate", "ffn1_w_up"), True), ("down1", ("ffn1_w_down",), False), ("w_in", ("w_in",), True),
          ("w_out", ("w_out",), False), ("gu2", ("ffn2_w_gate", "ffn2_w_up"), True), ("down2", ("ffn2_w_down",), False))


def kernel(x, ffn1_norm, ffn1_w_gate, ffn1_w_up, ffn1_w_down, mix_norm, w_in, sg_norm, sg_w, sg_b, pool_w, pool_scale, na_rpb, w_out, ffn2_norm, ffn2_w_gate, ffn2_w_up, ffn2_w_down, final_norm, loss_target, m_ffn1_norm, m_ffn1_w_gate, m_ffn1_w_up, m_ffn1_w_down, m_mix_norm, m_w_in, m_sg_norm, m_sg_w, m_sg_b, m_pool_w, m_pool_scale, m_na_rpb, m_w_out, m_ffn2_norm, m_ffn2_w_gate, m_ffn2_w_up, m_ffn2_w_down, m_final_norm, v_ffn1_norm, v_ffn1_w_gate, v_ffn1_w_up, v_ffn1_w_down, v_mix_norm, v_w_in, v_sg_norm, v_sg_w, v_sg_b, v_pool_w, v_pool_scale, v_na_rpb, v_w_out, v_ffn2_norm, v_ffn2_w_gate, v_ffn2_w_up, v_ffn2_w_down, v_final_norm):
    given = dict(locals())
    T, D = x.shape[1], x.shape[2]
    L = ffn1_norm.shape[0]
    assert x.shape[0] == 1 and D == SG_WIDTH + POOL_WIDTH + NA_WIDTH and w_in.shape[2] * N_DEV == Z_COLS
    assert T % (ATT_ROWS * GRID_W) == 0 and T // GRID_W >= NA_KH
    x0 = x.reshape(T, D)
    tgt = loss_target.reshape(T, D)
    members = {grp: (names, cols) for grp, names, cols in GROUPS}

    def shard(grp, l):
        names, cols = members[grp]
        return jnp.stack([(given[n][l].T if cols else given[n][l]).astype(BF16) for n in names])

    def gather(l, *grps):
        return _Carry(gathers=[shard(grp, l) for grp in grps]) if l < L else None

    def full(gathered):
        return gathered.reshape(gathered.shape[0], -1, D)

    def by_token(heads):
        return heads.transpose(1, 0, 2).reshape(T, NA_WIDTH)

    W = {"gu1": full(_exchange(gather(0, "gu1"), "gather_first")[0])}
    saved = []
    xc = x0
    h = _rmsnorm(xc, ffn1_norm[0:1])
    for l in range(L):
        s = dict(x0=xc, h1=h)
        (s["a1"], s["pq1"]), got = _ffn_gu(h, W["gu1"], gather(l, "down1", "w_in", "w_out"))
        W["down1"], W["w_in"], W["w_out"] = map(full, got)
        (xc, h), got = _mm_res_norm(s["a1"], W["down1"][0], xc, mix_norm[l:l + 1], 0.5, gather(l, "gu2"))
        W["gu2"] = full(got[0])
        s["x1"], s["h2"] = xc, h
        (z, qkv), got = _z_proj(h, W["w_in"][0], gather(l, "down2"))
        W["down2"] = full(got[0])
        s["ws"] = sg_w[l].astype(BF16)
        s["wst"] = jnp.swapaxes(sg_w[l], 1, 2).astype(BF16)
        s["bb"] = jnp.broadcast_to(sg_b[l][:, :, None], (SG_HEADS, SG_CHUNK, HEAD_DIM))
        s["gn"] = sg_norm[l:l + 1]
        s["pw"] = pool_w[l].astype(BF16)
        s["psc"] = pool_scale[l:l + 1]
        s["ecat"] = _rpb_expand(na_rpb[l])
        (att,), _ = _attn_fwd(qkv, s["ecat"])
        mix = jnp.concatenate([_sg_fwd(z, s["ws"], s["bb"], s["gn"]), _pool_fwd(z, s["pw"], s["psc"]),
                               by_token(att)], axis=1)
        s["z"], s["qkv"], s["mix"] = z, qkv, mix
        (xc, h), _ = _mm_res_norm(mix, W["w_out"][0], xc, ffn2_norm[l:l + 1], 1.0)
        s["x2"], s["h3"] = xc, h
        (s["a2"], s["pq2"]), got = _ffn_gu(h, W["gu2"], gather(l + 1, "gu1"))
        s["W"] = W
        W = {"gu1": full(got[0])} if got else {}
        gnext = ffn1_norm[l + 1:l + 2] if l + 1 < L else final_norm.reshape(1, D)
        (xc, h), _ = _mm_res_norm(s["a2"], s["W"]["down2"][0], xc, gnext, 0.5)
        saved.append(s)

    loss_row, dx, dxb, dg_final = _loss_bwd(xc, final_norm.reshape(1, D), tgt)
    loss = lax.psum(loss_row[0, 0], MESH_AXES)

    received = {grp: [[None] * len(names) for _ in range(L)] for grp, names, _ in GROUPS}
    small = {n: [None] * L for n in SMALL}

    def slots(g, *groups):
        return g.reshape(g.shape[0], N_DEV, -1, D), groups or tuple(range(g.shape[0]))

    for l in reversed(range(L)):
        s = saved[l]
        W = s["W"]
        (dgu,), _ = _mm_nt(dxb, W["down2"][0], 0.5, pq=s["pq2"])
        (g_down2,), _ = _mm_tn(s["a2"][None], dxb, 0.5)
        (dx, dxb, dgn), got = _dh_norm_bwd(dgu, W["gu2"], s["x2"], ffn2_norm[l:l + 1], dx,
                                           _Carry(scatters=[slots(g_down2)]))
        received["down2"][l][0] = got[0]
        small["ffn2_norm"][l] = dgn
        (g_gu2,), _ = _mm_tn(dgu, s["h3"], 1.0)

        (dmix,), _ = _mm_nt(dxb, W["w_out"][0], 1.0)
        (g_out,), _ = _mm_tn(s["mix"][None], dxb, 1.0)
        dzu, dzv, dws, dbb, dgn = _sg_bwd(s["z"], dmix, s["ws"], s["wst"], s["bb"], s["gn"])
        dzp, dpw, dpsc = _pool_bwd(s["z"], dmix, s["pw"], s["psc"])
        (dq, dk, dv, decat), got = _attn_bwd(s["qkv"], dmix, s["ecat"], _Carry(scatters=[slots(g_gu2, 0)]))
        received["gu2"][l][0] = got[0]
        small["sg_w"][l], small["sg_b"][l], small["sg_norm"][l] = dws, dbb[:, :, 0], dgn[0]
        small["pool_w"][l], small["pool_scale"][l] = dpw, dpsc[0]
        small["na_rpb"][l] = _rpb_collect(decat)
        dz = jnp.concatenate([dzu, dzv, dzp, by_token(dq), by_token(dk), by_token(dv)], axis=1)
        (dx, dxb, dgn), got = _dh_norm_bwd(dz[None], W["w_in"], s["x1"], mix_norm[l:l + 1], dx,
                                           _Carry(scatters=[slots(g_gu2, 1)]))
        received["gu2"][l][1] = got[0]
        small["mix_norm"][l] = dgn
        (g_in,), got = _mm_tn(dz[None], s["h2"], 1.0, _Carry(scatters=[slots(g_out)]))
        received["w_out"][l][0] = got[0]

        (dgu,), _ = _mm_nt(dxb, W["down1"][0], 0.5, pq=s["pq1"])
        (g_down1,), got = _mm_tn(s["a1"][None], dxb, 0.5, _Carry(scatters=[slots(g_in)]))
        received["w_in"][l][0] = got[0]
        (g_gu1,), got = _mm_tn(dgu, s["h1"], 1.0, _Carry(scatters=[slots(g_down1)]))
        received["down1"][l][0] = got[0]
        (dx, dxb, dgn), got = _dh_norm_bwd(dgu, W["gu1"], s["x0"], ffn1_norm[l:l + 1], dx,
                                           _Carry(scatters=[slots(g_gu1, 0), slots(g_gu1, 1)]))
        received["gu1"][l][0], received["gu1"][l][1] = got
        small["ffn1_norm"][l] = dgn

    small_shapes = {n: given[n].shape for n in SMALL}
    small_shapes["final_norm"] = final_norm.shape
    flat = [jnp.stack([jnp.reshape(g, (-1,)) for g in small[n]]).reshape(-1) for n in SMALL] + [dg_final.reshape(-1)]
    sizes = [f.shape[0] for f in flat]
    total = sum(sizes)
    padded = -(-total // 2048) * 2048
    local = jnp.concatenate(flat + [jnp.zeros((padded - total,), F32)]).reshape(1, -1, 128)
    summed = _sum_devices(_exchange(_Carry(gathers=[local]), "gather_small_grads")[0]).reshape(-1)
    grads, off = {}, 0
    for n, size in zip(list(SMALL) + ["final_norm"], sizes):
        grads[n] = summed[off:off + size].reshape(small_shapes[n])
        off += size
    for grp, names, cols in GROUPS:
        for t, n in enumerate(names):
            grads[n] = None
            for l in range(L):
                grads[n] = _sum_devices_into(received[grp][l][t], grads[n], l, L, cols)

    names = ['ffn1_norm', 'ffn1_w_gate', 'ffn1_w_up', 'ffn1_w_down', 'mix_norm', 'w_in', 'sg_norm', 'sg_w', 'sg_b',
             'pool_w', 'pool_scale', 'na_rpb', 'w_out', 'ffn2_norm', 'ffn2_w_gate', 'ffn2_w_up', 'ffn2_w_down',
             'final_norm']
    delta, new_m, new_v = {}, {}, {}
    for n in names:
        delta[n], new_m[n], new_v[n] = _adamw(given[n], grads[n], given["m_" + n], given["v_" + n])
    return (loss, dx.reshape(1, T, D), *[grads[n] for n in names], *[delta[n] for n in names],
            *[new_m[n] for n in names], *[new_v[n] for n in names])
```

```python
import functools
import math

import jax
import jax.numpy as jnp
from jax import lax
from jax.experimental import pallas as pl
from jax.experimental.pallas import tpu as pltpu

F32 = jnp.float32
BF16 = jnp.bfloat16
EPS = 1e-6
NEG = -1e30

HEAD_DIM = 128
SG_WIDTH = 512
SG_HEADS = 4
SG_CHUNK = 128
POOL_WINDOWS = (2, 4, 8, 16)
POOL_WIDTH = 512
POOL_HALO = 128
NA_WIDTH = 1024
NA_HEADS = 8
NA_KH = 8
NA_KW = 16
GRID_W = 64
Z_COLS = 2 * SG_WIDTH + POOL_WIDTH + 3 * NA_WIDTH
Q_OFF = 2 * SG_WIDTH + POOL_WIDTH
K_OFF = Q_OFF + NA_WIDTH
V_OFF = K_OFF + NA_WIDTH

ADAM_LR = 0.001
ADAM_B1 = 0.9
ADAM_B2 = 0.999
ADAM_EPS = 1e-08
ADAM_WD = 0.01
ADAM_STEP = 10

N_DEV = 8
MESH_AXES = ("x", "y", "c")
MESH = pl.DeviceIdType.MESH
ANY = pl.BlockSpec(memory_space=pl.ANY)

NT_DIMS = (((1,), (1,)), ((), ()))
TN_DIMS = (((0,), (0,)), ((), ()))

ROWS_PREF = 1024
MXU_DEPTH = 256
K_PREF = 1408


def _pick(n, pref, mult):
    best = None
    t = mult
    while t <= min(n, pref):
        if n % t == 0:
            best = t
        t += mult
    return n if best is None else best


def _gelu(x):
    return 0.5 * x * (1.0 + lax.erf(x * (1.0 / math.sqrt(2.0))))


def _gelu_grad(x):
    cdf = 0.5 * (1.0 + lax.erf(x * (1.0 / math.sqrt(2.0))))
    pdf = jnp.exp(-0.5 * x * x) * (1.0 / math.sqrt(2.0 * math.pi))
    return cdf + x * pdf


def _rms(x):
    return lax.rsqrt(jnp.mean(x * x, axis=-1, keepdims=True) + EPS)


def _norm_bwd(dh, x, g):
    r = _rms(x)
    w = dh * g
    dx = r * w - x * (r * r * r) * jnp.mean(w * x, axis=-1, keepdims=True)
    dg = jnp.sum(dh * (x * r), axis=0, keepdims=True)
    return dx, dg


def _position():
    return lax.axis_index("x"), lax.axis_index("y"), lax.axis_index("c")


def _index(p):
    return 4 * p[0] + 2 * p[1] + p[2]


class _Carry:
    def __init__(self, gathers=(), scatters=()):
        self.gathers, self.scatters = list(gathers), list(scatters)
        self.units = sum(g.shape[0] for g in self.gathers) + sum(len(groups) for _, groups in self.scatters)

    def arrays(self):
        return self.gathers + [s for s, _ in self.scatters]

    def out_shapes(self):
        return ([jax.ShapeDtypeStruct((g.shape[0], N_DEV) + g.shape[1:], g.dtype) for g in self.gathers]
                + [jax.ShapeDtypeStruct((len(groups),) + s.shape[1:], s.dtype) for s, groups in self.scatters])

    def scratch(self):
        return [pltpu.SemaphoreType.DMA((7 * self.units,)), pltpu.SemaphoreType.DMA((7 * self.units,)),
                pltpu.SemaphoreType.DMA((self.units,))]

    def _gather_copies(self, n, x_ref, out_ref, send, recv, local):
        x, y, c = _position()
        me, sibling = (x, y, c), (x, y, 1 - c)
        chips = [(1 - x, y), (x, 1 - y), (1 - x, 1 - y)]

        def copy(k, block, to, src=None):
            dst = out_ref.at[_index(block)]
            return pltpu.make_async_remote_copy(
                src_ref=dst if src is None else src, dst_ref=dst, send_sem=send.at[7 * n + k],
                recv_sem=recv.at[7 * n + k], device_id=to, device_id_type=MESH)

        return dict(
            mine=pltpu.make_async_copy(x_ref, out_ref.at[_index(me)], local.at[n]),
            first=[copy(0, me, sibling, x_ref)] + [copy(1 + j, me, (*ch, c), x_ref) for j, ch in enumerate(chips)],
            landed=[copy(1 + j, (*ch, c), me) for j, ch in enumerate(chips)],
            passed=[copy(4 + j, (*ch, c), sibling) for j, ch in enumerate(chips)],
            from_sibling=[copy(0, sibling, me)] + [copy(4 + j, (*ch, 1 - c), me) for j, ch in enumerate(chips)])

    def _scatter_copies(self, n, src_ref, out_ref, send, recv, local):
        x, y, c = _position()
        me = (x, y, c)
        sends, recvs = [], []
        for k in range(1, N_DEV):
            flip = lambda v, bit: 1 - v if bit else v
            peer = (flip(x, k & 4), flip(y, k & 2), flip(c, k & 1))
            sems = dict(send_sem=send.at[7 * n + k - 1], recv_sem=recv.at[7 * n + k - 1], device_id=peer,
                        device_id_type=MESH)
            sends.append(pltpu.make_async_remote_copy(src_ref=src_ref.at[_index(peer)], dst_ref=out_ref.at[_index(me)],
                                                      **sems))
            recvs.append(pltpu.make_async_remote_copy(src_ref=src_ref.at[_index(me)], dst_ref=out_ref.at[_index(peer)],
                                                      **sems))
        mine = pltpu.make_async_copy(src_ref.at[_index(me)], out_ref.at[_index(me)], local.at[n])
        return dict(mine=mine, sends=sends, recvs=recvs)

    def _pieces(self, ins, outs, sems):
        send, recv, local = sems
        gs, ss, unit = [], [], 0
        for n, g in enumerate(self.gathers):
            for t in range(g.shape[0]):
                gs.append(self._gather_copies(unit, ins[n].at[t], outs[n].at[t], send, recv, local))
                unit += 1
        for n, (_, groups) in enumerate(self.scatters, start=len(self.gathers)):
            for j, t in enumerate(groups):
                ss.append(self._scatter_copies(unit, ins[n].at[t], outs[n].at[j], send, recv, local))
                unit += 1
        return gs, ss

    def start(self, ins, outs, sems):
        gs, ss = self._pieces(ins, outs, sems)
        for g in gs:
            g["mine"].start()
            for cp in g["first"]:
                cp.start()
        for s in ss:
            s["mine"].start()
            for cp in s["sends"]:
                cp.start()

    def forward(self, ins, outs, sems):
        gs, _ = self._pieces(ins, outs, sems)
        for g in gs:
            for landed, passed in zip(g["landed"], g["passed"]):
                landed.wait_recv()
                passed.start()

    def finish(self, ins, outs, sems):
        gs, ss = self._pieces(ins, outs, sems)
        for g in gs:
            for cp in g["from_sibling"]:
                cp.wait_recv()
            for cp in g["first"] + g["passed"]:
                cp.wait_send()
            g["mine"].wait()
        for s in ss:
            for cp in s["recvs"]:
                cp.wait_recv()
            for cp in s["sends"]:
                cp.wait_send()
            s["mine"].wait()


def _call(body, *, name, grid, in_specs, out_specs, out_shape, args, scratch_shapes=(), carry=None):
    if carry is None or not carry.arrays():
        outs = pl.pallas_call(
            body, name=name, grid=grid, in_specs=in_specs, out_specs=out_specs, out_shape=out_shape,
            scratch_shapes=list(scratch_shapes),
            compiler_params=pltpu.CompilerParams(dimension_semantics=("arbitrary",) * len(grid)))(*args)
        return list(outs), []
    n_in, n_out, n_scr, n_car = len(in_specs), len(out_specs), len(scratch_shapes), len(carry.arrays())
    steps = math.prod(grid)
    middle = (steps * 6) // 10

    def wrapped(*refs):
        ins, refs = refs[:n_in], refs[n_in:]
        cins, refs = refs[:n_car], refs[n_car:]
        outs, refs = refs[:n_out], refs[n_out:]
        couts, refs = refs[:n_car], refs[n_car:]
        scr, sems = refs[:n_scr], refs[n_scr:]
        step = 0
        for d, size in enumerate(grid):
            step = step * size + pl.program_id(d)

        @pl.when(step == 0)
        def _():
            carry.start(cins, couts, sems)

        body(*ins, *outs, *scr)

        if carry.gathers:
            @pl.when(step == middle)
            def _():
                carry.forward(cins, couts, sems)

        @pl.when(step == steps - 1)
        def _():
            carry.finish(cins, couts, sems)

    outs = pl.pallas_call(
        wrapped, name=name + "_carry", grid=grid, in_specs=list(in_specs) + [ANY] * n_car,
        out_specs=list(out_specs) + [ANY] * n_car, out_shape=list(out_shape) + carry.out_shapes(),
        scratch_shapes=list(scratch_shapes) + carry.scratch(),
        compiler_params=pltpu.CompilerParams(dimension_semantics=("arbitrary",) * len(grid)))(*args, *carry.arrays())
    return list(outs[:n_out]), list(outs[n_out:])


def _exchange(carry, name):
    n_car = len(carry.arrays())

    def body(*refs):
        cins, couts, sems = refs[:n_car], refs[n_car:2 * n_car], refs[2 * n_car:]
        carry.start(cins, couts, sems)
        if carry.gathers:
            carry.forward(cins, couts, sems)
        carry.finish(cins, couts, sems)

    return list(pl.pallas_call(body, name=name, in_specs=[ANY] * n_car, out_specs=[ANY] * n_car,
                               out_shape=carry.out_shapes(), scratch_shapes=carry.scratch())(*carry.arrays()))


def _rmsnorm(x, g):
    T, D = x.shape
    tm = _pick(T, 512, 16)

    def body(x_ref, g_ref, o_ref):
        xv = x_ref[...]
        o_ref[...] = (xv * _rms(xv) * g_ref[...]).astype(BF16)

    return _call(body, name="rmsnorm", grid=(T // tm,),
                 in_specs=[pl.BlockSpec((tm, D), lambda i: (i, 0)), pl.BlockSpec((1, D), lambda i: (0, 0))],
                 out_specs=[pl.BlockSpec((tm, D), lambda i: (i, 0))],
                 out_shape=[jax.ShapeDtypeStruct((T, D), BF16)], args=(x, g))[0][0]


def _ffn_gu(h, wgu, carry=None):
    T, D = h.shape
    F = wgu.shape[1]
    tm = _pick(T, ROWS_PREF, 16)
    tn = _pick(F, 512, 128)

    def body(h_ref, wg_ref, wu_ref, a_ref, pq_ref):
        hv = h_ref[...]
        g = lax.dot_general(hv, wg_ref[0], NT_DIMS, preferred_element_type=F32)
        u = lax.dot_general(hv, wu_ref[0], NT_DIMS, preferred_element_type=F32)
        sg = jax.nn.sigmoid(g)
        q = g * sg
        a_ref[...] = (q * u).astype(BF16)
        pq_ref[0] = (u * (sg * (1.0 + g * (1.0 - sg)))).astype(BF16)
        pq_ref[1] = q.astype(BF16)

    return _call(body, name="ffn_gu", grid=(T // tm, F // tn),
                 in_specs=[pl.BlockSpec((tm, D), lambda i, j: (i, 0)),
                           pl.BlockSpec((1, tn, D), lambda i, j: (0, j, 0)),
                           pl.BlockSpec((1, tn, D), lambda i, j: (1, j, 0))],
                 out_specs=[pl.BlockSpec((tm, tn), lambda i, j: (i, j)),
                            pl.BlockSpec((2, tm, tn), lambda i, j: (0, i, j))],
                 out_shape=[jax.ShapeDtypeStruct((T, F), BF16), jax.ShapeDtypeStruct((2, T, F), BF16)],
                 args=(h, wgu, wgu), carry=carry)


def _mm_nt(a, w, scale, pq=None, carry=None):
    T, K = a.shape
    N = w.shape[0]
    tm = _pick(T, ROWS_PREF, 16)
    tn = _pick(N, 512, 128)

    def body(*refs):
        a_ref, w_ref = refs[:2]
        d = lax.dot_general(a_ref[...], w_ref[...], NT_DIMS, preferred_element_type=F32)
        if scale != 1.0:
            d = d * scale
        if pq is None:
            refs[2][...] = d.astype(BF16)
        else:
            pq_ref, o_ref = refs[2:]
            o_ref[0] = (d * pq_ref[0].astype(F32)).astype(BF16)
            o_ref[1] = (d * pq_ref[1].astype(F32)).astype(BF16)

    in_specs = [pl.BlockSpec((tm, K), lambda i, j: (i, 0)), pl.BlockSpec((tn, K), lambda i, j: (j, 0))]
    if pq is None:
        args, out_spec, out_shape = (a, w), pl.BlockSpec((tm, tn), lambda i, j: (i, j)), (T, N)
    else:
        in_specs.append(pl.BlockSpec((2, tm, tn), lambda i, j: (0, i, j)))
        args, out_spec, out_shape = (a, w, pq), pl.BlockSpec((2, tm, tn), lambda i, j: (0, i, j)), (2, T, N)
    return _call(body, name="mm_nt" if pq is None else "ffn_da", grid=(T // tm, N // tn),
                 in_specs=in_specs, out_specs=[out_spec], out_shape=[jax.ShapeDtypeStruct(out_shape, BF16)],
                 args=args, carry=carry)


def _z_proj(h, wt, carry=None):
    T, K = h.shape
    tm = _pick(T, ROWS_PREF, 16)
    tn = 4 * HEAD_DIM
    flat = Q_OFF // tn

    def body(h_ref, w_ref, z_ref, qkv_ref):
        j = pl.program_id(1)
        zv = lax.dot_general(h_ref[...], w_ref[...], NT_DIMS, preferred_element_type=F32).astype(BF16)

        @pl.when(j < flat)
        def _():
            z_ref[...] = zv

        @pl.when(j >= flat)
        def _():
            for c in range(tn // HEAD_DIM):
                qkv_ref[c] = zv[:, c * HEAD_DIM:(c + 1) * HEAD_DIM]

    return _call(body, name="z_proj", grid=(T // tm, Z_COLS // tn),
                 in_specs=[pl.BlockSpec((tm, K), lambda i, j: (i, 0)), pl.BlockSpec((tn, K), lambda i, j: (j, 0))],
                 out_specs=[pl.BlockSpec((tm, tn), lambda i, j: (i, jnp.minimum(j, flat - 1))),
                            pl.BlockSpec((tn // HEAD_DIM, tm, HEAD_DIM), lambda i, j: (jnp.maximum(j - flat, 0), i, 0))],
                 out_shape=[jax.ShapeDtypeStruct((T, Q_OFF), BF16),
                            jax.ShapeDtypeStruct((3 * NA_HEADS, T, HEAD_DIM), BF16)],
                 args=(h, wt), carry=carry)


def _mm_res_norm(a, w, x, gnext, scale, carry=None):
    T, K = a.shape
    D = w.shape[1]
    tm = _pick(T, 512, 16)
    tk = _pick(K, K_PREF, 128)
    nk = K // tk

    def body(a_ref, w_ref, x_ref, g_ref, xo_ref, ho_ref, acc):
        k = pl.program_id(1)

        @pl.when(k == 0)
        def _():
            acc[...] = jnp.zeros_like(acc)

        acc[...] += jnp.dot(a_ref[...], w_ref[...], preferred_element_type=F32)

        @pl.when(k == nk - 1)
        def _():
            xn = x_ref[...] + scale * acc[...]
            xo_ref[...] = xn
            ho_ref[...] = (xn * _rms(xn) * g_ref[...]).astype(BF16)

    row = pl.BlockSpec((tm, D), lambda i, k: (i, 0))
    return _call(body, name="mm_res_norm", grid=(T // tm, nk),
                 in_specs=[pl.BlockSpec((tm, tk), lambda i, k: (i, k)), pl.BlockSpec((tk, D), lambda i, k: (k, 0)),
                           row, pl.BlockSpec((1, D), lambda i, k: (0, 0))],
                 out_specs=[row, row],
                 out_shape=[jax.ShapeDtypeStruct((T, D), F32), jax.ShapeDtypeStruct((T, D), BF16)],
                 scratch_shapes=[pltpu.VMEM((tm, D), F32)], args=(a, w, x, gnext), carry=carry)


def _mm_tn(a, b, scale, carry=None):
    G, T, M = a.shape
    N = b.shape[1]
    tm = _pick(M, 512, 128)
    tk = _pick(T, 2048, 16)
    nk = T // tk

    def body(a_ref, b_ref, o_ref, acc):
        k = pl.program_id(2)
        p = lax.dot_general(a_ref[0], b_ref[...], TN_DIMS, preferred_element_type=F32)

        @pl.when(k == 0)
        def _():
            acc[...] = p

        @pl.when(k > 0)
        def _():
            acc[...] += p

        @pl.when(k == nk - 1)
        def _():
            o_ref[0] = (acc[...] * scale).astype(BF16)

    return _call(body, name="mm_tn", grid=(G, M // tm, nk),
                 in_specs=[pl.BlockSpec((1, tk, tm), lambda g, i, k: (g, k, i)),
                           pl.BlockSpec((tk, N), lambda g, i, k: (k, 0))],
                 out_specs=[pl.BlockSpec((1, tm, N), lambda g, i, k: (g, i, 0))],
                 out_shape=[jax.ShapeDtypeStruct((G, M, N), BF16)],
                 scratch_shapes=[pltpu.VMEM((tm, N), F32)], args=(a, b), carry=carry)


def _dh_norm_bwd(d, wt, x, g, dres, carry=None):
    T, D = x.shape
    G, _, K = d.shape
    tm = _pick(T, 512, 16)
    tk = _pick(K, 2 * MXU_DEPTH, MXU_DEPTH)
    nk = K // tk

    def body(*refs):
        dw_refs = refs[:2 * G]
        x_ref, g_ref, dres_ref, dx_ref, dxb_ref, dg_ref, acc = refs[2 * G:]
        i = pl.program_id(0)
        k = pl.program_id(1)

        @pl.when(k == 0)
        def _():
            acc[...] = jnp.zeros_like(acc)

        for n in range(G):
            acc[...] += jnp.dot(dw_refs[2 * n][0], dw_refs[2 * n + 1][0], preferred_element_type=F32)

        @pl.when(k == nk - 1)
        def _():
            dxn, dgp = _norm_bwd(acc[...], x_ref[...], g_ref[...])
            dxv = dres_ref[...] + dxn
            dx_ref[...] = dxv
            dxb_ref[...] = dxv.astype(BF16)

            @pl.when(i == 0)
            def _():
                dg_ref[...] = dgp

            @pl.when(i > 0)
            def _():
                dg_ref[...] += dgp

    row = pl.BlockSpec((tm, D), lambda i, k: (i, 0))
    vec = pl.BlockSpec((1, D), lambda i, k: (0, 0))
    pairs = []
    for n in range(G):
        pairs += [pl.BlockSpec((1, tm, tk), lambda i, k, n=n: (n, i, k)),
                  pl.BlockSpec((1, tk, D), lambda i, k, n=n: (n, k, 0))]
    return _call(body, name="dh_norm_bwd", grid=(T // tm, nk),
                 in_specs=pairs + [row, vec, row], out_specs=[row, row, vec],
                 out_shape=[jax.ShapeDtypeStruct((T, D), F32), jax.ShapeDtypeStruct((T, D), BF16),
                            jax.ShapeDtypeStruct((1, D), F32)],
                 scratch_shapes=[pltpu.VMEM((tm, D), F32)], args=(*[d, wt] * G, x, g, dres), carry=carry)


def _loss_bwd(x, g, tgt):
    T, D = x.shape
    tm = _pick(T, 512, 16)

    def body(x_ref, g_ref, t_ref, loss_ref, dx_ref, dxb_ref, dg_ref):
        i = pl.program_id(0)
        xv = x_ref[...]
        gv = g_ref[...]
        e = xv * _rms(xv) * gv - t_ref[...]
        part = jnp.sum(jnp.sum(e * e, axis=-1, keepdims=True), axis=0, keepdims=True) * (0.5 / D)
        dxn, dgp = _norm_bwd(e * (1.0 / D), xv, gv)
        dx_ref[...] = dxn
        dxb_ref[...] = dxn.astype(BF16)

        @pl.when(i == 0)
        def _():
            loss_ref[...] = jnp.broadcast_to(part, loss_ref.shape)
            dg_ref[...] = dgp

        @pl.when(i > 0)
        def _():
            loss_ref[...] += jnp.broadcast_to(part, loss_ref.shape)
            dg_ref[...] += dgp

    row = pl.BlockSpec((tm, D), lambda i: (i, 0))
    vec = pl.BlockSpec((1, D), lambda i: (0, 0))
    return _call(body, name="loss_bwd", grid=(T // tm,), in_specs=[row, vec, row],
                 out_specs=[pl.BlockSpec((1, 128), lambda i: (0, 0)), row, row, vec],
                 out_shape=[jax.ShapeDtypeStruct((1, 128), F32), jax.ShapeDtypeStruct((T, D), F32),
                            jax.ShapeDtypeStruct((T, D), BF16), jax.ShapeDtypeStruct((1, D), F32)],
                 args=(x, g, tgt))[0]


def _sg_fwd(z, ws, bb, gn):
    T = z.shape[0]

    def body(zu_ref, zv_ref, ws_ref, bb_ref, gn_ref, a_ref):
        for h in range(SG_HEADS):
            sl = slice(h * HEAD_DIM, (h + 1) * HEAD_DIM)
            gv = _gelu(zv_ref[:, sl].astype(F32))
            vn = (gv * _rms(gv) * gn_ref[:, sl]).astype(BF16)
            mixed = jnp.dot(ws_ref[h], vn, preferred_element_type=F32) + bb_ref[h]
            a_ref[:, sl] = (_gelu(zu_ref[:, sl].astype(F32)) * mixed).astype(BF16)

    full = lambda shape: pl.BlockSpec(shape, lambda n: (0,) * len(shape))
    return _call(body, name="sg_fwd", grid=(T // SG_CHUNK,),
                 in_specs=[pl.BlockSpec((SG_CHUNK, SG_WIDTH), lambda n: (n, 0)),
                           pl.BlockSpec((SG_CHUNK, SG_WIDTH), lambda n: (n, 1)),
                           full((SG_HEADS, SG_CHUNK, SG_CHUNK)), full((SG_HEADS, SG_CHUNK, HEAD_DIM)),
                           full((1, SG_WIDTH))],
                 out_specs=[pl.BlockSpec((SG_CHUNK, SG_WIDTH), lambda n: (n, 0))],
                 out_shape=[jax.ShapeDtypeStruct((T, SG_WIDTH), BF16)], args=(z, z, ws, bb, gn))[0][0]


def _sg_bwd(z, dmix, ws, wst, bb, gn):
    T = z.shape[0]

    def body(zu_ref, zv_ref, da_ref, ws_ref, wst_ref, bb_ref, gn_ref, dzu_ref, dzv_ref, dws_ref, dbb_ref, dgn_ref):
        n = pl.program_id(0)

        @pl.when(n == 0)
        def _():
            dws_ref[...] = jnp.zeros_like(dws_ref)
            dbb_ref[...] = jnp.zeros_like(dbb_ref)
            dgn_ref[...] = jnp.zeros_like(dgn_ref)

        for h in range(SG_HEADS):
            sl = slice(h * HEAD_DIM, (h + 1) * HEAD_DIM)
            u = zu_ref[:, sl].astype(F32)
            v = zv_ref[:, sl].astype(F32)
            da = da_ref[:, sl].astype(F32)
            gain = gn_ref[:, sl]
            gv = _gelu(v)
            r = _rms(gv)
            vn = (gv * r * gain).astype(BF16)
            mixed = jnp.dot(ws_ref[h], vn, preferred_element_type=F32) + bb_ref[h]
            dmixed = da * _gelu(u)
            dzu_ref[:, sl] = (da * mixed * _gelu_grad(u)).astype(BF16)
            dmb = dmixed.astype(BF16)
            dws_ref[h] += lax.dot_general(dmb, vn, NT_DIMS, preferred_element_type=F32)
            dbb_ref[h] += jnp.broadcast_to(jnp.sum(dmixed, axis=-1, keepdims=True), (SG_CHUNK, HEAD_DIM))
            dvn = jnp.dot(wst_ref[h], dmb, preferred_element_type=F32)
            dgv, dg = _norm_bwd(dvn, gv, gain)
            dgn_ref[:, sl] += dg
            dzv_ref[:, sl] = (dgv * _gelu_grad(v)).astype(BF16)

    full = lambda shape: pl.BlockSpec(shape, lambda n: (0,) * len(shape))
    wspec = full((SG_HEADS, SG_CHUNK, SG_CHUNK))
    tile = lambda c: pl.BlockSpec((SG_CHUNK, SG_WIDTH), lambda n: (n, c))
    return _call(body, name="sg_bwd", grid=(T // SG_CHUNK,),
                 in_specs=[tile(0), tile(1), tile(0), wspec, wspec, full((SG_HEADS, SG_CHUNK, HEAD_DIM)),
                           full((1, SG_WIDTH))],
                 out_specs=[tile(0), tile(0), wspec, full((SG_HEADS, SG_CHUNK, HEAD_DIM)), full((1, SG_WIDTH))],
                 out_shape=[jax.ShapeDtypeStruct((T, SG_WIDTH), BF16), jax.ShapeDtypeStruct((T, SG_WIDTH), BF16),
                            jax.ShapeDtypeStruct((SG_HEADS, SG_CHUNK, SG_CHUNK), F32),
                            jax.ShapeDtypeStruct((SG_HEADS, SG_CHUNK, HEAD_DIM), F32),
                            jax.ShapeDtypeStruct((1, SG_WIDTH), F32)], args=(z, z, dmix, ws, wst, bb, gn))[0]


def _pool_specs(T, tp, col):
    step = tp // POOL_HALO
    last = T // POOL_HALO - 1
    return [pl.BlockSpec((POOL_HALO, POOL_WIDTH), lambda i: (jnp.maximum(i * step - 1, 0), col)),
            pl.BlockSpec((tp, POOL_WIDTH), lambda i: (i, col)),
            pl.BlockSpec((POOL_HALO, POOL_WIDTH), lambda i: (jnp.minimum((i + 1) * step, last), col))]


def _pool_band(i, tp, T, win):
    ext = tp + 2 * POOL_HALO
    t = i * tp + lax.broadcasted_iota(jnp.int32, (tp, ext), 0)
    s = i * tp - POOL_HALO + lax.broadcasted_iota(jnp.int32, (tp, ext), 1)
    band = (s >= jnp.maximum(t - win // 2, 0)) & (s < jnp.minimum(t + win // 2, T))
    t1 = i * tp + lax.broadcasted_iota(jnp.int32, (tp, 1), 0)
    cnt = (jnp.minimum(t1 + win // 2, T) - jnp.maximum(t1 - win // 2, 0)).astype(F32)
    return band.astype(BF16), cnt


def _pool_fwd(z, pw, psc):
    T = z.shape[0]
    tp = _pick(T, 256, POOL_HALO)

    def body(pp_ref, pc_ref, pn_ref, w_ref, sc_ref, o_ref):
        i = pl.program_id(0)
        halo = jnp.concatenate([pp_ref[...], pc_ref[...], pn_ref[...]], axis=0)
        for g, win in enumerate(POOL_WINDOWS):
            sl = slice(g * HEAD_DIM, (g + 1) * HEAD_DIM)
            band, cnt = _pool_band(i, tp, T, win)
            ssum = jnp.dot(band, halo[:, sl], preferred_element_type=F32)
            d = ssum / cnt - pc_ref[:, sl].astype(F32)
            y = jnp.dot(d.astype(BF16), w_ref[g], preferred_element_type=F32) * sc_ref[:, sl]
            o_ref[:, sl] = y.astype(BF16)

    full = lambda shape: pl.BlockSpec(shape, lambda i: (0,) * len(shape))
    return _call(body, name="pool_fwd", grid=(T // tp,),
                 in_specs=_pool_specs(T, tp, 2) + [full((4, HEAD_DIM, HEAD_DIM)), full((1, POOL_WIDTH))],
                 out_specs=[pl.BlockSpec((tp, POOL_WIDTH), lambda i: (i, 0))],
                 out_shape=[jax.ShapeDtypeStruct((T, POOL_WIDTH), BF16)], args=(z, z, z, pw, psc))[0][0]


def _pool_bwd(z, dmix, pw, psc):
    T = z.shape[0]
    tp = _pick(T, 256, POOL_HALO)
    ext = tp + 2 * POOL_HALO

    def body(pp_ref, pc_ref, pn_ref, dp_ref, dc_ref, dn_ref, w_ref, sc_ref, dz_ref, dw_ref, dsc_ref):
        i = pl.program_id(0)

        @pl.when(i == 0)
        def _():
            dw_ref[...] = jnp.zeros_like(dw_ref)
            dsc_ref[...] = jnp.zeros_like(dsc_ref)

        halo = jnp.concatenate([pp_ref[...], pc_ref[...], pn_ref[...]], axis=0)
        dy_halo = jnp.concatenate([dp_ref[...], dc_ref[...], dn_ref[...]], axis=0)
        th = i * tp - POOL_HALO + lax.broadcasted_iota(jnp.int32, (ext, 1), 0)
        inside = (th >= 0) & (th < T)
        s2 = i * tp + lax.broadcasted_iota(jnp.int32, (tp, ext), 0)
        t2 = i * tp - POOL_HALO + lax.broadcasted_iota(jnp.int32, (tp, ext), 1)
        for g, win in enumerate(POOL_WINDOWS):
            sl = slice(g * HEAD_DIM, (g + 1) * HEAD_DIM)
            sc = sc_ref[:, sl]
            band, cnt = _pool_band(i, tp, T, win)
            ssum = jnp.dot(band, halo[:, sl], preferred_element_type=F32)
            db = (ssum / cnt - pc_ref[:, sl].astype(F32)).astype(BF16)
            yraw = jnp.dot(db, w_ref[g], preferred_element_type=F32)
            dyc = dc_ref[:, sl].astype(F32)
            dsc_ref[:, sl] += jnp.sum(dyc * yraw, axis=0, keepdims=True)
            dw_ref[g] += lax.dot_general(db, (dyc * sc).astype(BF16), TN_DIMS, preferred_element_type=F32)
            dd = lax.dot_general((dy_halo[:, sl].astype(F32) * sc).astype(BF16), w_ref[g], NT_DIMS,
                                 preferred_element_type=F32)
            cnt_h = (jnp.minimum(th + win // 2, T) - jnp.maximum(th - win // 2, 0)).astype(F32)
            ddc = jnp.where(inside, dd / jnp.maximum(cnt_h, 1.0), 0.0)
            hi = ddc.astype(BF16)
            lo = (ddc - hi.astype(F32)).astype(BF16)
            band_t = ((s2 >= jnp.maximum(t2 - win // 2, 0)) & (s2 < jnp.minimum(t2 + win // 2, T))).astype(BF16)
            dpool = (jnp.dot(band_t, hi, preferred_element_type=F32) + jnp.dot(band_t, lo, preferred_element_type=F32)
                     - dd[POOL_HALO:POOL_HALO + tp])
            dz_ref[:, sl] = dpool.astype(BF16)

    full = lambda shape: pl.BlockSpec(shape, lambda i: (0,) * len(shape))
    return _call(body, name="pool_bwd", grid=(T // tp,),
                 in_specs=_pool_specs(T, tp, 2) + _pool_specs(T, tp, 1)
                 + [full((4, HEAD_DIM, HEAD_DIM)), full((1, POOL_WIDTH))],
                 out_specs=[pl.BlockSpec((tp, POOL_WIDTH), lambda i: (i, 0)), full((4, HEAD_DIM, HEAD_DIM)),
                            full((1, POOL_WIDTH))],
                 out_shape=[jax.ShapeDtypeStruct((T, POOL_WIDTH), BF16),
                            jax.ShapeDtypeStruct((4, HEAD_DIM, HEAD_DIM), F32),
                            jax.ShapeDtypeStruct((1, POOL_WIDTH), F32)], args=(z, z, z, dmix, dmix, dmix, pw, psc))[0]


ATT_ROWS = 8
WIN_KEYS = NA_KH * GRID_W


def _col_mask():
    q = lax.broadcasted_iota(jnp.int32, (GRID_W, WIN_KEYS), 0)
    k = lax.broadcasted_iota(jnp.int32, (GRID_W, WIN_KEYS), 1) & (GRID_W - 1)
    start = jnp.clip(q - NA_KW // 2, 0, GRID_W - NA_KW)
    return (k >= start) & (k < start + NA_KW)


def _softmax(s, bias, mask):
    s = jnp.where(mask, s * (HEAD_DIM ** -0.5) + bias, NEG)
    p = jnp.exp(s - jnp.max(s, axis=-1, keepdims=True))
    return p / jnp.sum(p, axis=-1, keepdims=True)


def _attn_window(step, a, rows):
    r = step * ATT_ROWS + a
    sr = jnp.clip(r - NA_KH // 2, 0, rows - NA_KH)
    return pl.ds(pl.multiple_of(sr * GRID_W, GRID_W), WIN_KEYS), sr - r + NA_KH - 1


def _head_specs(T, blk):
    whole = lambda first: pl.BlockSpec((1, T, HEAD_DIM), lambda h, s: (first + h, 0, 0))
    return [pl.BlockSpec((1, blk, HEAD_DIM), lambda h, s: (h, s, 0)), whole(NA_HEADS), whole(2 * NA_HEADS)]


def _attn_fwd(qkv, ecat, carry=None):
    T = qkv.shape[1]
    rows = T // GRID_W
    blk = ATT_ROWS * GRID_W

    def body(q_ref, k_ref, v_ref, e_ref, o_ref, s_scr, p_scr):
        step = pl.program_id(1)
        mask = _col_mask()
        wins = [_attn_window(step, a, rows) for a in range(ATT_ROWS)]
        qs = [slice(a * GRID_W, (a + 1) * GRID_W) for a in range(ATT_ROWS)]
        for a, (win, _) in enumerate(wins):
            s_scr[a] = lax.dot_general(q_ref[0, qs[a], :], k_ref[0, win, :], NT_DIMS, preferred_element_type=F32)
        for a, (_, dr0) in enumerate(wins):
            p_scr[a] = _softmax(s_scr[a], e_ref[0, dr0], mask).astype(BF16)
        for a, (win, _) in enumerate(wins):
            o_ref[0, qs[a], :] = jnp.dot(p_scr[a], v_ref[0, win, :], preferred_element_type=F32).astype(BF16)

    return _call(body, name="attn_fwd", grid=(NA_HEADS, rows // ATT_ROWS),
                 in_specs=_head_specs(T, blk)
                 + [pl.BlockSpec((1, NA_KH, GRID_W, WIN_KEYS), lambda h, s: (h, 0, 0, 0))],
                 out_specs=[pl.BlockSpec((1, blk, HEAD_DIM), lambda h, s: (h, s, 0))],
                 out_shape=[jax.ShapeDtypeStruct((NA_HEADS, T, HEAD_DIM), BF16)],
                 scratch_shapes=[pltpu.VMEM((ATT_ROWS, GRID_W, WIN_KEYS), F32),
                                 pltpu.VMEM((ATT_ROWS, GRID_W, WIN_KEYS), BF16)],
                 args=(qkv, qkv, qkv, ecat), carry=carry)


def _attn_bwd(qkv, dmix, ecat, carry=None):
    T = qkv.shape[1]
    rows = T // GRID_W
    blk = ATT_ROWS * GRID_W
    nstep = rows // ATT_ROWS

    def body(q_ref, k_ref, v_ref, do_ref, e_ref, dq_ref, dk_ref, dv_ref, de_ref, dk_acc, dv_acc, s_scr, dp_scr,
             p_scr, ds_scr):
        step = pl.program_id(1)
        mask = _col_mask()

        @pl.when(step == 0)
        def _():
            dk_acc[...] = jnp.zeros_like(dk_acc)
            dv_acc[...] = jnp.zeros_like(dv_acc)
            de_ref[...] = jnp.zeros_like(de_ref)

        wins = [_attn_window(step, a, rows) for a in range(ATT_ROWS)]
        qs = [slice(a * GRID_W, (a + 1) * GRID_W) for a in range(ATT_ROWS)]
        for a, (win, _) in enumerate(wins):
            s_scr[a] = lax.dot_general(q_ref[0, qs[a], :], k_ref[0, win, :], NT_DIMS, preferred_element_type=F32)
            dp_scr[a] = lax.dot_general(do_ref[qs[a], :], v_ref[0, win, :], NT_DIMS, preferred_element_type=F32)
        for a, (_, dr0) in enumerate(wins):
            pr = _softmax(s_scr[a], e_ref[0, dr0], mask)
            dp = dp_scr[a]
            ds = pr * (dp - jnp.sum(dp * pr, axis=-1, keepdims=True))
            de_ref[0, dr0] += ds
            p_scr[a] = pr.astype(BF16)
            ds_scr[a] = (ds * (HEAD_DIM ** -0.5)).astype(BF16)
        for a, (win, _) in enumerate(wins):
            dq_ref[0, qs[a], :] = jnp.dot(ds_scr[a], k_ref[0, win, :], preferred_element_type=F32).astype(BF16)
            dv_acc[win, :] += lax.dot_general(p_scr[a], do_ref[qs[a], :], TN_DIMS, preferred_element_type=F32)
            dk_acc[win, :] += lax.dot_general(ds_scr[a], q_ref[0, qs[a], :], TN_DIMS, preferred_element_type=F32)

        @pl.when(step == nstep - 1)
        def _():
            dk_ref[0] = dk_acc[...].astype(BF16)
            dv_ref[0] = dv_acc[...].astype(BF16)

    whole = pl.BlockSpec((1, T, HEAD_DIM), lambda h, s: (h, 0, 0))
    e_spec = pl.BlockSpec((1, NA_KH, GRID_W, WIN_KEYS), lambda h, s: (h, 0, 0, 0))
    out = jax.ShapeDtypeStruct((NA_HEADS, T, HEAD_DIM), BF16)
    stage = lambda dtype: pltpu.VMEM((ATT_ROWS, GRID_W, WIN_KEYS), dtype)
    return _call(body, name="attn_bwd", grid=(NA_HEADS, nstep),
                 in_specs=_head_specs(T, blk)
                 + [pl.BlockSpec((blk, HEAD_DIM), lambda h, s: (s, (SG_WIDTH + POOL_WIDTH) // HEAD_DIM + h)), e_spec],
                 out_specs=[pl.BlockSpec((1, blk, HEAD_DIM), lambda h, s: (h, s, 0)), whole, whole, e_spec],
                 out_shape=[out, out, out, jax.ShapeDtypeStruct((NA_HEADS, NA_KH, GRID_W, WIN_KEYS), F32)],
                 scratch_shapes=[pltpu.VMEM((T, HEAD_DIM), F32), pltpu.VMEM((T, HEAD_DIM), F32),
                                 stage(F32), stage(F32), stage(BF16), stage(BF16)],
                 args=(qkv, qkv, qkv, dmix, ecat), carry=carry)


def _rpb_tables():
    col = jnp.arange(GRID_W)
    dc = jnp.clip(col[None, :] - col[:, None] + NA_KW - 1, 0, 2 * NA_KW - 2)
    by_col = (dc[None] == jnp.arange(2 * NA_KW - 1)[:, None, None]).astype(F32)
    d, j = jnp.arange(NA_KH)[:, None], jnp.arange(NA_KH)[None, :]
    by_row = (jnp.arange(2 * NA_KH - 1)[:, None, None] == (d + j)[None]).astype(F32)
    return by_col, by_row


def _rpb_expand(rpb):
    by_col, by_row = _rpb_tables()
    e = jnp.einsum("hrc,cqk->hrqk", rpb, by_col, precision=lax.Precision.HIGHEST)
    ecat = jnp.einsum("hrqk,rdj->hdqjk", e, by_row, precision=lax.Precision.HIGHEST)
    return ecat.reshape(NA_HEADS, NA_KH, GRID_W, WIN_KEYS)


def _rpb_collect(decat):
    by_col, by_row = _rpb_tables()
    de = jnp.einsum("hdqjk,rdj->hrqk", decat.reshape(NA_HEADS, NA_KH, GRID_W, NA_KH, GRID_W), by_row,
                    precision=lax.Precision.HIGHEST)
    return jnp.einsum("hrqk,cqk->hrc", de, by_col, precision=lax.Precision.HIGHEST)


def _adamw(w, g, m, v):
    shape = w.shape
    C = shape[-1]
    R = w.size // C
    tr = _pick(R, max(8, (1 << 18) // C), 8)
    args = [a.reshape(R, C) for a in (w, g, m, v)]

    def body(w_ref, g_ref, m_ref, v_ref, d_ref, mo_ref, vo_ref):
        gv = g_ref[...]
        mn = ADAM_B1 * m_ref[...] + (1.0 - ADAM_B1) * gv
        vn = ADAM_B2 * v_ref[...] + (1.0 - ADAM_B2) * (gv * gv)
        m_hat = mn / (1.0 - ADAM_B1 ** ADAM_STEP)
        v_hat = vn / (1.0 - ADAM_B2 ** ADAM_STEP)
        d_ref[...] = -ADAM_LR * (m_hat / (jnp.sqrt(v_hat) + ADAM_EPS) + ADAM_WD * w_ref[...])
        mo_ref[...] = mn
        vo_ref[...] = vn

    spec = pl.BlockSpec((tr, C), lambda i: (i, 0))
    out = jax.ShapeDtypeStruct((R, C), F32)
    res = _call(body, name="adamw", grid=(R // tr,), in_specs=[spec] * 4, out_specs=[spec] * 3, out_shape=[out] * 3,
                args=args)[0]
    return [r.reshape(shape) for r in res]


def _sum_devices(g):
    G, _, R, C = g.shape
    tr = _pick(R, max(16, (1 << 18) // C), 16)

    def body(g_ref, o_ref):
        acc = g_ref[0, 0].astype(F32)
        for k in range(1, N_DEV):
            acc = acc + g_ref[0, k].astype(F32)
        o_ref[0] = acc

    return _call(body, name="sum_devices", grid=(G, R // tr),
                 in_specs=[pl.BlockSpec((1, N_DEV, tr, C), lambda t, i: (t, 0, i, 0))],
                 out_specs=[pl.BlockSpec((1, tr, C), lambda t, i: (t, i, 0))],
                 out_shape=[jax.ShapeDtypeStruct((G, R, C), F32)], args=(g,))[0][0]


def _sum_devices_into(g, prev, l, L, transpose):
    _, _, R, C = g.shape
    if transpose:
        tc = _pick(C, 256, 128)
        grid, shape = (C // tc,), (L, C, R)
        in_spec = pl.BlockSpec((1, N_DEV, R, tc), lambda i: (0, 0, 0, i))
        out_spec = pl.BlockSpec((1, tc, R), lambda i: (l, i, 0))
    else:
        tr = _pick(R, max(16, (1 << 18) // C), 16)
        grid, shape = (R // tr,), (L, R, C)
        in_spec = pl.BlockSpec((1, N_DEV, tr, C), lambda i: (0, 0, i, 0))
        out_spec = pl.BlockSpec((1, tr, C), lambda i: (l, i, 0))

    def body(g_ref, *rest):
        acc = g_ref[0, 0].astype(F32)
        for k in range(1, N_DEV):
            acc = acc + g_ref[0, k].astype(F32)
        rest[-1][0] = acc.T if transpose else acc

    first = prev is None
    return pl.pallas_call(
        body, name="sum_devices_into", grid=grid, in_specs=[in_spec] if first else [in_spec, ANY],
        out_specs=out_spec, out_shape=jax.ShapeDtypeStruct(shape, F32),
        input_output_aliases={} if first else {1: 0},
        compiler_params=pltpu.CompilerParams(dimension_semantics=("arbitrary",)))(*((g,) if first else (g, prev)))


SMALL = ("ffn1_norm", "mix_norm", "sg_norm", "sg_w", "sg_b", "pool_w", "pool_scale", "na_rpb", "ffn2_norm")
GROUPS = (("gu1", ("ffn1_w_gate", "ffn1_w_up"), True), ("down1", ("ffn1_w_down",), False), ("w_in", ("w_in",), True),
          ("w_out", ("w_out",), False), ("gu2", ("ffn2_w_gate", "ffn2_w_up"), True), ("down2", ("ffn2_w_down",), False))


def kernel(x, ffn1_norm, ffn1_w_gate, ffn1_w_up, ffn1_w_down, mix_norm, w_in, sg_norm, sg_w, sg_b, pool_w, pool_scale, na_rpb, w_out, ffn2_norm, ffn2_w_gate, ffn2_w_up, ffn2_w_down, final_norm, loss_target, m_ffn1_norm, m_ffn1_w_gate, m_ffn1_w_up, m_ffn1_w_down, m_mix_norm, m_w_in, m_sg_norm, m_sg_w, m_sg_b, m_pool_w, m_pool_scale, m_na_rpb, m_w_out, m_ffn2_norm, m_ffn2_w_gate, m_ffn2_w_up, m_ffn2_w_down, m_final_norm, v_ffn1_norm, v_ffn1_w_gate, v_ffn1_w_up, v_ffn1_w_down, v_mix_norm, v_w_in, v_sg_norm, v_sg_w, v_sg_b, v_pool_w, v_pool_scale, v_na_rpb, v_w_out, v_ffn2_norm, v_ffn2_w_gate, v_ffn2_w_up, v_ffn2_w_down, v_final_norm):
    given = dict(locals())
    T, D = x.shape[1], x.shape[2]
    L = ffn1_norm.shape[0]
    assert x.shape[0] == 1 and D == SG_WIDTH + POOL_WIDTH + NA_WIDTH and w_in.shape[2] * N_DEV == Z_COLS
    assert T % (ATT_ROWS * GRID_W) == 0 and T // GRID_W >= NA_KH
    x0 = x.reshape(T, D)
    tgt = loss_target.reshape(T, D)
    members = {grp: (names, cols) for grp, names, cols in GROUPS}

    def shard(grp, l):
        names, cols = members[grp]
        return jnp.stack([(given[n][l].T if cols else given[n][l]).astype(BF16) for n in names])

    def gather(l, *grps):
        return _Carry(gathers=[shard(grp, l) for grp in grps]) if l < L else None

    def full(gathered):
        return gathered.reshape(gathered.shape[0], -1, D)

    def by_token(heads):
        return heads.transpose(1, 0, 2).reshape(T, NA_WIDTH)

    W = {"gu1": full(_exchange(gather(0, "gu1"), "gather_first")[0])}
    saved = []
    xc = x0
    h = _rmsnorm(xc, ffn1_norm[0:1])
    for l in range(L):
        s = dict(x0=xc, h1=h)
        (s["a1"], s["pq1"]), got = _ffn_gu(h, W["gu1"], gather(l, "down1", "w_in", "w_out"))
        W["down1"], W["w_in"], W["w_out"] = map(full, got)
        (xc, h), got = _mm_res_norm(s["a1"], W["down1"][0], xc, mix_norm[l:l + 1], 0.5, gather(l, "gu2"))
        W["gu2"] = full(got[0])
        s["x1"], s["h2"] = xc, h
        (z, qkv), got = _z_proj(h, W["w_in"][0], gather(l, "down2"))
        W["down2"] = full(got[0])
        s["ws"] = sg_w[l].astype(BF16)
        s["wst"] = jnp.swapaxes(sg_w[l], 1, 2).astype(BF16)
        s["bb"] = jnp.broadcast_to(sg_b[l][:, :, None], (SG_HEADS, SG_CHUNK, HEAD_DIM))
        s["gn"] = sg_norm[l:l + 1]
        s["pw"] = pool_w[l].astype(BF16)
        s["psc"] = pool_scale[l:l + 1]
        s["ecat"] = _rpb_expand(na_rpb[l])
        (att,), _ = _attn_fwd(qkv, s["ecat"])
        mix = jnp.concatenate([_sg_fwd(z, s["ws"], s["bb"], s["gn"]), _pool_fwd(z, s["pw"], s["psc"]),
                               by_token(att)], axis=1)
        s["z"], s["qkv"], s["mix"] = z, qkv, mix
        (xc, h), _ = _mm_res_norm(mix, W["w_out"][0], xc, ffn2_norm[l:l + 1], 1.0)
        s["x2"], s["h3"] = xc, h
        (s["a2"], s["pq2"]), got = _ffn_gu(h, W["gu2"], gather(l + 1, "gu1"))
        s["W"] = W
        W = {"gu1": full(got[0])} if got else {}
        gnext = ffn1_norm[l + 1:l + 2] if l + 1 < L else final_norm.reshape(1, D)
        (xc, h), _ = _mm_res_norm(s["a2"], s["W"]["down2"][0], xc, gnext, 0.5)
        saved.append(s)

    loss_row, dx, dxb, dg_final = _loss_bwd(xc, final_norm.reshape(1, D), tgt)
    loss = lax.psum(loss_row[0, 0], MESH_AXES)

    received = {grp: [[None] * len(names) for _ in range(L)] for grp, names, _ in GROUPS}
    small = {n: [None] * L for n in SMALL}

    def slots(g, *groups):
        return g.reshape(g.shape[0], N_DEV, -1, D), groups or tuple(range(g.shape[0]))

    for l in reversed(range(L)):
        s = saved[l]
        W = s["W"]
        (dgu,), _ = _mm_nt(dxb, W["down2"][0], 0.5, pq=s["pq2"])
        (g_down2,), _ = _mm_tn(s["a2"][None], dxb, 0.5)
        (dx, dxb, dgn), got = _dh_norm_bwd(dgu, W["gu2"], s["x2"], ffn2_norm[l:l + 1], dx,
                                           _Carry(scatters=[slots(g_down2)]))
        received["down2"][l][0] = got[0]
        small["ffn2_norm"][l] = dgn
        (g_gu2,), _ = _mm_tn(dgu, s["h3"], 1.0)

        (dmix,), _ = _mm_nt(dxb, W["w_out"][0], 1.0)
        (g_out,), _ = _mm_tn(s["mix"][None], dxb, 1.0)
        dzu, dzv, dws, dbb, dgn = _sg_bwd(s["z"], dmix, s["ws"], s["wst"], s["bb"], s["gn"])
        dzp, dpw, dpsc = _pool_bwd(s["z"], dmix, s["pw"], s["psc"])
        (dq, dk, dv, decat), got = _attn_bwd(s["qkv"], dmix, s["ecat"], _Carry(scatters=[slots(g_gu2, 0)]))
        received["gu2"][l][0] = got[0]
        small["sg_w"][l], small["sg_b"][l], small["sg_norm"][l] = dws, dbb[:, :, 0], dgn[0]
        small["pool_w"][l], small["pool_scale"][l] = dpw, dpsc[0]
        small["na_rpb"][l] = _rpb_collect(decat)
        dz = jnp.concatenate([dzu, dzv, dzp, by_token(dq), by_token(dk), by_token(dv)], axis=1)
        (dx, dxb, dgn), got = _dh_norm_bwd(dz[None], W["w_in"], s["x1"], mix_norm[l:l + 1], dx,
                                           _Carry(scatters=[slots(g_gu2, 1)]))
        received["gu2"][l][1] = got[0]
        small["mix_norm"][l] = dgn
        (g_in,), got = _mm_tn(dz[None], s["h2"], 1.0, _Carry(scatters=[slots(g_out)]))
        received["w_out"][l][0] = got[0]

        (dgu,), _ = _mm_nt(dxb, W["down1"][0], 0.5, pq=s["pq1"])
        (g_down1,), got = _mm_tn(s["a1"][None], dxb, 0.5, _Carry(scatters=[slots(g_in)]))
        received["w_in"][l][0] = got[0]
        (g_gu1,), got = _mm_tn(dgu, s["h1"], 1.0, _Carry(scatters=[slots(g_down1)]))
        received["down1"][l][0] = got[0]
        (dx, dxb, dgn), got = _dh_norm_bwd(dgu, W["gu1"], s["x0"], ffn1_norm[l:l + 1], dx,
                                           _Carry(scatters=[slots(g_gu1, 0), slots(g_gu1, 1)]))
        received["gu1"][l][0], received["gu1"][l][1] = got
        small["ffn1_norm"][l] = dgn

    small_shapes = {n: given[n].shape for n in SMALL}
    small_shapes["final_norm"] = final_norm.shape
    flat = [jnp.stack([jnp.reshape(g, (-1,)) for g in small[n]]).reshape(-1) for n in SMALL] + [dg_final.reshape(-1)]
    sizes = [f.shape[0] for f in flat]
    total = sum(sizes)
    padded = -(-total // 2048) * 2048
    local = jnp.concatenate(flat + [jnp.zeros((padded - total,), F32)]).reshape(1, -1, 128)
    summed = _sum_devices(_exchange(_Carry(gathers=[local]), "gather_small_grads")[0]).reshape(-1)
    grads, off = {}, 0
    for n, size in zip(list(SMALL) + ["final_norm"], sizes):
        grads[n] = summed[off:off + size].reshape(small_shapes[n])
        off += size
    for grp, names, cols in GROUPS:
        for t, n in enumerate(names):
            grads[n] = None
            for l in range(L):
                grads[n] = _sum_devices_into(received[grp][l][t], grads[n], l, L, cols)

    names = ['ffn1_norm', 'ffn1_w_gate', 'ffn1_w_up', 'ffn1_w_down', 'mix_norm', 'w_in', 'sg_norm', 'sg_w', 'sg_b',
             'pool_w', 'pool_scale', 'na_rpb', 'w_out', 'ffn2_norm', 'ffn2_w_gate', 'ffn2_w_up', 'ffn2_w_down',
             'final_norm']
    delta, new_m, new_v = {}, {}, {}
    for n in names:
        delta[n], new_m[n], new_v[n] = _adamw(given[n], grads[n], given["m_" + n], given["v_" + n])
    return (loss, dx.reshape(1, T, D), *[grads[n] for n in names], *[delta[n] for n in names],
            *[new_m[n] for n in names], *[new_v[n] for n in names])
```

```python
import functools
import math

import jax
import jax.numpy as jnp
from jax import lax
from jax.experimental import pallas as pl
from jax.experimental.pallas import tpu as pltpu

F32 = jnp.float32
BF16 = jnp.bfloat16
EPS = 1e-6
NEG = -1e30

HEAD_DIM = 128
SG_WIDTH = 512
SG_HEADS = 4
SG_CHUNK = 128
POOL_WINDOWS = (2, 4, 8, 16)
POOL_WIDTH = 512
POOL_HALO = 128
NA_WIDTH = 1024
NA_HEADS = 8
NA_KH = 8
NA_KW = 16
GRID_W = 64
Z_COLS = 2 * SG_WIDTH + POOL_WIDTH + 3 * NA_WIDTH
Q_OFF = 2 * SG_WIDTH + POOL_WIDTH
K_OFF = Q_OFF + NA_WIDTH
V_OFF = K_OFF + NA_WIDTH

ADAM_LR = 0.001
ADAM_B1 = 0.9
ADAM_B2 = 0.999
ADAM_EPS = 1e-08
ADAM_WD = 0.01
ADAM_STEP = 10

N_DEV = 8
MESH_AXES = ("x", "y", "c")
MESH = pl.DeviceIdType.MESH
ANY = pl.BlockSpec(memory_space=pl.ANY)

NT_DIMS = (((1,), (1,)), ((), ()))
TN_DIMS = (((0,), (0,)), ((), ()))

ROWS_PREF = 1024
MXU_DEPTH = 256
K_PREF = 1408


def _pick(n, pref, mult):
    best = None
    t = mult
    while t <= min(n, pref):
        if n % t == 0:
            best = t
        t += mult
    return n if best is None else best


def _gelu(x):
    return 0.5 * x * (1.0 + lax.erf(x * (1.0 / math.sqrt(2.0))))


def _gelu_grad(x):
    cdf = 0.5 * (1.0 + lax.erf(x * (1.0 / math.sqrt(2.0))))
    pdf = jnp.exp(-0.5 * x * x) * (1.0 / math.sqrt(2.0 * math.pi))
    return cdf + x * pdf


def _rms(x):
    return lax.rsqrt(jnp.mean(x * x, axis=-1, keepdims=True) + EPS)


def _norm_bwd(dh, x, g):
    r = _rms(x)
    w = dh * g
    dx = r * w - x * (r * r * r) * jnp.mean(w * x, axis=-1, keepdims=True)
    dg = jnp.sum(dh * (x * r), axis=0, keepdims=True)
    return dx, dg


def _position():
    return lax.axis_index("x"), lax.axis_index("y"), lax.axis_index("c")


def _index(p):
    return 4 * p[0] + 2 * p[1] + p[2]


class _Carry:
    def __init__(self, gathers=(), scatters=()):
        self.gathers, self.scatters = list(gathers), list(scatters)
        self.units = sum(g.shape[0] for g in self.gathers) + sum(len(groups) for _, groups in self.scatters)

    def arrays(self):
        return self.gathers + [s for s, _ in self.scatters]

    def out_shapes(self):
        return ([jax.ShapeDtypeStruct((g.shape[0], N_DEV) + g.shape[1:], g.dtype) for g in self.gathers]
                + [jax.ShapeDtypeStruct((len(groups),) + s.shape[1:], s.dtype) for s, groups in self.scatters])

    def scratch(self):
        return [pltpu.SemaphoreType.DMA((7 * self.units,)), pltpu.SemaphoreType.DMA((7 * self.units,)),
                pltpu.SemaphoreType.DMA((self.units,))]

    def _gather_copies(self, n, x_ref, out_ref, send, recv, local):
        x, y, c = _position()
        me, sibling = (x, y, c), (x, y, 1 - c)
        chips = [(1 - x, y), (x, 1 - y), (1 - x, 1 - y)]

        def copy(k, block, to, src=None):
            dst = out_ref.at[_index(block)]
            return pltpu.make_async_remote_copy(
                src_ref=dst if src is None else src, dst_ref=dst, send_sem=send.at[7 * n + k],
                recv_sem=recv.at[7 * n + k], device_id=to, device_id_type=MESH)

        return dict(
            mine=pltpu.make_async_copy(x_ref, out_ref.at[_index(me)], local.at[n]),
            first=[copy(0, me, sibling, x_ref)] + [copy(1 + j, me, (*ch, c), x_ref) for j, ch in enumerate(chips)],
            landed=[copy(1 + j, (*ch, c), me) for j, ch in enumerate(chips)],
            passed=[copy(4 + j, (*ch, c), sibling) for j, ch in enumerate(chips)],
            from_sibling=[copy(0, sibling, me)] + [copy(4 + j, (*ch, 1 - c), me) for j, ch in enumerate(chips)])

    def _scatter_copies(self, n, src_ref, out_ref, send, recv, local):
        x, y, c = _position()
        me = (x, y, c)
        sends, recvs = [], []
        for k in range(1, N_DEV):
            flip = lambda v, bit: 1 - v if bit else v
            peer = (flip(x, k & 4), flip(y, k & 2), flip(c, k & 1))
            sems = dict(send_sem=send.at[7 * n + k - 1], recv_sem=recv.at[7 * n + k - 1], device_id=peer,
                        device_id_type=MESH)
            sends.append(pltpu.make_async_remote_copy(src_ref=src_ref.at[_index(peer)], dst_ref=out_ref.at[_index(me)],
                                                      **sems))
            recvs.append(pltpu.make_async_remote_copy(src_ref=src_ref.at[_index(me)], dst_ref=out_ref.at[_index(peer)],
                                                      **sems))
        mine = pltpu.make_async_copy(src_ref.at[_index(me)], out_ref.at[_index(me)], local.at[n])
        return dict(mine=mine, sends=sends, recvs=recvs)

    def _pieces(self, ins, outs, sems):
        send, recv, local = sems
        gs, ss, unit = [], [], 0
        for n, g in enumerate(self.gathers):
            for t in range(g.shape[0]):
                gs.append(self._gather_copies(unit, ins[n].at[t], outs[n].at[t], send, recv, local))
                unit += 1
        for n, (_, groups) in enumerate(self.scatters, start=len(self.gathers)):
            for j, t in enumerate(groups):
                ss.append(self._scatter_copies(unit, ins[n].at[t], outs[n].at[j], send, recv, local))
                unit += 1
        return gs, ss

    def start(self, ins, outs, sems):
        gs, ss = self._pieces(ins, outs, sems)
        for g in gs:
            g["mine"].start()
            for cp in g["first"]:
                cp.start()
        for s in ss:
            s["mine"].start()
            for cp in s["sends"]:
                cp.start()

    def forward(self, ins, outs, sems):
        gs, _ = self._pieces(ins, outs, sems)
        for g in gs:
            for landed, passed in zip(g["landed"], g["passed"]):
                landed.wait_recv()
                passed.start()

    def finish(self, ins, outs, sems):
        gs, ss = self._pieces(ins, outs, sems)
        for g in gs:
            for cp in g["from_sibling"]:
                cp.wait_recv()
            for cp in g["first"] + g["passed"]:
                cp.wait_send()
            g["mine"].wait()
        for s in ss:
            for cp in s["recvs"]:
                cp.wait_recv()
            for cp in s["sends"]:
                cp.wait_send()
            s["mine"].wait()


def _call(body, *, name, grid, in_specs, out_specs, out_shape, args, scratch_shapes=(), carry=None):
    if carry is None or not carry.arrays():
        outs = pl.pallas_call(
            body, name=name, grid=grid, in_specs=in_specs, out_specs=out_specs, out_shape=out_shape,
            scratch_shapes=list(scratch_shapes),
            compiler_params=pltpu.CompilerParams(dimension_semantics=("arbitrary",) * len(grid)))(*args)
        return list(outs), []
    n_in, n_out, n_scr, n_car = len(in_specs), len(out_specs), len(scratch_shapes), len(carry.arrays())
    steps = math.prod(grid)
    middle = (steps * 6) // 10

    def wrapped(*refs):
        ins, refs = refs[:n_in], refs[n_in:]
        cins, refs = refs[:n_car], refs[n_car:]
        outs, refs = refs[:n_out], refs[n_out:]
        couts, refs = refs[:n_car], refs[n_car:]
        scr, sems = refs[:n_scr], refs[n_scr:]
        step = 0
        for d, size in enumerate(grid):
            step = step * size + pl.program_id(d)

        @pl.when(step == 0)
        def _():
            carry.start(cins, couts, sems)

        body(*ins, *outs, *scr)

        if carry.gathers:
            @pl.when(step == middle)
            def _():
                carry.forward(cins, couts, sems)

        @pl.when(step == steps - 1)
        def _():
            carry.finish(cins, couts, sems)

    outs = pl.pallas_call(
        wrapped, name=name + "_carry", grid=grid, in_specs=list(in_specs) + [ANY] * n_car,
        out_specs=list(out_specs) + [ANY] * n_car, out_shape=list(out_shape) + carry.out_shapes(),
        scratch_shapes=list(scratch_shapes) + carry.scratch(),
        compiler_params=pltpu.CompilerParams(dimension_semantics=("arbitrary",) * len(grid)))(*args, *carry.arrays())
    return list(outs[:n_out]), list(outs[n_out:])


def _exchange(carry, name):
    n_car = len(carry.arrays())

    def body(*refs):
        cins, couts, sems = refs[:n_car], refs[n_car:2 * n_car], refs[2 * n_car:]
        carry.start(cins, couts, sems)
        if carry.gathers:
            carry.forward(cins, couts, sems)
        carry.finish(cins, couts, sems)

    return list(pl.pallas_call(body, name=name, in_specs=[ANY] * n_car, out_specs=[ANY] * n_car,
                               out_shape=carry.out_shapes(), scratch_shapes=carry.scratch())(*carry.arrays()))


def _rmsnorm(x, g):
    T, D = x.shape
    tm = _pick(T, 512, 16)

    def body(x_ref, g_ref, o_ref):
        xv = x_ref[...]
        o_ref[...] = (xv * _rms(xv) * g_ref[...]).astype(BF16)

    return _call(body, name="rmsnorm", grid=(T // tm,),
                 in_specs=[pl.BlockSpec((tm, D), lambda i: (i, 0)), pl.BlockSpec((1, D), lambda i: (0, 0))],
                 out_specs=[pl.BlockSpec((tm, D), lambda i: (i, 0))],
                 out_shape=[jax.ShapeDtypeStruct((T, D), BF16)], args=(x, g))[0][0]


def _ffn_gu(h, wgu, carry=None):
    T, D = h.shape
    F = wgu.shape[1]
    tm = _pick(T, ROWS_PREF, 16)
    tn = _pick(F, 512, 128)

    def body(h_ref, wg_ref, wu_ref, a_ref, pq_ref):
        hv = h_ref[...]
        g = lax.dot_general(hv, wg_ref[0], NT_DIMS, preferred_element_type=F32)
        u = lax.dot_general(hv, wu_ref[0], NT_DIMS, preferred_element_type=F32)
        sg = jax.nn.sigmoid(g)
        q = g * sg
        a_ref[...] = (q * u).astype(BF16)
        pq_ref[0] = (u * (sg * (1.0 + g * (1.0 - sg)))).astype(BF16)
        pq_ref[1] = q.astype(BF16)

    return _call(body, name="ffn_gu", grid=(T // tm, F // tn),
                 in_specs=[pl.BlockSpec((tm, D), lambda i, j: (i, 0)),
                           pl.BlockSpec((1, tn, D), lambda i, j: (0, j, 0)),
                           pl.BlockSpec((1, tn, D), lambda i, j: (1, j, 0))],
                 out_specs=[pl.BlockSpec((tm, tn), lambda i, j: (i, j)),
                            pl.BlockSpec((2, tm, tn), lambda i, j: (0, i, j))],
                 out_shape=[jax.ShapeDtypeStruct((T, F), BF16), jax.ShapeDtypeStruct((2, T, F), BF16)],
                 args=(h, wgu, wgu), carry=carry)


def _mm_nt(a, w, scale, pq=None, carry=None):
    T, K = a.shape
    N = w.shape[0]
    tm = _pick(T, ROWS_PREF, 16)
    tn = _pick(N, 512, 128)

    def body(*refs):
        a_ref, w_ref = refs[:2]
        d = lax.dot_general(a_ref[...], w_ref[...], NT_DIMS, preferred_element_type=F32)
        if scale != 1.0:
            d = d * scale
        if pq is None:
            refs[2][...] = d.astype(BF16)
        else:
            pq_ref, o_ref = refs[2:]
            o_ref[0] = (d * pq_ref[0].astype(F32)).astype(BF16)
            o_ref[1] = (d * pq_ref[1].astype(F32)).astype(BF16)

    in_specs = [pl.BlockSpec((tm, K), lambda i, j: (i, 0)), pl.BlockSpec((tn, K), lambda i, j: (j, 0))]
    if pq is None:
        args, out_spec, out_shape = (a, w), pl.BlockSpec((tm, tn), lambda i, j: (i, j)), (T, N)
    else:
        in_specs.append(pl.BlockSpec((2, tm, tn), lambda i, j: (0, i, j)))
        args, out_spec, out_shape = (a, w, pq), pl.BlockSpec((2, tm, tn), lambda i, j: (0, i, j)), (2, T, N)
    return _call(body, name="mm_nt" if pq is None else "ffn_da", grid=(T // tm, N // tn),
                 in_specs=in_specs, out_specs=[out_spec], out_shape=[jax.ShapeDtypeStruct(out_shape, BF16)],
                 args=args, carry=carry)


def _z_proj(h, wt, carry=None):
    T, K = h.shape
    tm = _pick(T, ROWS_PREF, 16)
    tn = 4 * HEAD_DIM
    flat = Q_OFF // tn

    def body(h_ref, w_ref, z_ref, qkv_ref):
        j = pl.program_id(1)
        zv = lax.dot_general(h_ref[...], w_ref[...], NT_DIMS, preferred_element_type=F32).astype(BF16)

        @pl.when(j < flat)
        def _():
            z_ref[...] = zv

        @pl.when(j >= flat)
        def _():
            for c in range(tn // HEAD_DIM):
                qkv_ref[c] = zv[:, c * HEAD_DIM:(c + 1) * HEAD_DIM]

    return _call(body, name="z_proj", grid=(T // tm, Z_COLS // tn),
                 in_specs=[pl.BlockSpec((tm, K), lambda i, j: (i, 0)), pl.BlockSpec((tn, K), lambda i, j: (j, 0))],
                 out_specs=[pl.BlockSpec((tm, tn), lambda i, j: (i, jnp.minimum(j, flat - 1))),
                            pl.BlockSpec((tn // HEAD_DIM, tm, HEAD_DIM), lambda i, j: (jnp.maximum(j - flat, 0), i, 0))],
                 out_shape=[jax.ShapeDtypeStruct((T, Q_OFF), BF16),
                            jax.ShapeDtypeStruct((3 * NA_HEADS, T, HEAD_DIM), BF16)],
                 args=(h, wt), carry=carry)


def _mm_res_norm(a, w, x, gnext, scale, carry=None):
    T, K = a.shape
    D = w.shape[1]
    tm = _pick(T, 512, 16)
    tk = _pick(K, K_PREF, 128)
    nk = K // tk

    def body(a_ref, w_ref, x_ref, g_ref, xo_ref, ho_ref, acc):
        k = pl.program_id(1)

        @pl.when(k == 0)
        def _():
            acc[...] = jnp.zeros_like(acc)

        acc[...] += jnp.dot(a_ref[...], w_ref[...], preferred_element_type=F32)

        @pl.when(k == nk - 1)
        def _():
            xn = x_ref[...] + scale * acc[...]
            xo_ref[...] = xn
            ho_ref[...] = (xn * _rms(xn) * g_ref[...]).astype(BF16)

    row = pl.BlockSpec((tm, D), lambda i, k: (i, 0))
    return _call(body, name="mm_res_norm", grid=(T // tm, nk),
                 in_specs=[pl.BlockSpec((tm, tk), lambda i, k: (i, k)), pl.BlockSpec((tk, D), lambda i, k: (k, 0)),
                           row, pl.BlockSpec((1, D), lambda i, k: (0, 0))],
                 out_specs=[row, row],
                 out_shape=[jax.ShapeDtypeStruct((T, D), F32), jax.ShapeDtypeStruct((T, D), BF16)],
                 scratch_shapes=[pltpu.VMEM((tm, D), F32)], args=(a, w, x, gnext), carry=carry)


def _mm_tn(a, b, scale, carry=None):
    G, T, M = a.shape
    N = b.shape[1]
    tm = _pick(M, 512, 128)
    tk = _pick(T, 2048, 16)
    nk = T // tk

    def body(a_ref, b_ref, o_ref, acc):
        k = pl.program_id(2)

        @pl.when(k == 0)
        def _():
            acc[...] = jnp.zeros_like(acc)

        acc[...] += lax.dot_general(a_ref[0], b_ref[...], TN_DIMS, preferred_element_type=F32)

        @pl.when(k == nk - 1)
        def _():
            o_ref[0] = (acc[...] * scale).astype(BF16)

    return _call(body, name="mm_tn", grid=(G, M // tm, nk),
                 in_specs=[pl.BlockSpec((1, tk, tm), lambda g, i, k: (g, k, i)),
                           pl.BlockSpec((tk, N), lambda g, i, k: (k, 0))],
                 out_specs=[pl.BlockSpec((1, tm, N), lambda g, i, k: (g, i, 0))],
                 out_shape=[jax.ShapeDtypeStruct((G, M, N), BF16)],
                 scratch_shapes=[pltpu.VMEM((tm, N), F32)], args=(a, b), carry=carry)


def _dh_norm_bwd(d, wt, x, g, dres, carry=None):
    T, D = x.shape
    G, _, K = d.shape
    tm = _pick(T, 512, 16)
    tk = _pick(K, 2 * MXU_DEPTH, MXU_DEPTH)
    nk = K // tk

    def body(*refs):
        dw_refs = refs[:2 * G]
        x_ref, g_ref, dres_ref, dx_ref, dxb_ref, dg_ref, acc = refs[2 * G:]
        i = pl.program_id(0)
        k = pl.program_id(1)

        @pl.when(k == 0)
        def _():
            acc[...] = jnp.zeros_like(acc)

        for n in range(G):
            acc[...] += jnp.dot(dw_refs[2 * n][0], dw_refs[2 * n + 1][0], preferred_element_type=F32)

        @pl.when(k == nk - 1)
        def _():
            dxn, dgp = _norm_bwd(acc[...], x_ref[...], g_ref[...])
            dxv = dres_ref[...] + dxn
            dx_ref[...] = dxv
            dxb_ref[...] = dxv.astype(BF16)

            @pl.when(i == 0)
            def _():
                dg_ref[...] = dgp

            @pl.when(i > 0)
            def _():
                dg_ref[...] += dgp

    row = pl.BlockSpec((tm, D), lambda i, k: (i, 0))
    vec = pl.BlockSpec((1, D), lambda i, k: (0, 0))
    pairs = []
    for n in range(G):
        pairs += [pl.BlockSpec((1, tm, tk), lambda i, k, n=n: (n, i, k)),
                  pl.BlockSpec((1, tk, D), lambda i, k, n=n: (n, k, 0))]
    return _call(body, name="dh_norm_bwd", grid=(T // tm, nk),
                 in_specs=pairs + [row, vec, row], out_specs=[row, row, vec],
                 out_shape=[jax.ShapeDtypeStruct((T, D), F32), jax.ShapeDtypeStruct((T, D), BF16),
                            jax.ShapeDtypeStruct((1, D), F32)],
                 scratch_shapes=[pltpu.VMEM((tm, D), F32)], args=(*[d, wt] * G, x, g, dres), carry=carry)


def _loss_bwd(x, g, tgt):
    T, D = x.shape
    tm = _pick(T, 512, 16)

    def body(x_ref, g_ref, t_ref, loss_ref, dx_ref, dxb_ref, dg_ref):
        i = pl.program_id(0)
        xv = x_ref[...]
        gv = g_ref[...]
        e = xv * _rms(xv) * gv - t_ref[...]
        part = jnp.sum(jnp.sum(e * e, axis=-1, keepdims=True), axis=0, keepdims=True) * (0.5 / D)
        dxn, dgp = _norm_bwd(e * (1.0 / D), xv, gv)
        dx_ref[...] = dxn
        dxb_ref[...] = dxn.astype(BF16)

        @pl.when(i == 0)
        def _():
            loss_ref[...] = jnp.broadcast_to(part, loss_ref.shape)
            dg_ref[...] = dgp

        @pl.when(i > 0)
        def _():
            loss_ref[...] += jnp.broadcast_to(part, loss_ref.shape)
            dg_ref[...] += dgp

    row = pl.BlockSpec((tm, D), lambda i: (i, 0))
    vec = pl.BlockSpec((1, D), lambda i: (0, 0))
    return _call(body, name="loss_bwd", grid=(T // tm,), in_specs=[row, vec, row],
                 out_specs=[pl.BlockSpec((1, 128), lambda i: (0, 0)), row, row, vec],
                 out_shape=[jax.ShapeDtypeStruct((1, 128), F32), jax.ShapeDtypeStruct((T, D), F32),
                            jax.ShapeDtypeStruct((T, D), BF16), jax.ShapeDtypeStruct((1, D), F32)],
                 args=(x, g, tgt))[0]


def _sg_fwd(z, ws, bb, gn):
    T = z.shape[0]

    def body(zu_ref, zv_ref, ws_ref, bb_ref, gn_ref, a_ref):
        for h in range(SG_HEADS):
            sl = slice(h * HEAD_DIM, (h + 1) * HEAD_DIM)
            gv = _gelu(zv_ref[:, sl].astype(F32))
            vn = (gv * _rms(gv) * gn_ref[:, sl]).astype(BF16)
            mixed = jnp.dot(ws_ref[h], vn, preferred_element_type=F32) + bb_ref[h]
            a_ref[:, sl] = (_gelu(zu_ref[:, sl].astype(F32)) * mixed).astype(BF16)

    full = lambda shape: pl.BlockSpec(shape, lambda n: (0,) * len(shape))
    return _call(body, name="sg_fwd", grid=(T // SG_CHUNK,),
                 in_specs=[pl.BlockSpec((SG_CHUNK, SG_WIDTH), lambda n: (n, 0)),
                           pl.BlockSpec((SG_CHUNK, SG_WIDTH), lambda n: (n, 1)),
                           full((SG_HEADS, SG_CHUNK, SG_CHUNK)), full((SG_HEADS, SG_CHUNK, HEAD_DIM)),
                           full((1, SG_WIDTH))],
                 out_specs=[pl.BlockSpec((SG_CHUNK, SG_WIDTH), lambda n: (n, 0))],
                 out_shape=[jax.ShapeDtypeStruct((T, SG_WIDTH), BF16)], args=(z, z, ws, bb, gn))[0][0]


def _sg_bwd(z, dmix, ws, wst, bb, gn):
    T = z.shape[0]

    def body(zu_ref, zv_ref, da_ref, ws_ref, wst_ref, bb_ref, gn_ref, dzu_ref, dzv_ref, dws_ref, dbb_ref, dgn_ref):
        n = pl.program_id(0)

        @pl.when(n == 0)
        def _():
            dws_ref[...] = jnp.zeros_like(dws_ref)
            dbb_ref[...] = jnp.zeros_like(dbb_ref)
            dgn_ref[...] = jnp.zeros_like(dgn_ref)

        for h in range(SG_HEADS):
            sl = slice(h * HEAD_DIM, (h + 1) * HEAD_DIM)
            u = zu_ref[:, sl].astype(F32)
            v = zv_ref[:, sl].astype(F32)
            da = da_ref[:, sl].astype(F32)
            gain = gn_ref[:, sl]
            gv = _gelu(v)
            r = _rms(gv)
            vn = (gv * r * gain).astype(BF16)
            mixed = jnp.dot(ws_ref[h], vn, preferred_element_type=F32) + bb_ref[h]
            dmixed = da * _gelu(u)
            dzu_ref[:, sl] = (da * mixed * _gelu_grad(u)).astype(BF16)
            dmb = dmixed.astype(BF16)
            dws_ref[h] += lax.dot_general(dmb, vn, NT_DIMS, preferred_element_type=F32)
            dbb_ref[h] += jnp.broadcast_to(jnp.sum(dmixed, axis=-1, keepdims=True), (SG_CHUNK, HEAD_DIM))
            dvn = jnp.dot(wst_ref[h], dmb, preferred_element_type=F32)
            dgv, dg = _norm_bwd(dvn, gv, gain)
            dgn_ref[:, sl] += dg
            dzv_ref[:, sl] = (dgv * _gelu_grad(v)).astype(BF16)

    full = lambda shape: pl.BlockSpec(shape, lambda n: (0,) * len(shape))
    wspec = full((SG_HEADS, SG_CHUNK, SG_CHUNK))
    tile = lambda c: pl.BlockSpec((SG_CHUNK, SG_WIDTH), lambda n: (n, c))
    return _call(body, name="sg_bwd", grid=(T // SG_CHUNK,),
                 in_specs=[tile(0), tile(1), tile(0), wspec, wspec, full((SG_HEADS, SG_CHUNK, HEAD_DIM)),
                           full((1, SG_WIDTH))],
                 out_specs=[tile(0), tile(0), wspec, full((SG_HEADS, SG_CHUNK, HEAD_DIM)), full((1, SG_WIDTH))],
                 out_shape=[jax.ShapeDtypeStruct((T, SG_WIDTH), BF16), jax.ShapeDtypeStruct((T, SG_WIDTH), BF16),
                            jax.ShapeDtypeStruct((SG_HEADS, SG_CHUNK, SG_CHUNK), F32),
                            jax.ShapeDtypeStruct((SG_HEADS, SG_CHUNK, HEAD_DIM), F32),
                            jax.ShapeDtypeStruct((1, SG_WIDTH), F32)], args=(z, z, dmix, ws, wst, bb, gn))[0]


def _pool_specs(T, tp, col):
    step = tp // POOL_HALO
    last = T // POOL_HALO - 1
    return [pl.BlockSpec((POOL_HALO, POOL_WIDTH), lambda i: (jnp.maximum(i * step - 1, 0), col)),
            pl.BlockSpec((tp, POOL_WIDTH), lambda i: (i, col)),
            pl.BlockSpec((POOL_HALO, POOL_WIDTH), lambda i: (jnp.minimum((i + 1) * step, last), col))]


def _pool_band(i, tp, T, win):
    ext = tp + 2 * POOL_HALO
    t = i * tp + lax.broadcasted_iota(jnp.int32, (tp, ext), 0)
    s = i * tp - POOL_HALO + lax.broadcasted_iota(jnp.int32, (tp, ext), 1)
    band = (s >= jnp.maximum(t - win // 2, 0)) & (s < jnp.minimum(t + win // 2, T))
    t1 = i * tp + lax.broadcasted_iota(jnp.int32, (tp, 1), 0)
    cnt = (jnp.minimum(t1 + win // 2, T) - jnp.maximum(t1 - win // 2, 0)).astype(F32)
    return band.astype(BF16), cnt


def _pool_fwd(z, pw, psc):
    T = z.shape[0]
    tp = _pick(T, 256, POOL_HALO)

    def body(pp_ref, pc_ref, pn_ref, w_ref, sc_ref, o_ref):
        i = pl.program_id(0)
        halo = jnp.concatenate([pp_ref[...], pc_ref[...], pn_ref[...]], axis=0)
        for g, win in enumerate(POOL_WINDOWS):
            sl = slice(g * HEAD_DIM, (g + 1) * HEAD_DIM)
            band, cnt = _pool_band(i, tp, T, win)
            ssum = jnp.dot(band, halo[:, sl], preferred_element_type=F32)
            d = ssum / cnt - pc_ref[:, sl].astype(F32)
            y = jnp.dot(d.astype(BF16), w_ref[g], preferred_element_type=F32) * sc_ref[:, sl]
            o_ref[:, sl] = y.astype(BF16)

    full = lambda shape: pl.BlockSpec(shape, lambda i: (0,) * len(shape))
    return _call(body, name="pool_fwd", grid=(T // tp,),
                 in_specs=_pool_specs(T, tp, 2) + [full((4, HEAD_DIM, HEAD_DIM)), full((1, POOL_WIDTH))],
                 out_specs=[pl.BlockSpec((tp, POOL_WIDTH), lambda i: (i, 0))],
                 out_shape=[jax.ShapeDtypeStruct((T, POOL_WIDTH), BF16)], args=(z, z, z, pw, psc))[0][0]


def _pool_bwd(z, dmix, pw, psc):
    T = z.shape[0]
    tp = _pick(T, 256, POOL_HALO)
    ext = tp + 2 * POOL_HALO

    def body(pp_ref, pc_ref, pn_ref, dp_ref, dc_ref, dn_ref, w_ref, sc_ref, dz_ref, dw_ref, dsc_ref):
        i = pl.program_id(0)

        @pl.when(i == 0)
        def _():
            dw_ref[...] = jnp.zeros_like(dw_ref)
            dsc_ref[...] = jnp.zeros_like(dsc_ref)

        halo = jnp.concatenate([pp_ref[...], pc_ref[...], pn_ref[...]], axis=0)
        dy_halo = jnp.concatenate([dp_ref[...], dc_ref[...], dn_ref[...]], axis=0)
        th = i * tp - POOL_HALO + lax.broadcasted_iota(jnp.int32, (ext, 1), 0)
        inside = (th >= 0) & (th < T)
        s2 = i * tp + lax.broadcasted_iota(jnp.int32, (tp, ext), 0)
        t2 = i * tp - POOL_HALO + lax.broadcasted_iota(jnp.int32, (tp, ext), 1)
        for g, win in enumerate(POOL_WINDOWS):
            sl = slice(g * HEAD_DIM, (g + 1) * HEAD_DIM)
            sc = sc_ref[:, sl]
            band, cnt = _pool_band(i, tp, T, win)
            ssum = jnp.dot(band, halo[:, sl], preferred_element_type=F32)
            db = (ssum / cnt - pc_ref[:, sl].astype(F32)).astype(BF16)
            yraw = jnp.dot(db, w_ref[g], preferred_element_type=F32)
            dyc = dc_ref[:, sl].astype(F32)
            dsc_ref[:, sl] += jnp.sum(dyc * yraw, axis=0, keepdims=True)
            dw_ref[g] += lax.dot_general(db, (dyc * sc).astype(BF16), TN_DIMS, preferred_element_type=F32)
            dd = lax.dot_general((dy_halo[:, sl].astype(F32) * sc).astype(BF16), w_ref[g], NT_DIMS,
                                 preferred_element_type=F32)
            cnt_h = (jnp.minimum(th + win // 2, T) - jnp.maximum(th - win // 2, 0)).astype(F32)
            ddc = jnp.where(inside, dd / jnp.maximum(cnt_h, 1.0), 0.0)
            hi = ddc.astype(BF16)
            lo = (ddc - hi.astype(F32)).astype(BF16)
            band_t = ((s2 >= jnp.maximum(t2 - win // 2, 0)) & (s2 < jnp.minimum(t2 + win // 2, T))).astype(BF16)
            dpool = (jnp.dot(band_t, hi, preferred_element_type=F32) + jnp.dot(band_t, lo, preferred_element_type=F32)
                     - dd[POOL_HALO:POOL_HALO + tp])
            dz_ref[:, sl] = dpool.astype(BF16)

    full = lambda shape: pl.BlockSpec(shape, lambda i: (0,) * len(shape))
    return _call(body, name="pool_bwd", grid=(T // tp,),
                 in_specs=_pool_specs(T, tp, 2) + _pool_specs(T, tp, 1)
                 + [full((4, HEAD_DIM, HEAD_DIM)), full((1, POOL_WIDTH))],
                 out_specs=[pl.BlockSpec((tp, POOL_WIDTH), lambda i: (i, 0)), full((4, HEAD_DIM, HEAD_DIM)),
                            full((1, POOL_WIDTH))],
                 out_shape=[jax.ShapeDtypeStruct((T, POOL_WIDTH), BF16),
                            jax.ShapeDtypeStruct((4, HEAD_DIM, HEAD_DIM), F32),
                            jax.ShapeDtypeStruct((1, POOL_WIDTH), F32)], args=(z, z, z, dmix, dmix, dmix, pw, psc))[0]


ATT_ROWS = 8
WIN_KEYS = NA_KH * GRID_W


def _col_mask():
    q = lax.broadcasted_iota(jnp.int32, (GRID_W, WIN_KEYS), 0)
    k = lax.broadcasted_iota(jnp.int32, (GRID_W, WIN_KEYS), 1) & (GRID_W - 1)
    start = jnp.clip(q - NA_KW // 2, 0, GRID_W - NA_KW)
    return (k >= start) & (k < start + NA_KW)


def _softmax(s, bias, mask):
    s = jnp.where(mask, s * (HEAD_DIM ** -0.5) + bias, NEG)
    p = jnp.exp(s - jnp.max(s, axis=-1, keepdims=True))
    return p / jnp.sum(p, axis=-1, keepdims=True)


def _attn_window(step, a, rows):
    r = step * ATT_ROWS + a
    sr = jnp.clip(r - NA_KH // 2, 0, rows - NA_KH)
    return pl.ds(pl.multiple_of(sr * GRID_W, GRID_W), WIN_KEYS), sr - r + NA_KH - 1


def _head_specs(T, blk):
    whole = lambda first: pl.BlockSpec((1, T, HEAD_DIM), lambda h, s: (first + h, 0, 0))
    return [pl.BlockSpec((1, blk, HEAD_DIM), lambda h, s: (h, s, 0)), whole(NA_HEADS), whole(2 * NA_HEADS)]


def _attn_fwd(qkv, ecat, carry=None):
    T = qkv.shape[1]
    rows = T // GRID_W
    blk = ATT_ROWS * GRID_W

    def body(q_ref, k_ref, v_ref, e_ref, o_ref, s_scr, p_scr):
        step = pl.program_id(1)
        mask = _col_mask()
        wins = [_attn_window(step, a, rows) for a in range(ATT_ROWS)]
        qs = [slice(a * GRID_W, (a + 1) * GRID_W) for a in range(ATT_ROWS)]
        for a, (win, _) in enumerate(wins):
            s_scr[a] = lax.dot_general(q_ref[0, qs[a], :], k_ref[0, win, :], NT_DIMS, preferred_element_type=F32)
        for a, (_, dr0) in enumerate(wins):
            p_scr[a] = _softmax(s_scr[a], e_ref[0, dr0], mask).astype(BF16)
        for a, (win, _) in enumerate(wins):
            o_ref[0, qs[a], :] = jnp.dot(p_scr[a], v_ref[0, win, :], preferred_element_type=F32).astype(BF16)

    return _call(body, name="attn_fwd", grid=(NA_HEADS, rows // ATT_ROWS),
                 in_specs=_head_specs(T, blk)
                 + [pl.BlockSpec((1, NA_KH, GRID_W, WIN_KEYS), lambda h, s: (h, 0, 0, 0))],
                 out_specs=[pl.BlockSpec((1, blk, HEAD_DIM), lambda h, s: (h, s, 0))],
                 out_shape=[jax.ShapeDtypeStruct((NA_HEADS, T, HEAD_DIM), BF16)],
                 scratch_shapes=[pltpu.VMEM((ATT_ROWS, GRID_W, WIN_KEYS), F32),
                                 pltpu.VMEM((ATT_ROWS, GRID_W, WIN_KEYS), BF16)],
                 args=(qkv, qkv, qkv, ecat), carry=carry)


def _attn_bwd(qkv, dmix, ecat, carry=None):
    T = qkv.shape[1]
    rows = T // GRID_W
    blk = ATT_ROWS * GRID_W
    nstep = rows // ATT_ROWS

    def body(q_ref, k_ref, v_ref, do_ref, e_ref, dq_ref, dk_ref, dv_ref, de_ref, dk_acc, dv_acc, s_scr, dp_scr,
             p_scr, ds_scr):
        step = pl.program_id(1)
        mask = _col_mask()

        @pl.when(step == 0)
        def _():
            dk_acc[...] = jnp.zeros_like(dk_acc)
            dv_acc[...] = jnp.zeros_like(dv_acc)
            de_ref[...] = jnp.zeros_like(de_ref)

        wins = [_attn_window(step, a, rows) for a in range(ATT_ROWS)]
        qs = [slice(a * GRID_W, (a + 1) * GRID_W) for a in range(ATT_ROWS)]
        for a, (win, _) in enumerate(wins):
            s_scr[a] = lax.dot_general(q_ref[0, qs[a], :], k_ref[0, win, :], NT_DIMS, preferred_element_type=F32)
            dp_scr[a] = lax.dot_general(do_ref[qs[a], :], v_ref[0, win, :], NT_DIMS, preferred_element_type=F32)
        for a, (_, dr0) in enumerate(wins):
            pr = _softmax(s_scr[a], e_ref[0, dr0], mask)
            dp = dp_scr[a]
            ds = pr * (dp - jnp.sum(dp * pr, axis=-1, keepdims=True))
            de_ref[0, dr0] += ds
            p_scr[a] = pr.astype(BF16)
            ds_scr[a] = (ds * (HEAD_DIM ** -0.5)).astype(BF16)
        for a, (win, _) in enumerate(wins):
            dq_ref[0, qs[a], :] = jnp.dot(ds_scr[a], k_ref[0, win, :], preferred_element_type=F32).astype(BF16)
            dv_acc[win, :] += lax.dot_general(p_scr[a], do_ref[qs[a], :], TN_DIMS, preferred_element_type=F32)
            dk_acc[win, :] += lax.dot_general(ds_scr[a], q_ref[0, qs[a], :], TN_DIMS, preferred_element_type=F32)

        @pl.when(step == nstep - 1)
        def _():
            dk_ref[0] = dk_acc[...].astype(BF16)
            dv_ref[0] = dv_acc[...].astype(BF16)

    whole = pl.BlockSpec((1, T, HEAD_DIM), lambda h, s: (h, 0, 0))
    e_spec = pl.BlockSpec((1, NA_KH, GRID_W, WIN_KEYS), lambda h, s: (h, 0, 0, 0))
    out = jax.ShapeDtypeStruct((NA_HEADS, T, HEAD_DIM), BF16)
    stage = lambda dtype: pltpu.VMEM((ATT_ROWS, GRID_W, WIN_KEYS), dtype)
    return _call(body, name="attn_bwd", grid=(NA_HEADS, nstep),
                 in_specs=_head_specs(T, blk)
                 + [pl.BlockSpec((blk, HEAD_DIM), lambda h, s: (s, (SG_WIDTH + POOL_WIDTH) // HEAD_DIM + h)), e_spec],
                 out_specs=[pl.BlockSpec((1, blk, HEAD_DIM), lambda h, s: (h, s, 0)), whole, whole, e_spec],
                 out_shape=[out, out, out, jax.ShapeDtypeStruct((NA_HEADS, NA_KH, GRID_W, WIN_KEYS), F32)],
                 scratch_shapes=[pltpu.VMEM((T, HEAD_DIM), F32), pltpu.VMEM((T, HEAD_DIM), F32),
                                 stage(F32), stage(F32), stage(BF16), stage(BF16)],
                 args=(qkv, qkv, qkv, dmix, ecat), carry=carry)


def _rpb_tables():
    col = jnp.arange(GRID_W)
    dc = jnp.clip(col[None, :] - col[:, None] + NA_KW - 1, 0, 2 * NA_KW - 2)
    by_col = (dc[None] == jnp.arange(2 * NA_KW - 1)[:, None, None]).astype(F32)
    d, j = jnp.arange(NA_KH)[:, None], jnp.arange(NA_KH)[None, :]
    by_row = (jnp.arange(2 * NA_KH - 1)[:, None, None] == (d + j)[None]).astype(F32)
    return by_col, by_row


def _rpb_expand(rpb):
    by_col, by_row = _rpb_tables()
    e = jnp.einsum("hrc,cqk->hrqk", rpb, by_col, precision=lax.Precision.HIGHEST)
    ecat = jnp.einsum("hrqk,rdj->hdqjk", e, by_row, precision=lax.Precision.HIGHEST)
    return ecat.reshape(NA_HEADS, NA_KH, GRID_W, WIN_KEYS)


def _rpb_collect(decat):
    by_col, by_row = _rpb_tables()
    de = jnp.einsum("hdqjk,rdj->hrqk", decat.reshape(NA_HEADS, NA_KH, GRID_W, NA_KH, GRID_W), by_row,
                    precision=lax.Precision.HIGHEST)
    return jnp.einsum("hrqk,cqk->hrc", de, by_col, precision=lax.Precision.HIGHEST)


def _adamw(w, g, m, v):
    shape = w.shape
    C = shape[-1]
    R = w.size // C
    tr = _pick(R, max(8, (1 << 18) // C), 8)
    args = [a.reshape(R, C) for a in (w, g, m, v)]

    def body(w_ref, g_ref, m_ref, v_ref, d_ref, mo_ref, vo_ref):
        gv = g_ref[...]
        mn = ADAM_B1 * m_ref[...] + (1.0 - ADAM_B1) * gv
        vn = ADAM_B2 * v_ref[...] + (1.0 - ADAM_B2) * (gv * gv)
        m_hat = mn / (1.0 - ADAM_B1 ** ADAM_STEP)
        v_hat = vn / (1.0 - ADAM_B2 ** ADAM_STEP)
        d_ref[...] = -ADAM_LR * (m_hat / (jnp.sqrt(v_hat) + ADAM_EPS) + ADAM_WD * w_ref[...])
        mo_ref[...] = mn
        vo_ref[...] = vn

    spec = pl.BlockSpec((tr, C), lambda i: (i, 0))
    out = jax.ShapeDtypeStruct((R, C), F32)
    res = _call(body, name="adamw", grid=(R // tr,), in_specs=[spec] * 4, out_specs=[spec] * 3, out_shape=[out] * 3,
                args=args)[0]
    return [r.reshape(shape) for r in res]


def _sum_devices(g):
    G, _, R, C = g.shape
    tr = _pick(R, max(16, (1 << 18) // C), 16)

    def body(g_ref, o_ref):
        acc = g_ref[0, 0].astype(F32)
        for k in range(1, N_DEV):
            acc = acc + g_ref[0, k].astype(F32)
        o_ref[0] = acc

    return _call(body, name="sum_devices", grid=(G, R // tr),
                 in_specs=[pl.BlockSpec((1, N_DEV, tr, C), lambda t, i: (t, 0, i, 0))],
                 out_specs=[pl.BlockSpec((1, tr, C), lambda t, i: (t, i, 0))],
                 out_shape=[jax.ShapeDtypeStruct((G, R, C), F32)], args=(g,))[0][0]


def _sum_devices_into(g, prev, l, L, transpose):
    _, _, R, C = g.shape
    if transpose:
        tc = _pick(C, 256, 128)
        grid, shape = (C // tc,), (L, C, R)
        in_spec = pl.BlockSpec((1, N_DEV, R, tc), lambda i: (0, 0, 0, i))
        out_spec = pl.BlockSpec((1, tc, R), lambda i: (l, i, 0))
    else:
        tr = _pick(R, max(16, (1 << 18) // C), 16)
        grid, shape = (R // tr,), (L, R, C)
        in_spec = pl.BlockSpec((1, N_DEV, tr, C), lambda i: (0, 0, i, 0))
        out_spec = pl.BlockSpec((1, tr, C), lambda i: (l, i, 0))

    def body(g_ref, *rest):
        acc = g_ref[0, 0].astype(F32)
        for k in range(1, N_DEV):
            acc = acc + g_ref[0, k].astype(F32)
        rest[-1][0] = acc.T if transpose else acc

    first = prev is None
    return pl.pallas_call(
        body, name="sum_devices_into", grid=grid, in_specs=[in_spec] if first else [in_spec, ANY],
        out_specs=out_spec, out_shape=jax.ShapeDtypeStruct(shape, F32),
        input_output_aliases={} if first else {1: 0},
        compiler_params=pltpu.CompilerParams(dimension_semantics=("arbitrary",)))(*((g,) if first else (g, prev)))


SMALL = ("ffn1_norm", "mix_norm", "sg_norm", "sg_w", "sg_b", "pool_w", "pool_scale", "na_rpb", "ffn2_norm")
GROUPS = (("gu1", ("ffn1_w_gate", "ffn1_w_up"), True), ("down1", ("ffn1_w_down",), False), ("w_in", ("w_in",), True),
          ("w_out", ("w_out",), False), ("gu2", ("ffn2_w_gate", "ffn2_w_up"), True), ("down2", ("ffn2_w_down",), False))


def kernel(x, ffn1_norm, ffn1_w_gate, ffn1_w_up, ffn1_w_down, mix_norm, w_in, sg_norm, sg_w, sg_b, pool_w, pool_scale, na_rpb, w_out, ffn2_norm, ffn2_w_gate, ffn2_w_up, ffn2_w_down, final_norm, loss_target, m_ffn1_norm, m_ffn1_w_gate, m_ffn1_w_up, m_ffn1_w_down, m_mix_norm, m_w_in, m_sg_norm, m_sg_w, m_sg_b, m_pool_w, m_pool_scale, m_na_rpb, m_w_out, m_ffn2_norm, m_ffn2_w_gate, m_ffn2_w_up, m_ffn2_w_down, m_final_norm, v_ffn1_norm, v_ffn1_w_gate, v_ffn1_w_up, v_ffn1_w_down, v_mix_norm, v_w_in, v_sg_norm, v_sg_w, v_sg_b, v_pool_w, v_pool_scale, v_na_rpb, v_w_out, v_ffn2_norm, v_ffn2_w_gate, v_ffn2_w_up, v_ffn2_w_down, v_final_norm):
    given = dict(locals())
    T, D = x.shape[1], x.shape[2]
    L = ffn1_norm.shape[0]
    assert x.shape[0] == 1 and D == SG_WIDTH + POOL_WIDTH + NA_WIDTH and w_in.shape[2] * N_DEV == Z_COLS
    assert T % (ATT_ROWS * GRID_W) == 0 and T // GRID_W >= NA_KH
    x0 = x.reshape(T, D)
    tgt = loss_target.reshape(T, D)
    members = {grp: (names, cols) for grp, names, cols in GROUPS}

    def shard(grp, l):
        names, cols = members[grp]
        return jnp.stack([(given[n][l].T if cols else given[n][l]).astype(BF16) for n in names])

    def gather(l, *grps):
        return _Carry(gathers=[shard(grp, l) for grp in grps]) if l < L else None

    def full(gathered):
        return gathered.reshape(gathered.shape[0], -1, D)

    def by_token(heads):
        return heads.transpose(1, 0, 2).reshape(T, NA_WIDTH)

    W = {"gu1": full(_exchange(gather(0, "gu1"), "gather_first")[0])}
    saved = []
    xc = x0
    h = _rmsnorm(xc, ffn1_norm[0:1])
    for l in range(L):
        s = dict(x0=xc, h1=h)
        (s["a1"], s["pq1"]), got = _ffn_gu(h, W["gu1"], gather(l, "down1", "w_in", "w_out"))
        W["down1"], W["w_in"], W["w_out"] = map(full, got)
        (xc, h), got = _mm_res_norm(s["a1"], W["down1"][0], xc, mix_norm[l:l + 1], 0.5, gather(l, "gu2"))
        W["gu2"] = full(got[0])
        s["x1"], s["h2"] = xc, h
        (z, qkv), got = _z_proj(h, W["w_in"][0], gather(l, "down2"))
        W["down2"] = full(got[0])
        s["ws"] = sg_w[l].astype(BF16)
        s["wst"] = jnp.swapaxes(sg_w[l], 1, 2).astype(BF16)
        s["bb"] = jnp.broadcast_to(sg_b[l][:, :, None], (SG_HEADS, SG_CHUNK, HEAD_DIM))
        s["gn"] = sg_norm[l:l + 1]
        s["pw"] = pool_w[l].astype(BF16)
        s["psc"] = pool_scale[l:l + 1]
        s["ecat"] = _rpb_expand(na_rpb[l])
        (att,), _ = _attn_fwd(qkv, s["ecat"])
        mix = jnp.concatenate([_sg_fwd(z, s["ws"], s["bb"], s["gn"]), _pool_fwd(z, s["pw"], s["psc"]),
                               by_token(att)], axis=1)
        s["z"], s["qkv"], s["mix"] = z, qkv, mix
        (xc, h), _ = _mm_res_norm(mix, W["w_out"][0], xc, ffn2_norm[l:l + 1], 1.0)
        s["x2"], s["h3"] = xc, h
        (s["a2"], s["pq2"]), got = _ffn_gu(h, W["gu2"], gather(l + 1, "gu1"))
        s["W"] = W
        W = {"gu1": full(got[0])} if got else {}
        gnext = ffn1_norm[l + 1:l + 2] if l + 1 < L else final_norm.reshape(1, D)
        (xc, h), _ = _mm_res_norm(s["a2"], s["W"]["down2"][0], xc, gnext, 0.5)
        saved.append(s)

    loss_row, dx, dxb, dg_final = _loss_bwd(xc, final_norm.reshape(1, D), tgt)
    loss = lax.psum(loss_row[0, 0], MESH_AXES)

    received = {grp: [[None] * len(names) for _ in range(L)] for grp, names, _ in GROUPS}
    small = {n: [None] * L for n in SMALL}

    def slots(g, *groups):
        return g.reshape(g.shape[0], N_DEV, -1, D), groups or tuple(range(g.shape[0]))

    for l in reversed(range(L)):
        s = saved[l]
        W = s["W"]
        (dgu,), _ = _mm_nt(dxb, W["down2"][0], 0.5, pq=s["pq2"])
        (g_down2,), _ = _mm_tn(s["a2"][None], dxb, 0.5)
        (dx, dxb, dgn), got = _dh_norm_bwd(dgu, W["gu2"], s["x2"], ffn2_norm[l:l + 1], dx,
                                           _Carry(scatters=[slots(g_down2)]))
        received["down2"][l][0] = got[0]
        small["ffn2_norm"][l] = dgn
        (g_gu2,), _ = _mm_tn(dgu, s["h3"], 1.0)

        (dmix,), _ = _mm_nt(dxb, W["w_out"][0], 1.0)
        (g_out,), _ = _mm_tn(s["mix"][None], dxb, 1.0)
        dzu, dzv, dws, dbb, dgn = _sg_bwd(s["z"], dmix, s["ws"], s["wst"], s["bb"], s["gn"])
        dzp, dpw, dpsc = _pool_bwd(s["z"], dmix, s["pw"], s["psc"])
        (dq, dk, dv, decat), got = _attn_bwd(s["qkv"], dmix, s["ecat"], _Carry(scatters=[slots(g_gu2, 0)]))
        received["gu2"][l][0] = got[0]
        small["sg_w"][l], small["sg_b"][l], small["sg_norm"][l] = dws, dbb[:, :, 0], dgn[0]
        small["pool_w"][l], small["pool_scale"][l] = dpw, dpsc[0]
        small["na_rpb"][l] = _rpb_collect(decat)
        dz = jnp.concatenate([dzu, dzv, dzp, by_token(dq), by_token(dk), by_token(dv)], axis=1)
        (dx, dxb, dgn), got = _dh_norm_bwd(dz[None], W["w_in"], s["x1"], mix_norm[l:l + 1], dx,
                                           _Carry(scatters=[slots(g_gu2, 1)]))
        received["gu2"][l][1] = got[0]
        small["mix_norm"][l] = dgn
        (g_in,), got = _mm_tn(dz[None], s["h2"], 1.0, _Carry(scatters=[slots(g_out)]))
        received["w_out"][l][0] = got[0]

        (dgu,), _ = _mm_nt(dxb, W["down1"][0], 0.5, pq=s["pq1"])
        (g_down1,), got = _mm_tn(s["a1"][None], dxb, 0.5, _Carry(scatters=[slots(g_in)]))
        received["w_in"][l][0] = got[0]
        (g_gu1,), got = _mm_tn(dgu, s["h1"], 1.0, _Carry(scatters=[slots(g_down1)]))
        received["down1"][l][0] = got[0]
        (dx, dxb, dgn), got = _dh_norm_bwd(dgu, W["gu1"], s["x0"], ffn1_norm[l:l + 1], dx,
                                           _Carry(scatters=[slots(g_gu1, 0), slots(g_gu1, 1)]))
        received["gu1"][l][0], received["gu1"][l][1] = got
        small["ffn1_norm"][l] = dgn

    small_shapes = {n: given[n].shape for n in SMALL}
    small_shapes["final_norm"] = final_norm.shape
    flat = [jnp.stack([jnp.reshape(g, (-1,)) for g in small[n]]).reshape(-1) for n in SMALL] + [dg_final.reshape(-1)]
    sizes = [f.shape[0] for f in flat]
    total = sum(sizes)
    padded = -(-total // 2048) * 2048
    local = jnp.concatenate(flat + [jnp.zeros((padded - total,), F32)]).reshape(1, -1, 128)
    summed = _sum_devices(_exchange(_Carry(gathers=[local]), "gather_small_grads")[0]).reshape(-1)
    grads, off = {}, 0
    for n, size in zip(list(SMALL) + ["final_norm"], sizes):
        grads[n] = summed[off:off + size].reshape(small_shapes[n])
        off += size
    for grp, names, cols in GROUPS:
        for t, n in enumerate(names):
            grads[n] = None
            for l in range(L):
                grads[n] = _sum_devices_into(received[grp][l][t], grads[n], l, L, cols)

    names = ['ffn1_norm', 'ffn1_w_gate', 'ffn1_w_up', 'ffn1_w_down', 'mix_norm', 'w_in', 'sg_norm', 'sg_w', 'sg_b',
             'pool_w', 'pool_scale', 'na_rpb', 'w_out', 'ffn2_norm', 'ffn2_w_gate', 'ffn2_w_up', 'ffn2_w_down',
             'final_norm']
    delta, new_m, new_v = {}, {}, {}
    for n in names:
        delta[n], new_m[n], new_v[n] = _adamw(given[n], grads[n], given["m_" + n], given["v_" + n])
    return (loss, dx.reshape(1, T, D), *[grads[n] for n in names], *[delta[n] for n in names],
            *[new_m[n] for n in names], *[new_v[n] for n in names])
```

```python
import functools
import math

import jax
import jax.numpy as jnp
from jax import lax
from jax.experimental import pallas as pl
from jax.experimental.pallas import tpu as pltpu

F32 = jnp.float32
BF16 = jnp.bfloat16
EPS = 1e-6
NEG = -1e30

HEAD_DIM = 128
SG_WIDTH = 512
SG_HEADS = 4
SG_CHUNK = 128
POOL_WINDOWS = (2, 4, 8, 16)
POOL_WIDTH = 512
POOL_HALO = 128
NA_WIDTH = 1024
NA_HEADS = 8
NA_KH = 8
NA_KW = 16
GRID_W = 64
Z_COLS = 2 * SG_WIDTH + POOL_WIDTH + 3 * NA_WIDTH
Q_OFF = 2 * SG_WIDTH + POOL_WIDTH
K_OFF = Q_OFF + NA_WIDTH
V_OFF = K_OFF + NA_WIDTH

ADAM_LR = 0.001
ADAM_B1 = 0.9
ADAM_B2 = 0.999
ADAM_EPS = 1e-08
ADAM_WD = 0.01
ADAM_STEP = 10

N_DEV = 8
MESH_AXES = ("x", "y", "c")
MESH = pl.DeviceIdType.MESH
ANY = pl.BlockSpec(memory_space=pl.ANY)

NT_DIMS = (((1,), (1,)), ((), ()))
TN_DIMS = (((0,), (0,)), ((), ()))

ROWS_PREF = 1024
MXU_DEPTH = 256
K_PREF = 1408


def _pick(n, pref, mult):
    best = None
    t = mult
    while t <= min(n, pref):
        if n % t == 0:
            best = t
        t += mult
    return n if best is None else best


def _gelu(x):
    return 0.5 * x * (1.0 + lax.erf(x * (1.0 / math.sqrt(2.0))))


def _gelu_grad(x):
    cdf = 0.5 * (1.0 + lax.erf(x * (1.0 / math.sqrt(2.0))))
    pdf = jnp.exp(-0.5 * x * x) * (1.0 / math.sqrt(2.0 * math.pi))
    return cdf + x * pdf


def _rms(x):
    return lax.rsqrt(jnp.mean(x * x, axis=-1, keepdims=True) + EPS)


def _norm_bwd(dh, x, g):
    r = _rms(x)
    w = dh * g
    dx = r * w - x * (r * r * r) * jnp.mean(w * x, axis=-1, keepdims=True)
    dg = jnp.sum(dh * (x * r), axis=0, keepdims=True)
    return dx, dg


def _position():
    return lax.axis_index("x"), lax.axis_index("y"), lax.axis_index("c")


def _index(p):
    return 4 * p[0] + 2 * p[1] + p[2]


class _Carry:
    def __init__(self, gathers=(), scatters=()):
        self.gathers, self.scatters = list(gathers), list(scatters)
        self.units = sum(g.shape[0] for g in self.gathers) + sum(len(groups) for _, groups in self.scatters)

    def arrays(self):
        return self.gathers + [s for s, _ in self.scatters]

    def out_shapes(self):
        return ([jax.ShapeDtypeStruct((g.shape[0], N_DEV) + g.shape[1:], g.dtype) for g in self.gathers]
                + [jax.ShapeDtypeStruct((len(groups),) + s.shape[1:], s.dtype) for s, groups in self.scatters])

    def scratch(self):
        return [pltpu.SemaphoreType.DMA((7 * self.units,)), pltpu.SemaphoreType.DMA((7 * self.units,)),
                pltpu.SemaphoreType.DMA((self.units,))]

    def _gather_copies(self, n, x_ref, out_ref, send, recv, local):
        x, y, c = _position()
        me, sibling = (x, y, c), (x, y, 1 - c)
        chips = [(1 - x, y), (x, 1 - y), (1 - x, 1 - y)]

        def copy(k, block, to, src=None):
            dst = out_ref.at[_index(block)]
            return pltpu.make_async_remote_copy(
                src_ref=dst if src is None else src, dst_ref=dst, send_sem=send.at[7 * n + k],
                recv_sem=recv.at[7 * n + k], device_id=to, device_id_type=MESH)

        return dict(
            mine=pltpu.make_async_copy(x_ref, out_ref.at[_index(me)], local.at[n]),
            first=[copy(0, me, sibling, x_ref)] + [copy(1 + j, me, (*ch, c), x_ref) for j, ch in enumerate(chips)],
            landed=[copy(1 + j, (*ch, c), me) for j, ch in enumerate(chips)],
            passed=[copy(4 + j, (*ch, c), sibling) for j, ch in enumerate(chips)],
            from_sibling=[copy(0, sibling, me)] + [copy(4 + j, (*ch, 1 - c), me) for j, ch in enumerate(chips)])

    def _scatter_copies(self, n, src_ref, out_ref, send, recv, local):
        x, y, c = _position()
        me = (x, y, c)
        sends, recvs = [], []
        for k in range(1, N_DEV):
            flip = lambda v, bit: 1 - v if bit else v
            peer = (flip(x, k & 4), flip(y, k & 2), flip(c, k & 1))
            sems = dict(send_sem=send.at[7 * n + k - 1], recv_sem=recv.at[7 * n + k - 1], device_id=peer,
                        device_id_type=MESH)
            sends.append(pltpu.make_async_remote_copy(src_ref=src_ref.at[_index(peer)], dst_ref=out_ref.at[_index(me)],
                                                      **sems))
            recvs.append(pltpu.make_async_remote_copy(src_ref=src_ref.at[_index(me)], dst_ref=out_ref.at[_index(peer)],
                                                      **sems))
        mine = pltpu.make_async_copy(src_ref.at[_index(me)], out_ref.at[_index(me)], local.at[n])
        return dict(mine=mine, sends=sends, recvs=recvs)

    def _pieces(self, ins, outs, sems):
        send, recv, local = sems
        gs, ss, unit = [], [], 0
        for n, g in enumerate(self.gathers):
            for t in range(g.shape[0]):
                gs.append(self._gather_copies(unit, ins[n].at[t], outs[n].at[t], send, recv, local))
                unit += 1
        for n, (_, groups) in enumerate(self.scatters, start=len(self.gathers)):
            for j, t in enumerate(groups):
                ss.append(self._scatter_copies(unit, ins[n].at[t], outs[n].at[j], send, recv, local))
                unit += 1
        return gs, ss

    def start(self, ins, outs, sems):
        gs, ss = self._pieces(ins, outs, sems)
        for g in gs:
            g["mine"].start()
            for cp in g["first"]:
                cp.start()
        for s in ss:
            s["mine"].start()
            for cp in s["sends"]:
                cp.start()

    def forward(self, ins, outs, sems):
        gs, _ = self._pieces(ins, outs, sems)
        for g in gs:
            for landed, passed in zip(g["landed"], g["passed"]):
                landed.wait_recv()
                passed.start()

    def finish(self, ins, outs, sems):
        gs, ss = self._pieces(ins, outs, sems)
        for g in gs:
            for cp in g["from_sibling"]:
                cp.wait_recv()
            for cp in g["first"] + g["passed"]:
                cp.wait_send()
            g["mine"].wait()
        for s in ss:
            for cp in s["recvs"]:
                cp.wait_recv()
            for cp in s["sends"]:
                cp.wait_send()
            s["mine"].wait()


def _call(body, *, name, grid, in_specs, out_specs, out_shape, args, scratch_shapes=(), carry=None):
    if carry is None or not carry.arrays():
        outs = pl.pallas_call(
            body, name=name, grid=grid, in_specs=in_specs, out_specs=out_specs, out_shape=out_shape,
            scratch_shapes=list(scratch_shapes),
            compiler_params=pltpu.CompilerParams(dimension_semantics=("arbitrary",) * len(grid)))(*args)
        return list(outs), []
    n_in, n_out, n_scr, n_car = len(in_specs), len(out_specs), len(scratch_shapes), len(carry.arrays())
    steps = math.prod(grid)
    middle = (steps * 6) // 10

    def wrapped(*refs):
        ins, refs = refs[:n_in], refs[n_in:]
        cins, refs = refs[:n_car], refs[n_car:]
        outs, refs = refs[:n_out], refs[n_out:]
        couts, refs = refs[:n_car], refs[n_car:]
        scr, sems = refs[:n_scr], refs[n_scr:]
        step = 0
        for d, size in enumerate(grid):
            step = step * size + pl.program_id(d)

        @pl.when(step == 0)
        def _():
            carry.start(cins, couts, sems)

        body(*ins, *outs, *scr)

        if carry.gathers:
            @pl.when(step == middle)
            def _():
                carry.forward(cins, couts, sems)

        @pl.when(step == steps - 1)
        def _():
            carry.finish(cins, couts, sems)

    outs = pl.pallas_call(
        wrapped, name=name + "_carry", grid=grid, in_specs=list(in_specs) + [ANY] * n_car,
        out_specs=list(out_specs) + [ANY] * n_car, out_shape=list(out_shape) + carry.out_shapes(),
        scratch_shapes=list(scratch_shapes) + carry.scratch(),
        compiler_params=pltpu.CompilerParams(dimension_semantics=("arbitrary",) * len(grid)))(*args, *carry.arrays())
    return list(outs[:n_out]), list(outs[n_out:])


def _exchange(carry, name):
    n_car = len(carry.arrays())

    def body(*refs):
        cins, couts, sems = refs[:n_car], refs[n_car:2 * n_car], refs[2 * n_car:]
        carry.start(cins, couts, sems)
        if carry.gathers:
            carry.forward(cins, couts, sems)
        carry.finish(cins, couts, sems)

    return list(pl.pallas_call(body, name=name, in_specs=[ANY] * n_car, out_specs=[ANY] * n_car,
                               out_shape=carry.out_shapes(), scratch_shapes=carry.scratch())(*carry.arrays()))


def _rmsnorm(x, g):
    T, D = x.shape
    tm = _pick(T, 512, 16)

    def body(x_ref, g_ref, o_ref):
        xv = x_ref[...]
        o_ref[...] = (xv * _rms(xv) * g_ref[...]).astype(BF16)

    return _call(body, name="rmsnorm", grid=(T // tm,),
                 in_specs=[pl.BlockSpec((tm, D), lambda i: (i, 0)), pl.BlockSpec((1, D), lambda i: (0, 0))],
                 out_specs=[pl.BlockSpec((tm, D), lambda i: (i, 0))],
                 out_shape=[jax.ShapeDtypeStruct((T, D), BF16)], args=(x, g))[0][0]


def _ffn_gu(h, wgu, carry=None):
    T, D = h.shape
    F = wgu.shape[1]
    tm = _pick(T, ROWS_PREF, 16)
    tn = _pick(F, 512, 128)

    def body(h_ref, wg_ref, wu_ref, a_ref, pq_ref):
        hv = h_ref[...]
        g = lax.dot_general(hv, wg_ref[0], NT_DIMS, preferred_element_type=F32)
        u = lax.dot_general(hv, wu_ref[0], NT_DIMS, preferred_element_type=F32)
        sg = jax.nn.sigmoid(g)
        q = g * sg
        a_ref[...] = (q * u).astype(BF16)
        pq_ref[0] = (u * (sg * (1.0 + g * (1.0 - sg)))).astype(BF16)
        pq_ref[1] = q.astype(BF16)

    return _call(body, name="ffn_gu", grid=(T // tm, F // tn),
                 in_specs=[pl.BlockSpec((tm, D), lambda i, j: (i, 0)),
                           pl.BlockSpec((1, tn, D), lambda i, j: (0, j, 0)),
                           pl.BlockSpec((1, tn, D), lambda i, j: (1, j, 0))],
                 out_specs=[pl.BlockSpec((tm, tn), lambda i, j: (i, j)),
                            pl.BlockSpec((2, tm, tn), lambda i, j: (0, i, j))],
                 out_shape=[jax.ShapeDtypeStruct((T, F), BF16), jax.ShapeDtypeStruct((2, T, F), BF16)],
                 args=(h, wgu, wgu), carry=carry)


def _mm_nt(a, w, scale, pq=None, carry=None):
    T, K = a.shape
    N = w.shape[0]
    tm = _pick(T, ROWS_PREF, 16)
    tn = _pick(N, 512, 128)

    def body(*refs):
        a_ref, w_ref = refs[:2]
        d = lax.dot_general(a_ref[...], w_ref[...], NT_DIMS, preferred_element_type=F32)
        if scale != 1.0:
            d = d * scale
        if pq is None:
            refs[2][...] = d.astype(BF16)
        else:
            pq_ref, o_ref = refs[2:]
            o_ref[0] = (d * pq_ref[0].astype(F32)).astype(BF16)
            o_ref[1] = (d * pq_ref[1].astype(F32)).astype(BF16)

    in_specs = [pl.BlockSpec((tm, K), lambda i, j: (i, 0)), pl.BlockSpec((tn, K), lambda i, j: (j, 0))]
    if pq is None:
        args, out_spec, out_shape = (a, w), pl.BlockSpec((tm, tn), lambda i, j: (i, j)), (T, N)
    else:
        in_specs.append(pl.BlockSpec((2, tm, tn), lambda i, j: (0, i, j)))
        args, out_spec, out_shape = (a, w, pq), pl.BlockSpec((2, tm, tn), lambda i, j: (0, i, j)), (2, T, N)
    return _call(body, name="mm_nt" if pq is None else "ffn_da", grid=(T // tm, N // tn),
                 in_specs=in_specs, out_specs=[out_spec], out_shape=[jax.ShapeDtypeStruct(out_shape, BF16)],
                 args=args, carry=carry)


def _z_proj(h, wt, carry=None):
    T, K = h.shape
    tm = _pick(T, ROWS_PREF, 16)
    tn = 4 * HEAD_DIM
    flat = Q_OFF // tn

    def body(h_ref, w_ref, z_ref, qkv_ref):
        j = pl.program_id(1)
        zv = lax.dot_general(h_ref[...], w_ref[...], NT_DIMS, preferred_element_type=F32).astype(BF16)

        @pl.when(j < flat)
        def _():
            z_ref[...] = zv

        @pl.when(j >= flat)
        def _():
            for c in range(tn // HEAD_DIM):
                qkv_ref[c] = zv[:, c * HEAD_DIM:(c + 1) * HEAD_DIM]

    return _call(body, name="z_proj", grid=(T // tm, Z_COLS // tn),
                 in_specs=[pl.BlockSpec((tm, K), lambda i, j: (i, 0)), pl.BlockSpec((tn, K), lambda i, j: (j, 0))],
                 out_specs=[pl.BlockSpec((tm, tn), lambda i, j: (i, jnp.minimum(j, flat - 1))),
                            pl.BlockSpec((tn // HEAD_DIM, tm, HEAD_DIM), lambda i, j: (jnp.maximum(j - flat, 0), i, 0))],
                 out_shape=[jax.ShapeDtypeStruct((T, Q_OFF), BF16),
                            jax.ShapeDtypeStruct((3 * NA_HEADS, T, HEAD_DIM), BF16)],
                 args=(h, wt), carry=carry)


def _mm_res_norm(a, w, x, gnext, scale, carry=None):
    T, K = a.shape
    D = w.shape[1]
    tm = _pick(T, 512, 16)
    tk = _pick(K, K_PREF, 128)
    nk = K // tk

    def body(a_ref, w_ref, x_ref, g_ref, xo_ref, ho_ref, acc):
        k = pl.program_id(1)

        @pl.when(k == 0)
        def _():
            acc[...] = jnp.zeros_like(acc)

        acc[...] += jnp.dot(a_ref[...], w_ref[...], preferred_element_type=F32)

        @pl.when(k == nk - 1)
        def _():
            xn = x_ref[...] + scale * acc[...]
            xo_ref[...] = xn
            ho_ref[...] = (xn * _rms(xn) * g_ref[...]).astype(BF16)

    row = pl.BlockSpec((tm, D), lambda i, k: (i, 0))
    return _call(body, name="mm_res_norm", grid=(T // tm, nk),
                 in_specs=[pl.BlockSpec((tm, tk), lambda i, k: (i, k)), pl.BlockSpec((tk, D), lambda i, k: (k, 0)),
                           row, pl.BlockSpec((1, D), lambda i, k: (0, 0))],
                 out_specs=[row, row],
                 out_shape=[jax.ShapeDtypeStruct((T, D), F32), jax.ShapeDtypeStruct((T, D), BF16)],
                 scratch_shapes=[pltpu.VMEM((tm, D), F32)], args=(a, w, x, gnext), carry=carry)


def _mm_tn(a, b, scale, carry=None):
    G, T, M = a.shape
    N = b.shape[1]
    tm = _pick(M, 512, 128)
    tk = _pick(T, 2048, 16)
    nk = T // tk

    def body(a_ref, b_ref, o_ref, acc):
        k = pl.program_id(2)

        @pl.when(k == 0)
        def _():
            acc[...] = jnp.zeros_like(acc)

        acc[...] += lax.dot_general(a_ref[0], b_ref[...], TN_DIMS, preferred_element_type=F32)

        @pl.when(k == nk - 1)
        def _():
            o_ref[0] = (acc[...] * scale).astype(BF16)

    return _call(body, name="mm_tn", grid=(G, M // tm, nk),
                 in_specs=[pl.BlockSpec((1, tk, tm), lambda g, i, k: (g, k, i)),
                           pl.BlockSpec((tk, N), lambda g, i, k: (k, 0))],
                 out_specs=[pl.BlockSpec((1, tm, N), lambda g, i, k: (g, i, 0))],
                 out_shape=[jax.ShapeDtypeStruct((G, M, N), BF16)],
                 scratch_shapes=[pltpu.VMEM((tm, N), F32)], args=(a, b), carry=carry)


def _dh_norm_bwd(d, wt, x, g, dres, carry=None):
    T, D = x.shape
    G, _, K = d.shape
    tm = _pick(T, 512, 16)
    tk = _pick(K, 2 * MXU_DEPTH, MXU_DEPTH)
    nk = K // tk

    def body(*refs):
        dw_refs = refs[:2 * G]
        x_ref, g_ref, dres_ref, dx_ref, dxb_ref, dg_ref, acc = refs[2 * G:]
        i = pl.program_id(0)
        k = pl.program_id(1)

        @pl.when(k == 0)
        def _():
            acc[...] = jnp.zeros_like(acc)

        for n in range(G):
            acc[...] += jnp.dot(dw_refs[2 * n][0], dw_refs[2 * n + 1][0], preferred_element_type=F32)

        @pl.when(k == nk - 1)
        def _():
            dxn, dgp = _norm_bwd(acc[...], x_ref[...], g_ref[...])
            dxv = dres_ref[...] + dxn
            dx_ref[...] = dxv
            dxb_ref[...] = dxv.astype(BF16)

            @pl.when(i == 0)
            def _():
                dg_ref[...] = dgp

            @pl.when(i > 0)
            def _():
                dg_ref[...] += dgp

    row = pl.BlockSpec((tm, D), lambda i, k: (i, 0))
    vec = pl.BlockSpec((1, D), lambda i, k: (0, 0))
    pairs = []
    for n in range(G):
        pairs += [pl.BlockSpec((1, tm, tk), lambda i, k, n=n: (n, i, k)),
                  pl.BlockSpec((1, tk, D), lambda i, k, n=n: (n, k, 0))]
    return _call(body, name="dh_norm_bwd", grid=(T // tm, nk),
                 in_specs=pairs + [row, vec, row], out_specs=[row, row, vec],
                 out_shape=[jax.ShapeDtypeStruct((T, D), F32), jax.ShapeDtypeStruct((T, D), BF16),
                            jax.ShapeDtypeStruct((1, D), F32)],
                 scratch_shapes=[pltpu.VMEM((tm, D), F32)], args=(*[d, wt] * G, x, g, dres), carry=carry)


def _loss_bwd(x, g, tgt):
    T, D = x.shape
    tm = _pick(T, 512, 16)

    def body(x_ref, g_ref, t_ref, loss_ref, dx_ref, dxb_ref, dg_ref):
        i = pl.program_id(0)
        xv = x_ref[...]
        gv = g_ref[...]
        e = xv * _rms(xv) * gv - t_ref[...]
        part = jnp.sum(jnp.sum(e * e, axis=-1, keepdims=True), axis=0, keepdims=True) * (0.5 / D)
        dxn, dgp = _norm_bwd(e * (1.0 / D), xv, gv)
        dx_ref[...] = dxn
        dxb_ref[...] = dxn.astype(BF16)

        @pl.when(i == 0)
        def _():
            loss_ref[...] = jnp.broadcast_to(part, loss_ref.shape)
            dg_ref[...] = dgp

        @pl.when(i > 0)
        def _():
            loss_ref[...] += jnp.broadcast_to(part, loss_ref.shape)
            dg_ref[...] += dgp

    row = pl.BlockSpec((tm, D), lambda i: (i, 0))
    vec = pl.BlockSpec((1, D), lambda i: (0, 0))
    return _call(body, name="loss_bwd", grid=(T // tm,), in_specs=[row, vec, row],
                 out_specs=[pl.BlockSpec((1, 128), lambda i: (0, 0)), row, row, vec],
                 out_shape=[jax.ShapeDtypeStruct((1, 128), F32), jax.ShapeDtypeStruct((T, D), F32),
                            jax.ShapeDtypeStruct((T, D), BF16), jax.ShapeDtypeStruct((1, D), F32)],
                 args=(x, g, tgt))[0]


def _sg_fwd(z, ws, bb, gn):
    T = z.shape[0]

    def body(zu_ref, zv_ref, ws_ref, bb_ref, gn_ref, a_ref):
        for h in range(SG_HEADS):
            sl = slice(h * HEAD_DIM, (h + 1) * HEAD_DIM)
            gv = _gelu(zv_ref[:, sl].astype(F32))
            vn = (gv * _rms(gv) * gn_ref[:, sl]).astype(BF16)
            mixed = jnp.dot(ws_ref[h], vn, preferred_element_type=F32) + bb_ref[h]
            a_ref[:, sl] = (_gelu(zu_ref[:, sl].astype(F32)) * mixed).astype(BF16)

    full = lambda shape: pl.BlockSpec(shape, lambda n: (0,) * len(shape))
    return _call(body, name="sg_fwd", grid=(T // SG_CHUNK,),
                 in_specs=[pl.BlockSpec((SG_CHUNK, SG_WIDTH), lambda n: (n, 0)),
                           pl.BlockSpec((SG_CHUNK, SG_WIDTH), lambda n: (n, 1)),
                           full((SG_HEADS, SG_CHUNK, SG_CHUNK)), full((SG_HEADS, SG_CHUNK, HEAD_DIM)),
                           full((1, SG_WIDTH))],
                 out_specs=[pl.BlockSpec((SG_CHUNK, SG_WIDTH), lambda n: (n, 0))],
                 out_shape=[jax.ShapeDtypeStruct((T, SG_WIDTH), BF16)], args=(z, z, ws, bb, gn))[0][0]


def _sg_bwd(z, dmix, ws, wst, bb, gn):
    T = z.shape[0]

    def body(zu_ref, zv_ref, da_ref, ws_ref, wst_ref, bb_ref, gn_ref, dzu_ref, dzv_ref, dws_ref, dbb_ref, dgn_ref):
        n = pl.program_id(0)

        @pl.when(n == 0)
        def _():
            dws_ref[...] = jnp.zeros_like(dws_ref)
            dbb_ref[...] = jnp.zeros_like(dbb_ref)
            dgn_ref[...] = jnp.zeros_like(dgn_ref)

        for h in range(SG_HEADS):
            sl = slice(h * HEAD_DIM, (h + 1) * HEAD_DIM)
            u = zu_ref[:, sl].astype(F32)
            v = zv_ref[:, sl].astype(F32)
            da = da_ref[:, sl].astype(F32)
            gain = gn_ref[:, sl]
            gv = _gelu(v)
            r = _rms(gv)
            vn = (gv * r * gain).astype(BF16)
            mixed = jnp.dot(ws_ref[h], vn, preferred_element_type=F32) + bb_ref[h]
            dmixed = da * _gelu(u)
            dzu_ref[:, sl] = (da * mixed * _gelu_grad(u)).astype(BF16)
            dmb = dmixed.astype(BF16)
            dws_ref[h] += lax.dot_general(dmb, vn, NT_DIMS, preferred_element_type=F32)
            dbb_ref[h] += jnp.broadcast_to(jnp.sum(dmixed, axis=-1, keepdims=True), (SG_CHUNK, HEAD_DIM))
            dvn = jnp.dot(wst_ref[h], dmb, preferred_element_type=F32)
            dgv, dg = _norm_bwd(dvn, gv, gain)
            dgn_ref[:, sl] += dg
            dzv_ref[:, sl] = (dgv * _gelu_grad(v)).astype(BF16)

    full = lambda shape: pl.BlockSpec(shape, lambda n: (0,) * len(shape))
    wspec = full((SG_HEADS, SG_CHUNK, SG_CHUNK))
    tile = lambda c: pl.BlockSpec((SG_CHUNK, SG_WIDTH), lambda n: (n, c))
    return _call(body, name="sg_bwd", grid=(T // SG_CHUNK,),
                 in_specs=[tile(0), tile(1), tile(0), wspec, wspec, full((SG_HEADS, SG_CHUNK, HEAD_DIM)),
                           full((1, SG_WIDTH))],
                 out_specs=[tile(0), tile(0), wspec, full((SG_HEADS, SG_CHUNK, HEAD_DIM)), full((1, SG_WIDTH))],
                 out_shape=[jax.ShapeDtypeStruct((T, SG_WIDTH), BF16), jax.ShapeDtypeStruct((T, SG_WIDTH), BF16),
                            jax.ShapeDtypeStruct((SG_HEADS, SG_CHUNK, SG_CHUNK), F32),
                            jax.ShapeDtypeStruct((SG_HEADS, SG_CHUNK, HEAD_DIM), F32),
                            jax.ShapeDtypeStruct((1, SG_WIDTH), F32)], args=(z, z, dmix, ws, wst, bb, gn))[0]


def _pool_specs(T, tp, col):
    step = tp // POOL_HALO
    last = T // POOL_HALO - 1
    return [pl.BlockSpec((POOL_HALO, POOL_WIDTH), lambda i: (jnp.maximum(i * step - 1, 0), col)),
            pl.BlockSpec((tp, POOL_WIDTH), lambda i: (i, col)),
            pl.BlockSpec((POOL_HALO, POOL_WIDTH), lambda i: (jnp.minimum((i + 1) * step, last), col))]


def _pool_band(i, tp, T, win):
    ext = tp + 2 * POOL_HALO
    t = i * tp + lax.broadcasted_iota(jnp.int32, (tp, ext), 0)
    s = i * tp - POOL_HALO + lax.broadcasted_iota(jnp.int32, (tp, ext), 1)
    band = (s >= jnp.maximum(t - win // 2, 0)) & (s < jnp.minimum(t + win // 2, T))
    t1 = i * tp + lax.broadcasted_iota(jnp.int32, (tp, 1), 0)
    cnt = (jnp.minimum(t1 + win // 2, T) - jnp.maximum(t1 - win // 2, 0)).astype(F32)
    return band.astype(BF16), cnt


def _pool_fwd(z, pw, psc):
    T = z.shape[0]
    tp = _pick(T, 256, POOL_HALO)

    def body(pp_ref, pc_ref, pn_ref, w_ref, sc_ref, o_ref):
        i = pl.program_id(0)
        halo = jnp.concatenate([pp_ref[...], pc_ref[...], pn_ref[...]], axis=0)
        for g, win in enumerate(POOL_WINDOWS):
            sl = slice(g * HEAD_DIM, (g + 1) * HEAD_DIM)
            band, cnt = _pool_band(i, tp, T, win)
            ssum = jnp.dot(band, halo[:, sl], preferred_element_type=F32)
            d = ssum / cnt - pc_ref[:, sl].astype(F32)
            y = jnp.dot(d.astype(BF16), w_ref[g], preferred_element_type=F32) * sc_ref[:, sl]
            o_ref[:, sl] = y.astype(BF16)

    full = lambda shape: pl.BlockSpec(shape, lambda i: (0,) * len(shape))
    return _call(body, name="pool_fwd", grid=(T // tp,),
                 in_specs=_pool_specs(T, tp, 2) + [full((4, HEAD_DIM, HEAD_DIM)), full((1, POOL_WIDTH))],
                 out_specs=[pl.BlockSpec((tp, POOL_WIDTH), lambda i: (i, 0))],
                 out_shape=[jax.ShapeDtypeStruct((T, POOL_WIDTH), BF16)], args=(z, z, z, pw, psc))[0][0]


def _pool_bwd(z, dmix, pw, psc):
    T = z.shape[0]
    tp = _pick(T, 256, POOL_HALO)
    ext = tp + 2 * POOL_HALO

    def body(pp_ref, pc_ref, pn_ref, dp_ref, dc_ref, dn_ref, w_ref, sc_ref, dz_ref, dw_ref, dsc_ref):
        i = pl.program_id(0)

        @pl.when(i == 0)
        def _():
            dw_ref[...] = jnp.zeros_like(dw_ref)
            dsc_ref[...] = jnp.zeros_like(dsc_ref)

        halo = jnp.concatenate([pp_ref[...], pc_ref[...], pn_ref[...]], axis=0)
        dy_halo = jnp.concatenate([dp_ref[...], dc_ref[...], dn_ref[...]], axis=0)
        th = i * tp - POOL_HALO + lax.broadcasted_iota(jnp.int32, (ext, 1), 0)
        inside = (th >= 0) & (th < T)
        s2 = i * tp + lax.broadcasted_iota(jnp.int32, (tp, ext), 0)
        t2 = i * tp - POOL_HALO + lax.broadcasted_iota(jnp.int32, (tp, ext), 1)
        for g, win in enumerate(POOL_WINDOWS):
            sl = slice(g * HEAD_DIM, (g + 1) * HEAD_DIM)
            sc = sc_ref[:, sl]
            band, cnt = _pool_band(i, tp, T, win)
            ssum = jnp.dot(band, halo[:, sl], preferred_element_type=F32)
            db = (ssum / cnt - pc_ref[:, sl].astype(F32)).astype(BF16)
            yraw = jnp.dot(db, w_ref[g], preferred_element_type=F32)
            dyc = dc_ref[:, sl].astype(F32)
            dsc_ref[:, sl] += jnp.sum(dyc * yraw, axis=0, keepdims=True)
            dw_ref[g] += lax.dot_general(db, (dyc * sc).astype(BF16), TN_DIMS, preferred_element_type=F32)
            dd = lax.dot_general((dy_halo[:, sl].astype(F32) * sc).astype(BF16), w_ref[g], NT_DIMS,
                                 preferred_element_type=F32)
            cnt_h = (jnp.minimum(th + win // 2, T) - jnp.maximum(th - win // 2, 0)).astype(F32)
            ddc = jnp.where(inside, dd / jnp.maximum(cnt_h, 1.0), 0.0)
            hi = ddc.astype(BF16)
            lo = (ddc - hi.astype(F32)).astype(BF16)
            band_t = ((s2 >= jnp.maximum(t2 - win // 2, 0)) & (s2 < jnp.minimum(t2 + win // 2, T))).astype(BF16)
            dpool = (jnp.dot(band_t, hi, preferred_element_type=F32) + jnp.dot(band_t, lo, preferred_element_type=F32)
                     - dd[POOL_HALO:POOL_HALO + tp])
            dz_ref[:, sl] = dpool.astype(BF16)

    full = lambda shape: pl.BlockSpec(shape, lambda i: (0,) * len(shape))
    return _call(body, name="pool_bwd", grid=(T // tp,),
                 in_specs=_pool_specs(T, tp, 2) + _pool_specs(T, tp, 1)
                 + [full((4, HEAD_DIM, HEAD_DIM)), full((1, POOL_WIDTH))],
                 out_specs=[pl.BlockSpec((tp, POOL_WIDTH), lambda i: (i, 0)), full((4, HEAD_DIM, HEAD_DIM)),
                            full((1, POOL_WIDTH))],
                 out_shape=[jax.ShapeDtypeStruct((T, POOL_WIDTH), BF16),
                            jax.ShapeDtypeStruct((4, HEAD_DIM, HEAD_DIM), F32),
                            jax.ShapeDtypeStruct((1, POOL_WIDTH), F32)], args=(z, z, z, dmix, dmix, dmix, pw, psc))[0]


ATT_ROWS = 16
WIN_KEYS = NA_KH * GRID_W


def _col_mask():
    q = lax.broadcasted_iota(jnp.int32, (GRID_W, WIN_KEYS), 0)
    k = lax.broadcasted_iota(jnp.int32, (GRID_W, WIN_KEYS), 1) & (GRID_W - 1)
    start = jnp.clip(q - NA_KW // 2, 0, GRID_W - NA_KW)
    return (k >= start) & (k < start + NA_KW)


def _softmax(s, bias, mask):
    s = jnp.where(mask, s * (HEAD_DIM ** -0.5) + bias, NEG)
    p = jnp.exp(s - jnp.max(s, axis=-1, keepdims=True))
    return p / jnp.sum(p, axis=-1, keepdims=True)


def _attn_window(step, a, rows):
    r = step * ATT_ROWS + a
    sr = jnp.clip(r - NA_KH // 2, 0, rows - NA_KH)
    return pl.ds(pl.multiple_of(sr * GRID_W, GRID_W), WIN_KEYS), sr - r + NA_KH - 1


def _head_specs(T, blk):
    whole = lambda first: pl.BlockSpec((1, T, HEAD_DIM), lambda h, s: (first + h, 0, 0))
    return [pl.BlockSpec((1, blk, HEAD_DIM), lambda h, s: (h, s, 0)), whole(NA_HEADS), whole(2 * NA_HEADS)]


def _attn_fwd(qkv, ecat, carry=None):
    T = qkv.shape[1]
    rows = T // GRID_W
    blk = ATT_ROWS * GRID_W

    def body(q_ref, k_ref, v_ref, e_ref, o_ref, s_scr, p_scr):
        step = pl.program_id(1)
        mask = _col_mask()
        wins = [_attn_window(step, a, rows) for a in range(ATT_ROWS)]
        qs = [slice(a * GRID_W, (a + 1) * GRID_W) for a in range(ATT_ROWS)]
        for a, (win, _) in enumerate(wins):
            s_scr[a] = lax.dot_general(q_ref[0, qs[a], :], k_ref[0, win, :], NT_DIMS, preferred_element_type=F32)
        for a, (_, dr0) in enumerate(wins):
            p_scr[a] = _softmax(s_scr[a], e_ref[0, dr0], mask).astype(BF16)
        for a, (win, _) in enumerate(wins):
            o_ref[0, qs[a], :] = jnp.dot(p_scr[a], v_ref[0, win, :], preferred_element_type=F32).astype(BF16)

    return _call(body, name="attn_fwd", grid=(NA_HEADS, rows // ATT_ROWS),
                 in_specs=_head_specs(T, blk)
                 + [pl.BlockSpec((1, NA_KH, GRID_W, WIN_KEYS), lambda h, s: (h, 0, 0, 0))],
                 out_specs=[pl.BlockSpec((1, blk, HEAD_DIM), lambda h, s: (h, s, 0))],
                 out_shape=[jax.ShapeDtypeStruct((NA_HEADS, T, HEAD_DIM), BF16)],
                 scratch_shapes=[pltpu.VMEM((ATT_ROWS, GRID_W, WIN_KEYS), F32),
                                 pltpu.VMEM((ATT_ROWS, GRID_W, WIN_KEYS), BF16)],
                 args=(qkv, qkv, qkv, ecat), carry=carry)


def _attn_bwd(qkv, dmix, ecat, carry=None):
    T = qkv.shape[1]
    rows = T // GRID_W
    blk = ATT_ROWS * GRID_W
    nstep = rows // ATT_ROWS

    def body(q_ref, k_ref, v_ref, do_ref, e_ref, dq_ref, dk_ref, dv_ref, de_ref, dk_acc, dv_acc, s_scr, dp_scr,
             p_scr, ds_scr):
        step = pl.program_id(1)
        mask = _col_mask()

        @pl.when(step == 0)
        def _():
            dk_acc[...] = jnp.zeros_like(dk_acc)
            dv_acc[...] = jnp.zeros_like(dv_acc)
            de_ref[...] = jnp.zeros_like(de_ref)

        wins = [_attn_window(step, a, rows) for a in range(ATT_ROWS)]
        qs = [slice(a * GRID_W, (a + 1) * GRID_W) for a in range(ATT_ROWS)]
        for a, (win, _) in enumerate(wins):
            s_scr[a] = lax.dot_general(q_ref[0, qs[a], :], k_ref[0, win, :], NT_DIMS, preferred_element_type=F32)
            dp_scr[a] = lax.dot_general(do_ref[qs[a], :], v_ref[0, win, :], NT_DIMS, preferred_element_type=F32)
        for a, (_, dr0) in enumerate(wins):
            pr = _softmax(s_scr[a], e_ref[0, dr0], mask)
            dp = dp_scr[a]
            ds = pr * (dp - jnp.sum(dp * pr, axis=-1, keepdims=True))
            de_ref[0, dr0] += ds
            p_scr[a] = pr.astype(BF16)
            ds_scr[a] = (ds * (HEAD_DIM ** -0.5)).astype(BF16)
        for a, (win, _) in enumerate(wins):
            dq_ref[0, qs[a], :] = jnp.dot(ds_scr[a], k_ref[0, win, :], preferred_element_type=F32).astype(BF16)
            dv_acc[win, :] += lax.dot_general(p_scr[a], do_ref[qs[a], :], TN_DIMS, preferred_element_type=F32)
            dk_acc[win, :] += lax.dot_general(ds_scr[a], q_ref[0, qs[a], :], TN_DIMS, preferred_element_type=F32)

        @pl.when(step == nstep - 1)
        def _():
            dk_ref[0] = dk_acc[...].astype(BF16)
            dv_ref[0] = dv_acc[...].astype(BF16)

    whole = pl.BlockSpec((1, T, HEAD_DIM), lambda h, s: (h, 0, 0))
    e_spec = pl.BlockSpec((1, NA_KH, GRID_W, WIN_KEYS), lambda h, s: (h, 0, 0, 0))
    out = jax.ShapeDtypeStruct((NA_HEADS, T, HEAD_DIM), BF16)
    stage = lambda dtype: pltpu.VMEM((ATT_ROWS, GRID_W, WIN_KEYS), dtype)
    return _call(body, name="attn_bwd", grid=(NA_HEADS, nstep),
                 in_specs=_head_specs(T, blk)
                 + [pl.BlockSpec((blk, HEAD_DIM), lambda h, s: (s, (SG_WIDTH + POOL_WIDTH) // HEAD_DIM + h)), e_spec],
                 out_specs=[pl.BlockSpec((1, blk, HEAD_DIM), lambda h, s: (h, s, 0)), whole, whole, e_spec],
                 out_shape=[out, out, out, jax.ShapeDtypeStruct((NA_HEADS, NA_KH, GRID_W, WIN_KEYS), F32)],
                 scratch_shapes=[pltpu.VMEM((T, HEAD_DIM), F32), pltpu.VMEM((T, HEAD_DIM), F32),
                                 stage(F32), stage(F32), stage(BF16), stage(BF16)],
                 args=(qkv, qkv, qkv, dmix, ecat), carry=carry)


def _rpb_tables():
    col = jnp.arange(GRID_W)
    dc = jnp.clip(col[None, :] - col[:, None] + NA_KW - 1, 0, 2 * NA_KW - 2)
    by_col = (dc[None] == jnp.arange(2 * NA_KW - 1)[:, None, None]).astype(F32)
    d, j = jnp.arange(NA_KH)[:, None], jnp.arange(NA_KH)[None, :]
    by_row = (jnp.arange(2 * NA_KH - 1)[:, None, None] == (d + j)[None]).astype(F32)
    return by_col, by_row


def _rpb_expand(rpb):
    by_col, by_row = _rpb_tables()
    e = jnp.einsum("hrc,cqk->hrqk", rpb, by_col, precision=lax.Precision.HIGHEST)
    ecat = jnp.einsum("hrqk,rdj->hdqjk", e, by_row, precision=lax.Precision.HIGHEST)
    return ecat.reshape(NA_HEADS, NA_KH, GRID_W, WIN_KEYS)


def _rpb_collect(decat):
    by_col, by_row = _rpb_tables()
    de = jnp.einsum("hdqjk,rdj->hrqk", decat.reshape(NA_HEADS, NA_KH, GRID_W, NA_KH, GRID_W), by_row,
                    precision=lax.Precision.HIGHEST)
    return jnp.einsum("hrqk,cqk->hrc", de, by_col, precision=lax.Precision.HIGHEST)


def _adamw(w, g, m, v):
    shape = w.shape
    C = shape[-1]
    R = w.size // C
    tr = _pick(R, max(8, (1 << 18) // C), 8)
    args = [a.reshape(R, C) for a in (w, g, m, v)]

    def body(w_ref, g_ref, m_ref, v_ref, d_ref, mo_ref, vo_ref):
        gv = g_ref[...]
        mn = ADAM_B1 * m_ref[...] + (1.0 - ADAM_B1) * gv
        vn = ADAM_B2 * v_ref[...] + (1.0 - ADAM_B2) * (gv * gv)
        m_hat = mn / (1.0 - ADAM_B1 ** ADAM_STEP)
        v_hat = vn / (1.0 - ADAM_B2 ** ADAM_STEP)
        d_ref[...] = -ADAM_LR * (m_hat / (jnp.sqrt(v_hat) + ADAM_EPS) + ADAM_WD * w_ref[...])
        mo_ref[...] = mn
        vo_ref[...] = vn

    spec = pl.BlockSpec((tr, C), lambda i: (i, 0))
    out = jax.ShapeDtypeStruct((R, C), F32)
    res = _call(body, name="adamw", grid=(R // tr,), in_specs=[spec] * 4, out_specs=[spec] * 3, out_shape=[out] * 3,
                args=args)[0]
    return [r.reshape(shape) for r in res]


def _sum_devices(g):
    G, _, R, C = g.shape
    tr = _pick(R, max(16, (1 << 18) // C), 16)

    def body(g_ref, o_ref):
        acc = g_ref[0, 0].astype(F32)
        for k in range(1, N_DEV):
            acc = acc + g_ref[0, k].astype(F32)
        o_ref[0] = acc

    return _call(body, name="sum_devices", grid=(G, R // tr),
                 in_specs=[pl.BlockSpec((1, N_DEV, tr, C), lambda t, i: (t, 0, i, 0))],
                 out_specs=[pl.BlockSpec((1, tr, C), lambda t, i: (t, i, 0))],
                 out_shape=[jax.ShapeDtypeStruct((G, R, C), F32)], args=(g,))[0][0]


def _sum_devices_into(g, prev, l, L, transpose):
    _, _, R, C = g.shape
    if transpose:
        tc = _pick(C, 256, 128)
        grid, shape = (C // tc,), (L, C, R)
        in_spec = pl.BlockSpec((1, N_DEV, R, tc), lambda i: (0, 0, 0, i))
        out_spec = pl.BlockSpec((1, tc, R), lambda i: (l, i, 0))
    else:
        tr = _pick(R, max(16, (1 << 18) // C), 16)
        grid, shape = (R // tr,), (L, R, C)
        in_spec = pl.BlockSpec((1, N_DEV, tr, C), lambda i: (0, 0, i, 0))
        out_spec = pl.BlockSpec((1, tr, C), lambda i: (l, i, 0))

    def body(g_ref, *rest):
        acc = g_ref[0, 0].astype(F32)
        for k in range(1, N_DEV):
            acc = acc + g_ref[0, k].astype(F32)
        rest[-1][0] = acc.T if transpose else acc

    first = prev is None
    return pl.pallas_call(
        body, name="sum_devices_into", grid=grid, in_specs=[in_spec] if first else [in_spec, ANY],
        out_specs=out_spec, out_shape=jax.ShapeDtypeStruct(shape, F32),
        input_output_aliases={} if first else {1: 0},
        compiler_params=pltpu.CompilerParams(dimension_semantics=("arbitrary",)))(*((g,) if first else (g, prev)))


SMALL = ("ffn1_norm", "mix_norm", "sg_norm", "sg_w", "sg_b", "pool_w", "pool_scale", "na_rpb", "ffn2_norm")
GROUPS = (("gu1", ("ffn1_w_gate", "ffn1_w_up"), True), ("down1", ("ffn1_w_down",), False), ("w_in", ("w_in",), True),
          ("w_out", ("w_out",), False), ("gu2", ("ffn2_w_gate", "ffn2_w_up"), True), ("down2", ("ffn2_w_down",), False))


def kernel(x, ffn1_norm, ffn1_w_gate, ffn1_w_up, ffn1_w_down, mix_norm, w_in, sg_norm, sg_w, sg_b, pool_w, pool_scale, na_rpb, w_out, ffn2_norm, ffn2_w_gate, ffn2_w_up, ffn2_w_down, final_norm, loss_target, m_ffn1_norm, m_ffn1_w_gate, m_ffn1_w_up, m_ffn1_w_down, m_mix_norm, m_w_in, m_sg_norm, m_sg_w, m_sg_b, m_pool_w, m_pool_scale, m_na_rpb, m_w_out, m_ffn2_norm, m_ffn2_w_gate, m_ffn2_w_up, m_ffn2_w_down, m_final_norm, v_ffn1_norm, v_ffn1_w_gate, v_ffn1_w_up, v_ffn1_w_down, v_mix_norm, v_w_in, v_sg_norm, v_sg_w, v_sg_b, v_pool_w, v_pool_scale, v_na_rpb, v_w_out, v_ffn2_norm, v_ffn2_w_gate, v_ffn2_w_up, v_ffn2_w_down, v_final_norm):
    given = dict(locals())
    T, D = x.shape[1], x.shape[2]
    L = ffn1_norm.shape[0]
    assert x.shape[0] == 1 and D == SG_WIDTH + POOL_WIDTH + NA_WIDTH and w_in.shape[2] * N_DEV == Z_COLS
    assert T % (ATT_ROWS * GRID_W) == 0 and T // GRID_W >= NA_KH
    x0 = x.reshape(T, D)
    tgt = loss_target.reshape(T, D)
    members = {grp: (names, cols) for grp, names, cols in GROUPS}

    def shard(grp, l):
        names, cols = members[grp]
        return jnp.stack([(given[n][l].T if cols else given[n][l]).astype(BF16) for n in names])

    def gather(l, *grps):
        return _Carry(gathers=[shard(grp, l) for grp in grps]) if l < L else None

    def full(gathered):
        return gathered.reshape(gathered.shape[0], -1, D)

    def by_token(heads):
        return heads.transpose(1, 0, 2).reshape(T, NA_WIDTH)

    W = {"gu1": full(_exchange(gather(0, "gu1"), "gather_first")[0])}
    saved = []
    xc = x0
    h = _rmsnorm(xc, ffn1_norm[0:1])
    for l in range(L):
        s = dict(x0=xc, h1=h)
        (s["a1"], s["pq1"]), got = _ffn_gu(h, W["gu1"], gather(l, "down1", "w_in", "w_out"))
        W["down1"], W["w_in"], W["w_out"] = map(full, got)
        (xc, h), got = _mm_res_norm(s["a1"], W["down1"][0], xc, mix_norm[l:l + 1], 0.5, gather(l, "gu2"))
        W["gu2"] = full(got[0])
        s["x1"], s["h2"] = xc, h
        (z, qkv), got = _z_proj(h, W["w_in"][0], gather(l, "down2"))
        W["down2"] = full(got[0])
        s["ws"] = sg_w[l].astype(BF16)
        s["wst"] = jnp.swapaxes(sg_w[l], 1, 2).astype(BF16)
        s["bb"] = jnp.broadcast_to(sg_b[l][:, :, None], (SG_HEADS, SG_CHUNK, HEAD_DIM))
        s["gn"] = sg_norm[l:l + 1]
        s["pw"] = pool_w[l].astype(BF16)
        s["psc"] = pool_scale[l:l + 1]
        s["ecat"] = _rpb_expand(na_rpb[l])
        (att,), _ = _attn_fwd(qkv, s["ecat"])
        mix = jnp.concatenate([_sg_fwd(z, s["ws"], s["bb"], s["gn"]), _pool_fwd(z, s["pw"], s["psc"]),
                               by_token(att)], axis=1)
        s["z"], s["qkv"], s["mix"] = z, qkv, mix
        (xc, h), _ = _mm_res_norm(mix, W["w_out"][0], xc, ffn2_norm[l:l + 1], 1.0)
        s["x2"], s["h3"] = xc, h
        (s["a2"], s["pq2"]), got = _ffn_gu(h, W["gu2"], gather(l + 1, "gu1"))
        s["W"] = W
        W = {"gu1": full(got[0])} if got else {}
        gnext = ffn1_norm[l + 1:l + 2] if l + 1 < L else final_norm.reshape(1, D)
        (xc, h), _ = _mm_res_norm(s["a2"], s["W"]["down2"][0], xc, gnext, 0.5)
        saved.append(s)

    loss_row, dx, dxb, dg_final = _loss_bwd(xc, final_norm.reshape(1, D), tgt)
    loss = lax.psum(loss_row[0, 0], MESH_AXES)

    received = {grp: [[None] * len(names) for _ in range(L)] for grp, names, _ in GROUPS}
    small = {n: [None] * L for n in SMALL}

    def slots(g, *groups):
        return g.reshape(g.shape[0], N_DEV, -1, D), groups or tuple(range(g.shape[0]))

    for l in reversed(range(L)):
        s = saved[l]
        W = s["W"]
        (dgu,), _ = _mm_nt(dxb, W["down2"][0], 0.5, pq=s["pq2"])
        (g_down2,), _ = _mm_tn(s["a2"][None], dxb, 0.5)
        (dx, dxb, dgn), got = _dh_norm_bwd(dgu, W["gu2"], s["x2"], ffn2_norm[l:l + 1], dx,
                                           _Carry(scatters=[slots(g_down2)]))
        received["down2"][l][0] = got[0]
        small["ffn2_norm"][l] = dgn
        (g_gu2,), _ = _mm_tn(dgu, s["h3"], 1.0)

        (dmix,), _ = _mm_nt(dxb, W["w_out"][0], 1.0)
        (g_out,), _ = _mm_tn(s["mix"][None], dxb, 1.0)
        dzu, dzv, dws, dbb, dgn = _sg_bwd(s["z"], dmix, s["ws"], s["wst"], s["bb"], s["gn"])
        dzp, dpw, dpsc = _pool_bwd(s["z"], dmix, s["pw"], s["psc"])
        (dq, dk, dv, decat), got = _attn_bwd(s["qkv"], dmix, s["ecat"], _Carry(scatters=[slots(g_gu2, 0)]))
        received["gu2"][l][0] = got[0]
        small["sg_w"][l], small["sg_b"][l], small["sg_norm"][l] = dws, dbb[:, :, 0], dgn[0]
        small["pool_w"][l], small["pool_scale"][l] = dpw, dpsc[0]
        small["na_rpb"][l] = _rpb_collect(decat)
        dz = jnp.concatenate([dzu, dzv, dzp, by_token(dq), by_token(dk), by_token(dv)], axis=1)
        (dx, dxb, dgn), got = _dh_norm_bwd(dz[None], W["w_in"], s["x1"], mix_norm[l:l + 1], dx,
                                           _Carry(scatters=[slots(g_gu2, 1)]))
        received["gu2"][l][1] = got[0]
        small["mix_norm"][l] = dgn
        (g_in,), got = _mm_tn(dz[None], s["h2"], 1.0, _Carry(scatters=[slots(g_out)]))
        received["w_out"][l][0] = got[0]

        (dgu,), _ = _mm_nt(dxb, W["down1"][0], 0.5, pq=s["pq1"])
        (g_down1,), got = _mm_tn(s["a1"][None], dxb, 0.5, _Carry(scatters=[slots(g_in)]))
        received["w_in"][l][0] = got[0]
        (g_gu1,), got = _mm_tn(dgu, s["h1"], 1.0, _Carry(scatters=[slots(g_down1)]))
        received["down1"][l][0] = got[0]
        (dx, dxb, dgn), got = _dh_norm_bwd(dgu, W["gu1"], s["x0"], ffn1_norm[l:l + 1], dx,
                                           _Carry(scatters=[slots(g_gu1, 0), slots(g_gu1, 1)]))
        received["gu1"][l][0], received["gu1"][l][1] = got
        small["ffn1_norm"][l] = dgn

    small_shapes = {n: given[n].shape for n in SMALL}
    small_shapes["final_norm"] = final_norm.shape
    flat = [jnp.stack([jnp.reshape(g, (-1,)) for g in small[n]]).reshape(-1) for n in SMALL] + [dg_final.reshape(-1)]
    sizes = [f.shape[0] for f in flat]
    total = sum(sizes)
    padded = -(-total // 2048) * 2048
    local = jnp.concatenate(flat + [jnp.zeros((padded - total,), F32)]).reshape(1, -1, 128)
    summed = _sum_devices(_exchange(_Carry(gathers=[local]), "gather_small_grads")[0]).reshape(-1)
    grads, off = {}, 0
    for n, size in zip(list(SMALL) + ["final_norm"], sizes):
        grads[n] = summed[off:off + size].reshape(small_shapes[n])
        off += size
    for grp, names, cols in GROUPS:
        for t, n in enumerate(names):
            grads[n] = None
            for l in range(L):
                grads[n] = _sum_devices_into(received[grp][l][t], grads[n], l, L, cols)

    names = ['ffn1_norm', 'ffn1_w_gate', 'ffn1_w_up', 'ffn1_w_down', 'mix_norm', 'w_in', 'sg_norm', 'sg_w', 'sg_b',
             'pool_w', 'pool_scale', 'na_rpb', 'w_out', 'ffn2_norm', 'ffn2_w_gate', 'ffn2_w_up', 'ffn2_w_down',
             'final_norm']
    delta, new_m, new_v = {}, {}, {}
    for n in names:
        delta[n], new_m[n], new_v[n] = _adamw(given[n], grads[n], given["m_" + n], given["v_" + n])
    return (loss, dx.reshape(1, T, D), *[grads[n] for n in names], *[delta[n] for n in names],
            *[new_m[n] for n in names], *[new_v[n] for n in names])
```
